```python
import jax, jax.numpy as jnp
from jax import lax
import numpy as np

D_MODEL = 1024
BATCH = 4
SEQ = 4096
DEPTH = 2

MIX_WIDTH = D_MODEL
ATTN_WIDTH = MIX_WIDTH // 2
RET_WIDTH = MIX_WIDTH - ATTN_WIDTH
HEAD_DIM = 64
N_Q_HEADS = ATTN_WIDTH // HEAD_DIM
N_KV_HEADS = 2
GQA_GROUP = N_Q_HEADS // N_KV_HEADS
WINDOW = 128
BLOCK = 128
RET_HEADS = 4
RET_DK = RET_WIDTH // RET_HEADS
RET_DV = RET_WIDTH // RET_HEADS
CHUNK = 128
D_FF = 4 * D_MODEL
EPS = 1e-6
NEG_INF = -1e30
IN_SPLITS = (ATTN_WIDTH, N_KV_HEADS * HEAD_DIM, N_KV_HEADS * HEAD_DIM, RET_HEADS * RET_DK, RET_HEADS * RET_DK, RET_HEADS * RET_DV, RET_WIDTH)
IN_WIDTH = ATTN_WIDTH + 2 * N_KV_HEADS * HEAD_DIM + 2 * RET_HEADS * RET_DK + RET_HEADS * RET_DV + RET_WIDTH

kernel_name = 'hybrid_swa_sink_retention_adaln_block'


def rms_norm(x, gain):
    xf = x.astype(jnp.float32)
    y = xf * lax.rsqrt(jnp.mean(xf * xf, axis=-1, keepdims=True) + EPS)
    return (y * gain.astype(jnp.float32)).astype(x.dtype)


def alibi_slopes(n):
    return jnp.exp2(-8.0 * jnp.arange(1, n + 1, dtype=jnp.float32) / n)


def with_previous_block(t):
    prev = jnp.concatenate([jnp.zeros_like(t[:, :1]), t[:, :-1]], axis=1)
    return jnp.concatenate([prev, t], axis=2)


def sliding_window_sink_attention(q, k, v, q_gain, k_gain, sinks):
    b, s = q.shape[0], q.shape[1]
    nb = s // BLOCK
    q = rms_norm(q, q_gain).reshape(b, nb, BLOCK, N_KV_HEADS, GQA_GROUP, HEAD_DIM)
    k = with_previous_block(rms_norm(k, k_gain).reshape(b, nb, BLOCK, N_KV_HEADS, HEAD_DIM))
    v = with_previous_block(v.reshape(b, nb, BLOCK, N_KV_HEADS, HEAD_DIM))
    scores = jnp.einsum('bnqkgd,bnskd->bnkgqs', q, k, preferred_element_type=jnp.float32) * (HEAD_DIM ** -0.5)
    q_pos = jnp.arange(BLOCK)[:, None]
    k_pos = jnp.arange(2 * BLOCK)[None, :]
    dist = q_pos + BLOCK - k_pos
    valid = (dist >= 0) & (dist < WINDOW)
    valid = valid[None] & ((jnp.arange(nb)[:, None, None] > 0) | (k_pos[None] >= BLOCK))
    bias = -alibi_slopes(N_Q_HEADS)[:, None, None] * dist.astype(jnp.float32)
    scores = scores + bias.reshape(N_KV_HEADS, GQA_GROUP, BLOCK, 2 * BLOCK)
    scores = jnp.where(valid[None, :, None, None], scores, NEG_INF)
    sink = jnp.broadcast_to(sinks.astype(jnp.float32).reshape(N_KV_HEADS, GQA_GROUP, 1, 1), scores.shape[:-1] + (1,))
    probs = jax.nn.softmax(jnp.concatenate([scores, sink], axis=-1), axis=-1)[..., :-1]
    out = jnp.einsum('bnkgqs,bnskd->bnqkgd', probs.astype(v.dtype), v)
    return out.reshape(b, s, ATTN_WIDTH)


def multiscale_retention(q, k, v, g, norm_gain):
    b, s = q.shape[0], q.shape[1]
    nc = s // CHUNK
    f32 = jnp.float32
    qc = q.astype(f32).reshape(b, nc, CHUNK, RET_HEADS, RET_DK)
    kc = k.astype(f32).reshape(b, nc, CHUNK, RET_HEADS, RET_DK) * (RET_DK ** -0.5)
    vc = v.astype(f32).reshape(b, nc, CHUNK, RET_HEADS, RET_DV)
    log_gamma = jnp.log1p(-jnp.exp2(-5.0 - jnp.arange(RET_HEADS, dtype=f32)))
    idx = jnp.arange(CHUNK, dtype=f32)
    rel = idx[:, None] - idx[None, :]
    decay_in = jnp.where(rel >= 0, jnp.exp(log_gamma[:, None, None] * jnp.maximum(rel, 0.0)), 0.0)
    q_decay = jnp.exp(log_gamma[None, :] * (idx[:, None] + 1.0))
    k_decay = jnp.exp(log_gamma[None, :] * (CHUNK - 1.0 - idx[:, None]))
    chunk_decay = jnp.exp(log_gamma * CHUNK)
    inner = jnp.einsum('bnihd,bnjhd->bnhij', qc, kc) * decay_in
    o_inner = jnp.einsum('bnhij,bnjhe->bnihe', inner, vc)
    kv = jnp.einsum('bnjhd,bnjhe->bnhde', kc * k_decay[:, :, None], vc)

    def step(state, kv_n):
        return chunk_decay[None, :, None, None] * state + kv_n, state

    _, prev = lax.scan(step, jnp.zeros((b, RET_HEADS, RET_DK, RET_DV), f32), jnp.moveaxis(kv, 1, 0))
    prev = jnp.moveaxis(prev, 0, 1)
    o_cross = jnp.einsum('bnihd,bnhde->bnihe', qc, prev) * q_decay[:, :, None]
    o = (o_inner + o_cross).reshape(b, s, RET_HEADS, RET_DV)
    mu = jnp.mean(o, axis=-1, keepdims=True)
    var = jnp.mean(jnp.square(o - mu), axis=-1, keepdims=True)
    o = ((o - mu) * lax.rsqrt(var + EPS)).reshape(b, s, RET_WIDTH) * norm_gain.astype(f32)
    return (jax.nn.silu(g.astype(f32)) * o).astype(q.dtype)


def hybrid_layer(x, c_act, norm1_g, norm2_g, w_ada, b_ada, w_in, q_norm_g, k_norm_g, sinks, ret_norm_g, w_out, w_mlp1, w_mlp2):
    b, s = x.shape[0], x.shape[1]
    ada = (c_act @ w_ada + b_ada)[:, None, :]
    shift1, scale1, gate1, shift2, scale2, gate2 = jnp.split(ada, 6, axis=-1)
    h = rms_norm(x, norm1_g) * (1.0 + scale1) + shift1
    proj = h @ w_in
    offsets = np.cumsum(IN_SPLITS)[:-1].tolist()
    aq, ak, av, rq, rk, rv, rg = jnp.split(proj, offsets, axis=-1)
    attn = sliding_window_sink_attention(
        aq.reshape(b, s, N_Q_HEADS, HEAD_DIM),
        ak.reshape(b, s, N_KV_HEADS, HEAD_DIM),
        av.reshape(b, s, N_KV_HEADS, HEAD_DIM),
        q_norm_g, k_norm_g, sinks)
    ret = multiscale_retention(
        rq.reshape(b, s, RET_HEADS, RET_DK),
        rk.reshape(b, s, RET_HEADS, RET_DK),
        rv.reshape(b, s, RET_HEADS, RET_DV),
        rg, ret_norm_g)
    mixed = jnp.concatenate([attn.astype(x.dtype), ret.astype(x.dtype)], axis=-1) @ w_out
    x = x + gate1 * mixed
    h = rms_norm(x, norm2_g) * (1.0 + scale2) + shift2
    x = x + gate2 * (jnp.square(jax.nn.relu(h @ w_mlp1)) @ w_mlp2)
    return x


def setup_inputs(seed: int = 0) -> dict:
    key = jax.random.key(seed)
    ks = jax.random.split(key, 14)
    f32 = jnp.float32

    def normal(k, shape, scale):
        return jax.random.normal(k, shape, f32) * scale

    return {
        'x': normal(ks[0], (BATCH, SEQ, D_MODEL), 1.0),
        'c': normal(ks[1], (BATCH, D_MODEL), 1.0),
        'norm1_g': 1.0 + normal(ks[2], (DEPTH, D_MODEL), 0.05),
        'norm2_g': 1.0 + normal(ks[3], (DEPTH, D_MODEL), 0.05),
        'w_ada': normal(ks[4], (DEPTH, D_MODEL, 6 * D_MODEL), 0.5 * D_MODEL ** -0.5),
        'b_ada': normal(ks[5], (DEPTH, 6 * D_MODEL), 0.02),
        'w_in': normal(ks[6], (DEPTH, D_MODEL, IN_WIDTH), D_MODEL ** -0.5),
        'q_norm_g': 1.0 + normal(ks[7], (DEPTH, HEAD_DIM), 0.05),
        'k_norm_g': 1.0 + normal(ks[8], (DEPTH, HEAD_DIM), 0.05),
        'sinks': normal(ks[9], (DEPTH, N_Q_HEADS), 0.5),
        'ret_norm_g': 1.0 + normal(ks[10], (DEPTH, RET_WIDTH), 0.05),
        'w_out': normal(ks[11], (DEPTH, MIX_WIDTH, D_MODEL), MIX_WIDTH ** -0.5),
        'w_mlp1': normal(ks[12], (DEPTH, D_MODEL, D_FF), D_MODEL ** -0.5),
        'w_mlp2': normal(ks[13], (DEPTH, D_FF, D_MODEL), D_FF ** -0.5),
    }


def reference(x, c, norm1_g, norm2_g, w_ada, b_ada, w_in, q_norm_g, k_norm_g, sinks, ret_norm_g, w_out, w_mlp1, w_mlp2):
    c_act = jax.nn.silu(c)
    for layer in range(DEPTH):
        x = hybrid_layer(x, c_act, norm1_g[layer], norm2_g[layer], w_ada[layer], b_ada[layer], w_in[layer],
                         q_norm_g[layer], k_norm_g[layer], sinks[layer], ret_norm_g[layer], w_out[layer],
                         w_mlp1[layer], w_mlp2[layer])
    return x
```

```python
import functools

import numpy as np
import jax
import jax.numpy as jnp
from jax import lax
from jax.experimental import pallas as pl
from jax.experimental.pallas import tpu as pltpu

D_MODEL = 1024
DEPTH = 2
ATTN_WIDTH = 512
RET_WIDTH = 512
HEAD_DIM = 64
N_Q_HEADS = 8
N_KV_HEADS = 2
N_PAIRS = N_Q_HEADS // 2
BLOCK = 128
RET_HEADS = 4
RET_DK = 128
D_FF = 4 * D_MODEL
EPS = 1e-6
NEG_INF = -1e30
IN_WIDTH = 2816
OFF_AQ, OFF_AK, OFF_AV, OFF_RQ, OFF_RK, OFF_RV, OFF_RG = 0, 512, 640, 768, 1280, 1792, 2304

MIX_ROWS = 512
MLP_ROWS = 512
FF_CHUNK = 1024
ADA_COLS = 1536
VMEM_LIMIT = 56 * 1024 * 1024

F32 = jnp.float32
BF16 = jnp.bfloat16


@functools.lru_cache(maxsize=None)
def _constant_tables():
    q_pos = np.arange(BLOCK)[:, None]
    k_pos = np.arange(2 * BLOCK)[None, :]
    dist = q_pos + BLOCK - k_pos
    valid = (dist >= 0) & (dist < BLOCK)
    valid_first = valid & (k_pos >= BLOCK)
    slopes = np.exp2(-8.0 * np.arange(1, N_Q_HEADS + 1, dtype=np.float64) / N_Q_HEADS)
    bias = -slopes[:, None, None] * dist[None].astype(np.float64)
    tabs = []
    for v in (valid_first, valid):
        per_head = np.where(v[None], bias, NEG_INF)
        tabs.append(per_head.reshape(N_PAIRS, 2, BLOCK, 2 * BLOCK).transpose(0, 2, 1, 3)
                    .reshape(N_PAIRS, BLOCK, 4 * BLOCK))
    bias_tab = np.stack(tabs).astype(np.float32)

    log_gamma = np.log1p(-np.exp2(-5.0 - np.arange(RET_HEADS, dtype=np.float64)))
    idx = np.arange(BLOCK, dtype=np.float64)
    rel = idx[:, None] - idx[None, :]
    k_scale = RET_DK ** -0.5
    decay_in = np.where(rel >= 0, np.exp(log_gamma[:, None, None] * np.maximum(rel, 0.0)), 0.0) * k_scale
    q_decay = np.exp(log_gamma[:, None] * (idx[None, :] + 1.0))
    k_decay = np.exp(log_gamma[:, None] * (BLOCK - 1.0 - idx[None, :])) * k_scale
    q_decay = np.broadcast_to(q_decay[:, :, None], (RET_HEADS, BLOCK, BLOCK))
    k_decay = np.broadcast_to(k_decay[:, :, None], (RET_HEADS, BLOCK, BLOCK))
    chunk_decay = np.exp(log_gamma * BLOCK)
    return (bias_tab, decay_in.astype(np.float32), np.ascontiguousarray(q_decay, np.float32),
            np.ascontiguousarray(k_decay, np.float32), chunk_decay.astype(np.float32))


def _ada_kernel(c_ref, w_ref, b_ref, o_ref):
    c = c_ref[...]
    c_act = c * (1.0 / (1.0 + jnp.exp(-c)))
    o_ref[...] = jnp.dot(c_act.astype(BF16), w_ref[...].astype(BF16),
                         preferred_element_type=F32) + b_ref[...]


def _ada_call(c_pad, w_ada, b_ada):
    rows = c_pad.shape[0]
    n = w_ada.shape[-1]
    return pl.pallas_call(
        _ada_kernel,
        grid=(DEPTH, n // ADA_COLS),
        in_specs=[
            pl.BlockSpec((rows, D_MODEL), lambda l, j: (0, 0)),
            pl.BlockSpec((None, D_MODEL, ADA_COLS), lambda l, j: (l, 0, j)),
            pl.BlockSpec((None, 1, ADA_COLS), lambda l, j: (l, 0, j)),
        ],
        out_specs=pl.BlockSpec((None, rows, ADA_COLS), lambda l, j: (l, 0, j)),
        out_shape=jax.ShapeDtypeStruct((DEPTH, rows, n), F32),
        compiler_params=pltpu.CompilerParams(
            dimension_semantics=("arbitrary", "arbitrary"), vmem_limit_bytes=VMEM_LIMIT),
        name="ada",
    )(c_pad, w_ada, b_ada.reshape(DEPTH, 1, n))


def _mix_kernel(x_ref, ada_ref, g1_ref, win_ref, gq_ref, gk_ref, sinks_ref, rg_ref, wout_ref,
                bias_ref, din_ref, qdec_ref, kdec_ref, cdec_ref,
                o_ref, proj_ref, mixed_ref, k2_ref, v2_ref, s_ref):
    t = pl.program_id(1)

    @pl.when(t == 0)
    def _():
        k2_ref[...] = jnp.zeros_like(k2_ref)
        v2_ref[...] = jnp.zeros_like(v2_ref)
        s_ref[...] = jnp.zeros_like(s_ref)

    x = x_ref[...]
    ms = jnp.mean(x * x, axis=-1, keepdims=True)
    h = x * lax.rsqrt(ms + EPS) * g1_ref[...]
    h = h * (1.0 + ada_ref[1:2, :]) + ada_ref[0:1, :]
    proj_ref[...] = jnp.dot(h.astype(BF16), win_ref[...], preferred_element_type=F32)

    lane = lax.broadcasted_iota(jnp.int32, (BLOCK, 2 * HEAD_DIM), 1)
    lo = lane < HEAD_DIM

    def head_norm(a, gain_row):
        a2 = a * a
        s_lo = jnp.sum(jnp.where(lo, a2, 0.0), axis=-1, keepdims=True)
        s_hi = jnp.sum(jnp.where(lo, 0.0, a2), axis=-1, keepdims=True)
        mean_sq = jnp.where(lo, s_lo, s_hi) * (1.0 / HEAD_DIM)
        return a * lax.rsqrt(mean_sq + EPS) * gain_row

    def block_body(j, carry):
        r0 = pl.multiple_of(j * BLOCK, BLOCK)
        rows = pl.ds(r0, BLOCK)
        tab = jnp.where(jnp.logical_and(t == 0, j == 0), 0, 1)

        kn = head_norm(proj_ref[rows, OFF_AK:OFF_AK + 128], gk_ref[...])
        kn_t = kn.T.astype(BF16)
        av = proj_ref[rows, OFF_AV:OFF_AV + 128]
        av_rot = pltpu.roll(av, HEAD_DIM, axis=1)
        for g in range(N_KV_HEADS):
            k2_ref[g, 0:64, 0:128] = k2_ref[g, 0:64, 128:256]
            k2_ref[g, 64:128, 256:384] = k2_ref[g, 64:128, 384:512]
            kt = kn_t[64 * g:64 * g + 64, :]
            k2_ref[g, 0:64, 128:256] = kt
            k2_ref[g, 64:128, 384:512] = kt
            v2_ref[g, 0:128, :] = v2_ref[g, 128:256, :]
            v2_ref[g, 256:384, :] = v2_ref[g, 384:512, :]
            top = av if g == 0 else av_rot
            bot = av_rot if g == 0 else av
            v2_ref[g, 128:256, :] = jnp.where(lo, top, 0.0).astype(BF16)
            v2_ref[g, 384:512, :] = jnp.where(lo, 0.0, bot).astype(BF16)

        gq = gq_ref[...] * (HEAD_DIM ** -0.5)
        for g in range(N_KV_HEADS):
            qs = []
            for pp in range(2):
                p = 2 * g + pp
                qn = head_norm(proj_ref[rows, OFF_AQ + 128 * p:OFF_AQ + 128 * p + 128], gq)
                qs.append(qn.astype(BF16))
            s_all = jnp.dot(jnp.concatenate(qs, axis=0), k2_ref[g], preferred_element_type=F32)
            probs, inv_l = [], []
            for pp in range(2):
                p = 2 * g + pp
                sp = s_all[128 * pp:128 * pp + 128, :] + bias_ref[tab, p]
                es, ls = [], []
                for hh in range(2):
                    sink = sinks_ref[2 * p + hh]
                    sh = sp[:, 256 * hh:256 * hh + 256]
                    m = jnp.maximum(jnp.max(sh, axis=-1, keepdims=True), sink)
                    e = jnp.exp(sh - m)
                    ls.append(jnp.sum(e, axis=-1, keepdims=True) + jnp.exp(sink - m))
                    es.append(e.astype(BF16))
                probs.append(jnp.concatenate(es, axis=1))
                inv_l.append(jnp.where(lo, 1.0 / ls[0], 1.0 / ls[1]))
            o_all = jnp.dot(jnp.concatenate(probs, axis=0), v2_ref[g], preferred_element_type=F32)
            for pp in range(2):
                p = 2 * g + pp
                o = o_all[128 * pp:128 * pp + 128, :] * inv_l[pp]
                mixed_ref[rows, 128 * p:128 * p + 128] = o.astype(BF16)

        for hd in range(RET_HEADS):
            c0 = 128 * hd
            q = proj_ref[rows, OFF_RQ + c0:OFF_RQ + c0 + 128].astype(BF16)
            k = proj_ref[rows, OFF_RK + c0:OFF_RK + c0 + 128]
            v = proj_ref[rows, OFF_RV + c0:OFF_RV + c0 + 128].astype(BF16)
            gate = proj_ref[rows, OFF_RG + c0:OFF_RG + c0 + 128]
            inner = lax.dot_general(q, k.astype(BF16), (((1,), (1,)), ((), ())),
                                    preferred_element_type=F32) * din_ref[hd]
            state = s_ref[hd]
            o = (jnp.dot(inner.astype(BF16), v, preferred_element_type=F32)
                 + jnp.dot(q, state.astype(BF16), preferred_element_type=F32) * qdec_ref[hd])
            kd = (k * kdec_ref[hd]).astype(BF16)
            kv = lax.dot_general(kd, v, (((0,), (0,)), ((), ())), preferred_element_type=F32)
            s_ref[hd] = cdec_ref[hd] * state + kv
            mu = jnp.mean(o, axis=-1, keepdims=True)
            d = o - mu
            var = jnp.mean(d * d, axis=-1, keepdims=True)
            on = d * lax.rsqrt(var + EPS) * rg_ref[:, c0:c0 + 128]
            y = gate * (1.0 / (1.0 + jnp.exp(-gate))) * on
            mixed_ref[rows, ATTN_WIDTH + c0:ATTN_WIDTH + c0 + 128] = y.astype(BF16)
        return carry

    lax.fori_loop(0, MIX_ROWS // BLOCK, block_body, 0)

    y = jnp.dot(mixed_ref[...], wout_ref[...], preferred_element_type=F32)
    o_ref[...] = x_ref[...] + ada_ref[2:3, :] * y


def _mix_call(x, ada_l, g1, w_in, gq, gk, sinks, rg, w_out, tables):
    b, s, d = x.shape
    bias_tab, din, qdec, kdec, cdec = tables
    const2 = lambda i, j: (0, 0)
    const3 = lambda i, j: (0, 0, 0)
    const4 = lambda i, j: (0, 0, 0, 0)
    smem = pl.BlockSpec(memory_space=pltpu.SMEM)
    return pl.pallas_call(
        _mix_kernel,
        grid=(b, s // MIX_ROWS),
        in_specs=[
            pl.BlockSpec((None, MIX_ROWS, d), lambda i, j: (i, j, 0)),
            pl.BlockSpec((None, 6, d), lambda i, j: (i, 0, 0)),
            pl.BlockSpec((1, d), const2),
            pl.BlockSpec((d, IN_WIDTH), const2),
            pl.BlockSpec((1, 128), const2),
            pl.BlockSpec((1, 128), const2),
            smem,
            pl.BlockSpec((1, RET_WIDTH), const2),
            pl.BlockSpec((d, d), const2),
            pl.BlockSpec(bias_tab.shape, const4),
            pl.BlockSpec(din.shape, const3),
            pl.BlockSpec(qdec.shape, const3),
            pl.BlockSpec(kdec.shape, const3),
            smem,
        ],
        out_specs=pl.BlockSpec((None, MIX_ROWS, d), lambda i, j: (i, j, 0)),
        out_shape=jax.ShapeDtypeStruct((b, s, d), F32),
        scratch_shapes=[
            pltpu.VMEM((MIX_ROWS, IN_WIDTH), F32),
            pltpu.VMEM((MIX_ROWS, d), BF16),
            pltpu.VMEM((N_KV_HEADS, 128, 512), BF16),
            pltpu.VMEM((N_KV_HEADS, 512, 128), BF16),
            pltpu.VMEM((RET_HEADS, 128, 128), F32),
        ],
        compiler_params=pltpu.CompilerParams(
            dimension_semantics=("arbitrary", "arbitrary"), vmem_limit_bytes=VMEM_LIMIT),
        name="mix",
    )(x, ada_l, g1, w_in, gq, gk, sinks, rg, w_out, bias_tab, din, qdec, kdec, cdec)


def _mlp_kernel(x_ref, ada_ref, g2_ref, w1_ref, w2_ref, o_ref):
    x = x_ref[...]
    ms = jnp.mean(x * x, axis=-1, keepdims=True)
    h = x * lax.rsqrt(ms + EPS) * g2_ref[...]
    h = (h * (1.0 + ada_ref[4:5, :]) + ada_ref[3:4, :]).astype(BF16)
    acc = None
    for c in range(D_FF // FF_CHUNK):
        cols = slice(c * FF_CHUNK, (c + 1) * FF_CHUNK)
        a = jnp.dot(h, w1_ref[:, cols], preferred_element_type=F32)
        a = jnp.maximum(a, 0.0)
        part = jnp.dot((a * a).astype(BF16), w2_ref[cols, :], preferred_element_type=F32)
        acc = part if acc is None else acc + part
    o_ref[...] = x + ada_ref[5:6, :] * acc


def _mlp_call(x, ada_l, g2, w1, w2):
    b, s, d = x.shape
    const2 = lambda i, j: (0, 0)
    return pl.pallas_call(
        _mlp_kernel,
        grid=(b, s // MLP_ROWS),
        in_specs=[
            pl.BlockSpec((None, MLP_ROWS, d), lambda i, j: (i, j, 0)),
            pl.BlockSpec((None, 6, d), lambda i, j: (i, 0, 0)),
            pl.BlockSpec((1, d), const2),
            pl.BlockSpec((d, D_FF), const2, pipeline_mode=pl.Buffered(1)),
            pl.BlockSpec((D_FF, d), const2, pipeline_mode=pl.Buffered(1)),
        ],
        out_specs=pl.BlockSpec((None, MLP_ROWS, d), lambda i, j: (i, j, 0)),
        out_shape=jax.ShapeDtypeStruct((b, s, d), F32),
        compiler_params=pltpu.CompilerParams(
            dimension_semantics=("arbitrary", "arbitrary"), vmem_limit_bytes=VMEM_LIMIT),
        name="mlp",
    )(x, ada_l, g2, w1, w2)


def kernel(x, c, norm1_g, norm2_g, w_ada, b_ada, w_in, q_norm_g, k_norm_g, sinks, ret_norm_g,
           w_out, w_mlp1, w_mlp2):
    b = x.shape[0]
    tables = tuple(jnp.asarray(t) for t in _constant_tables())
    c_pad = jnp.pad(c, ((0, 8 - b), (0, 0)))
    ada = _ada_call(c_pad, w_ada, b_ada)[:, :b].reshape(DEPTH, b, 6, D_MODEL)
    for l in range(DEPTH):
        x = _mix_call(
            x, ada[l], norm1_g[l][None, :], w_in[l].astype(BF16),
            jnp.tile(q_norm_g[l], 2)[None, :], jnp.tile(k_norm_g[l], 2)[None, :],
            sinks[l], ret_norm_g[l][None, :], w_out[l].astype(BF16), tables)
        x = _mlp_call(x, ada[l], norm2_g[l][None, :], w_mlp1[l].astype(BF16), w_mlp2[l].astype(BF16))
    return x
```

```python
import functools

import numpy as np
import jax
import jax.numpy as jnp
from jax import lax
from jax.experimental import pallas as pl
from jax.experimental.pallas import tpu as pltpu

D_MODEL = 1024
DEPTH = 2
ATTN_WIDTH = 512
RET_WIDTH = 512
HEAD_DIM = 64
N_Q_HEADS = 8
N_KV_HEADS = 2
N_PAIRS = N_Q_HEADS // 2
BLOCK = 128
RET_HEADS = 4
RET_DK = 128
D_FF = 4 * D_MODEL
EPS = 1e-6
NEG_INF = -1e30
IN_WIDTH = 2816
OFF_AQ, OFF_AK, OFF_AV, OFF_RQ, OFF_RK, OFF_RV, OFF_RG = 0, 512, 640, 768, 1280, 1792, 2304

MIX_ROWS = 512
PROJ_CHUNK = 256
FF_CHUNK = 1024
ADA_COLS = 1536
VMEM_LIMIT = 60 * 1024 * 1024

F32 = jnp.float32
BF16 = jnp.bfloat16


@functools.lru_cache(maxsize=None)
def _constant_tables():
    q_pos = np.arange(BLOCK)[:, None]
    k_pos = np.arange(2 * BLOCK)[None, :]
    dist = q_pos + BLOCK - k_pos
    valid = (dist >= 0) & (dist < BLOCK)
    valid_first = valid & (k_pos >= BLOCK)
    slopes = np.exp2(-8.0 * np.arange(1, N_Q_HEADS + 1, dtype=np.float64) / N_Q_HEADS)
    bias = -slopes[:, None, None] * dist[None].astype(np.float64)
    bias_tab = np.stack([np.where(v[None], bias, NEG_INF) for v in (valid_first, valid)])
    bias_tab = bias_tab.astype(np.float32)

    log_gamma = np.log1p(-np.exp2(-5.0 - np.arange(RET_HEADS, dtype=np.float64)))
    idx = np.arange(BLOCK, dtype=np.float64)
    rel = idx[:, None] - idx[None, :]
    k_scale = RET_DK ** -0.5
    decay_in = np.where(rel >= 0, np.exp(log_gamma[:, None, None] * np.maximum(rel, 0.0)), 0.0) * k_scale
    q_decay = np.exp(log_gamma[:, None] * (idx[None, :] + 1.0))
    k_decay = np.exp(log_gamma[:, None] * (BLOCK - 1.0 - idx[None, :])) * k_scale
    q_decay = np.broadcast_to(q_decay[:, :, None], (RET_HEADS, BLOCK, BLOCK))
    k_decay = np.broadcast_to(k_decay[:, :, None], (RET_HEADS, BLOCK, BLOCK))
    chunk_decay = np.exp(log_gamma * BLOCK)
    return (bias_tab, decay_in.astype(np.float32), np.ascontiguousarray(q_decay, np.float32),
            np.ascontiguousarray(k_decay, np.float32), chunk_decay.astype(np.float32))


def _ada_kernel(c_ref, w_ref, b_ref, o_ref):
    c = c_ref[...]
    c_act = c * (1.0 / (1.0 + jnp.exp(-c)))
    o_ref[...] = jnp.dot(c_act.astype(BF16), w_ref[...].astype(BF16),
                         preferred_element_type=F32) + b_ref[...]


def _ada_call(c_pad, w_ada, b_ada):
    rows = c_pad.shape[0]
    n = w_ada.shape[-1]
    return pl.pallas_call(
        _ada_kernel,
        grid=(DEPTH, n // ADA_COLS),
        in_specs=[
            pl.BlockSpec((rows, D_MODEL), lambda l, j: (0, 0)),
            pl.BlockSpec((None, D_MODEL, ADA_COLS), lambda l, j: (l, 0, j)),
            pl.BlockSpec((None, 1, ADA_COLS), lambda l, j: (l, 0, j)),
        ],
        out_specs=pl.BlockSpec((None, rows, ADA_COLS), lambda l, j: (l, 0, j)),
        out_shape=jax.ShapeDtypeStruct((DEPTH, rows, n), F32),
        compiler_params=pltpu.CompilerParams(
            dimension_semantics=("arbitrary", "arbitrary"), vmem_limit_bytes=VMEM_LIMIT),
        name="ada",
    )(c_pad, w_ada, b_ada.reshape(DEPTH, 1, n))


def _mix_slot(chunks, proj_ref, mixed_ref, kwin_ref, kwin_prev, vwin_ref, vwin_prev,
              first_tab, first_keep, gq_ref, gk_ref, sinks_ref, rg_ref,
              bias_ref, din_ref, qdec_ref, kdec_ref, cdec_ref, s_ref):
    chunks = list(chunks)

    def issue_chunk():
        if chunks:
            chunks.pop(0)()

    def head_norm(a, gain_row):
        lo = lax.broadcasted_iota(jnp.int32, a.shape, 1) < HEAD_DIM
        a2 = a * a
        s_lo = jnp.sum(jnp.where(lo, a2, 0.0), axis=-1, keepdims=True)
        s_hi = jnp.sum(jnp.where(lo, 0.0, a2), axis=-1, keepdims=True)
        mean_sq = jnp.where(lo, s_lo, s_hi) * (1.0 / HEAD_DIM)
        return a * lax.rsqrt(mean_sq + EPS) * gain_row

    issue_chunk()

    kn = head_norm(proj_ref[:, OFF_AK:OFF_AK + 128], gk_ref[...])
    kn_t = kn.T.astype(BF16)
    av = proj_ref[:, OFF_AV:OFF_AV + 128]
    av_rot = pltpu.roll(av, HEAD_DIM, axis=1)
    lo_t = lax.broadcasted_iota(jnp.int32, av.shape, 1) < HEAD_DIM
    v_parts = (jnp.where(lo_t, av, 0.0), jnp.where(lo_t, 0.0, av_rot),
               jnp.where(lo_t, av_rot, 0.0), jnp.where(lo_t, 0.0, av))
    for g in range(N_KV_HEADS):
        kwin_ref[g, :, 0:BLOCK] = kwin_prev[g, :, MIX_ROWS:MIX_ROWS + BLOCK]
        kt = kn_t[64 * g:64 * g + 64, :]
        kwin_ref[g, 0:64, BLOCK:BLOCK + MIX_ROWS] = kt
        kwin_ref[g, 64:128, BLOCK:BLOCK + MIX_ROWS] = kt
    for n in range(2 * N_KV_HEADS):
        vwin_ref[n, 0:BLOCK, :] = vwin_prev[n, MIX_ROWS:MIX_ROWS + BLOCK, :]
        vwin_ref[n, BLOCK:BLOCK + MIX_ROWS, :] = v_parts[n].astype(BF16)

    lo = lax.broadcasted_iota(jnp.int32, (BLOCK, 2 * HEAD_DIM), 1) < HEAD_DIM
    gq = gq_ref[...] * (HEAD_DIM ** -0.5)
    for j in range(MIX_ROWS // BLOCK):
        rows = pl.ds(j * BLOCK, BLOCK)
        win = slice(j * BLOCK, (j + 2) * BLOCK)
        tab = first_tab if j == 0 else 1

        for g in range(N_KV_HEADS):
            if j > 0 or g > 0:
                issue_chunk()
            qs = []
            for pp in range(2):
                p = 2 * g + pp
                qn = head_norm(proj_ref[rows, OFF_AQ + 128 * p:OFF_AQ + 128 * p + 128], gq)
                qs.append(jnp.where(lo, qn, 0.0).astype(BF16))
                qs.append(jnp.where(lo, 0.0, qn).astype(BF16))
            s_all = jnp.dot(jnp.concatenate(qs, axis=0), kwin_ref[g, :, win],
                            preferred_element_type=F32)
            es, ls = [], []
            for hh in range(4):
                head = 4 * g + hh
                sink = sinks_ref[head]
                sh = s_all[128 * hh:128 * hh + 128, :] + bias_ref[tab, head]
                m = jnp.maximum(jnp.max(sh, axis=-1, keepdims=True), sink)
                e = jnp.exp(sh - m)
                ls.append(jnp.sum(e, axis=-1, keepdims=True) + jnp.exp(sink - m))
                es.append(e.astype(BF16))
            o_all = (jnp.dot(jnp.concatenate([es[0], es[2]], axis=0), vwin_ref[2 * g, win, :],
                             preferred_element_type=F32)
                     + jnp.dot(jnp.concatenate([es[1], es[3]], axis=0), vwin_ref[2 * g + 1, win, :],
                               preferred_element_type=F32))
            for pp in range(2):
                p = 2 * g + pp
                inv_l = jnp.where(lo, 1.0 / ls[2 * pp], 1.0 / ls[2 * pp + 1])
                o = o_all[128 * pp:128 * pp + 128, :] * inv_l
                mixed_ref[rows, 128 * p:128 * p + 128] = o.astype(BF16)

        for hd in range(RET_HEADS):
            if hd % 2 == 0:
                issue_chunk()
            c0 = 128 * hd
            q = proj_ref[rows, OFF_RQ + c0:OFF_RQ + c0 + 128].astype(BF16)
            k = proj_ref[rows, OFF_RK + c0:OFF_RK + c0 + 128]
            v = proj_ref[rows, OFF_RV + c0:OFF_RV + c0 + 128].astype(BF16)
            gate = proj_ref[rows, OFF_RG + c0:OFF_RG + c0 + 128]
            inner = lax.dot_general(q, k.astype(BF16), (((1,), (1,)), ((), ())),
                                    preferred_element_type=F32) * din_ref[hd]
            state = s_ref[hd] * first_keep if j == 0 else s_ref[hd]
            o = (jnp.dot(inner.astype(BF16), v, preferred_element_type=F32)
                 + jnp.dot(q, state.astype(BF16), preferred_element_type=F32) * qdec_ref[hd])
            kd = (k * kdec_ref[hd]).astype(BF16)
            kv = lax.dot_general(kd, v, (((0,), (0,)), ((), ())), preferred_element_type=F32)
            s_ref[hd] = cdec_ref[hd] * state + kv
            mu = jnp.mean(o, axis=-1, keepdims=True)
            d = o - mu
            var = jnp.mean(d * d, axis=-1, keepdims=True)
            on = d * lax.rsqrt(var + EPS) * rg_ref[:, c0:c0 + 128]
            y = gate * (1.0 / (1.0 + jnp.exp(-gate))) * on
            mixed_ref[rows, ATTN_WIDTH + c0:ATTN_WIDTH + c0 + 128] = y.astype(BF16)

    while chunks:
        issue_chunk()


def _mix_kernel(tiles_per_seq,
                xn_ref, xr_ref, ada_n_ref, ada_r_ref, g1_ref, win_ref, gq_ref, gk_ref, sinks_ref,
                rg_ref, wout_ref, bias_ref, din_ref, qdec_ref, kdec_ref, cdec_ref,
                o_ref,
                proj_a, proj_b, mixed_a, mixed_b, hn_ref, kwin_a, kwin_b, vwin_a, vwin_b, s_ref):
    i = pl.program_id(0)

    @pl.when(i == 0)
    def _():
        proj_b[...] = jnp.zeros_like(proj_b)
        mixed_a[...] = jnp.zeros_like(mixed_a)
        kwin_a[...] = jnp.zeros_like(kwin_a)
        vwin_a[...] = jnp.zeros_like(vwin_a)
        s_ref[...] = jnp.zeros_like(s_ref)

    consts = (gq_ref, gk_ref, sinks_ref, rg_ref, bias_ref, din_ref, qdec_ref, kdec_ref, cdec_ref, s_ref)

    def slot(half, proj_cur, proj_next, mixed_cur, mixed_prev, kwin, kwin_prev, vwin, vwin_prev,
             first_tab, first_keep):
        def out_chunk(c):
            cols = slice(c * PROJ_CHUNK, (c + 1) * PROJ_CHUNK)
            y = jnp.dot(mixed_prev[...], wout_ref[:, cols], preferred_element_type=F32)
            o_ref[half, :, cols] = xr_ref[half, :, cols] + ada_r_ref[2:3, cols] * y

        def norm_next():
            x = xn_ref[half]
            ms = jnp.mean(x * x, axis=-1, keepdims=True)
            h = x * lax.rsqrt(ms + EPS) * g1_ref[...]
            hn_ref[...] = (h * (1.0 + ada_n_ref[1:2, :]) + ada_n_ref[0:1, :]).astype(BF16)

        def in_chunk(c):
            cols = slice(c * PROJ_CHUNK, (c + 1) * PROJ_CHUNK)
            proj_next[:, cols] = jnp.dot(hn_ref[...], win_ref[:, cols], preferred_element_type=F32)

        def first_chunk():
            out_chunk(0)
            norm_next()

        chunks = [first_chunk]
        chunks += [functools.partial(out_chunk, c) for c in range(1, D_MODEL // PROJ_CHUNK)]
        chunks += [functools.partial(in_chunk, c) for c in range(IN_WIDTH // PROJ_CHUNK)]
        _mix_slot(chunks, proj_cur, mixed_cur, kwin, kwin_prev, vwin, vwin_prev,
                  first_tab, first_keep, *consts)

    slot(0, proj_b, proj_a, mixed_b, mixed_a, kwin_b, kwin_a, vwin_b, vwin_a, 1, 1.0)
    seq_start = (2 * i) % tiles_per_seq == 0
    slot(1, proj_a, proj_b, mixed_a, mixed_b, kwin_a, kwin_b, vwin_a, vwin_b,
         jnp.where(seq_start, 0, 1), jnp.where(seq_start, 0.0, 1.0))


def _mix_call(layer, x, ada, g1, w_in, gq, gk, sinks, rg, w_out, tables):
    b, s, d = x.shape
    tiles_per_seq = s // MIX_ROWS
    pairs_per_seq = tiles_per_seq // 2
    n_pairs = b * pairs_per_seq
    bias_tab, din, qdec, kdec, cdec = tables
    x_pairs = x.reshape(n_pairs, 2, MIX_ROWS, d)

    nxt = lambda i: jnp.minimum(i, n_pairs - 1)
    res = lambda i: jnp.maximum(i - 1, 0)
    lay2 = lambda i: (layer, 0, 0)
    const3 = lambda i: (0, 0, 0)
    const4 = lambda i: (0, 0, 0, 0)
    single = pl.Buffered(1)
    smem = pl.BlockSpec(memory_space=pltpu.SMEM)
    out = pl.pallas_call(
        functools.partial(_mix_kernel, tiles_per_seq),
        grid=(n_pairs + 1,),
        in_specs=[
            pl.BlockSpec((None, 2, MIX_ROWS, d), lambda i: (nxt(i), 0, 0, 0)),
            pl.BlockSpec((None, 2, MIX_ROWS, d), lambda i: (res(i), 0, 0, 0)),
            pl.BlockSpec((None, None, 6, d), lambda i: (layer, nxt(i) // pairs_per_seq, 0, 0)),
            pl.BlockSpec((None, None, 6, d), lambda i: (layer, res(i) // pairs_per_seq, 0, 0)),
            pl.BlockSpec((None, 1, d), lay2),
            pl.BlockSpec((None, d, IN_WIDTH), lay2, pipeline_mode=single),
            pl.BlockSpec((None, 1, 128), lay2),
            pl.BlockSpec((None, 1, 128), lay2),
            smem,
            pl.BlockSpec((None, 1, RET_WIDTH), lay2),
            pl.BlockSpec((None, d, d), lay2, pipeline_mode=single),
            pl.BlockSpec(bias_tab.shape, const4, pipeline_mode=single),
            pl.BlockSpec(din.shape, const3, pipeline_mode=single),
            pl.BlockSpec(qdec.shape, const3, pipeline_mode=single),
            pl.BlockSpec(kdec.shape, const3, pipeline_mode=single),
            smem,
        ],
        out_specs=pl.BlockSpec((None, 2, MIX_ROWS, d), lambda i: (res(i), 0, 0, 0)),
        out_shape=jax.ShapeDtypeStruct((n_pairs, 2, MIX_ROWS, d), F32),
        scratch_shapes=[
            pltpu.VMEM((MIX_ROWS, IN_WIDTH), F32),
            pltpu.VMEM((MIX_ROWS, IN_WIDTH), F32),
            pltpu.VMEM((MIX_ROWS, d), BF16),
            pltpu.VMEM((MIX_ROWS, d), BF16),
            pltpu.VMEM((MIX_ROWS, d), BF16),
            pltpu.VMEM((N_KV_HEADS, 2 * HEAD_DIM, BLOCK + MIX_ROWS), BF16),
            pltpu.VMEM((N_KV_HEADS, 2 * HEAD_DIM, BLOCK + MIX_ROWS), BF16),
            pltpu.VMEM((2 * N_KV_HEADS, BLOCK + MIX_ROWS, 2 * HEAD_DIM), BF16),
            pltpu.VMEM((2 * N_KV_HEADS, BLOCK + MIX_ROWS, 2 * HEAD_DIM), BF16),
            pltpu.VMEM((RET_HEADS, 128, 128), F32),
        ],
        compiler_params=pltpu.CompilerParams(
            dimension_semantics=("arbitrary",), vmem_limit_bytes=VMEM_LIMIT),
        name="mix",
    )(x_pairs, x_pairs, ada, ada, g1, w_in, gq, gk, sinks, rg, w_out, bias_tab, din, qdec, kdec, cdec)
    return out.reshape(b, s, d)


MLP_ROWS = 1024


def _mlp_kernel(x_ref, ada_ref, g2_ref, w1_ref, w2_ref, o_ref):
    x = x_ref[...]
    ms = jnp.mean(x * x, axis=-1, keepdims=True)
    h = x * lax.rsqrt(ms + EPS) * g2_ref[...]
    h = (h * (1.0 + ada_ref[4:5, :]) + ada_ref[3:4, :]).astype(BF16)
    acc = None
    for c in range(D_FF // FF_CHUNK):
        cols = slice(c * FF_CHUNK, (c + 1) * FF_CHUNK)
        a = jnp.dot(h, w1_ref[:, cols], preferred_element_type=F32)
        a = jnp.maximum(a, 0.0)
        part = jnp.dot((a * a).astype(BF16), w2_ref[cols, :], preferred_element_type=F32)
        acc = part if acc is None else acc + part
    o_ref[...] = x + ada_ref[5:6, :] * acc


def _mlp_call(layer, x, ada, g2, w1, w2):
    b, s, d = x.shape
    lay2 = lambda i, j: (layer, 0, 0)
    single = pl.Buffered(1)
    return pl.pallas_call(
        _mlp_kernel,
        grid=(b, s // MLP_ROWS),
        in_specs=[
            pl.BlockSpec((None, MLP_ROWS, d), lambda i, j: (i, j, 0)),
            pl.BlockSpec((None, None, 6, d), lambda i, j: (layer, i, 0, 0)),
            pl.BlockSpec((None, 1, d), lay2),
            pl.BlockSpec((None, d, D_FF), lay2, pipeline_mode=single),
            pl.BlockSpec((None, D_FF, d), lay2, pipeline_mode=single),
        ],
        out_specs=pl.BlockSpec((None, MLP_ROWS, d), lambda i, j: (i, j, 0)),
        out_shape=jax.ShapeDtypeStruct((b, s, d), F32),
        compiler_params=pltpu.CompilerParams(
            dimension_semantics=("arbitrary", "arbitrary"), vmem_limit_bytes=VMEM_LIMIT),
        name="mlp",
    )(x, ada, g2, w1, w2)


def kernel(x, c, norm1_g, norm2_g, w_ada, b_ada, w_in, q_norm_g, k_norm_g, sinks, ret_norm_g,
           w_out, w_mlp1, w_mlp2):
    b = x.shape[0]
    tables = tuple(jnp.asarray(t) for t in _constant_tables())
    c_pad = jnp.pad(c, ((0, 8 - b), (0, 0)))
    ada = _ada_call(c_pad, w_ada, b_ada)[:, :b].reshape(DEPTH, b, 6, D_MODEL)
    g1 = norm1_g[:, None, :]
    g2 = norm2_g[:, None, :]
    gq = jnp.tile(q_norm_g, (1, 2))[:, None, :]
    gk = jnp.tile(k_norm_g, (1, 2))[:, None, :]
    rg = ret_norm_g[:, None, :]
    w_in_b, w_out_b = w_in.astype(BF16), w_out.astype(BF16)
    w1_b, w2_b = w_mlp1.astype(BF16), w_mlp2.astype(BF16)
    for l in range(DEPTH):
        x = _mix_call(l, x, ada, g1, w_in_b, gq, gk, sinks[l], rg, w_out_b, tables)
        x = _mlp_call(l, x, ada, g2, w1_b, w2_b)
    return x
```

```python
import functools

import numpy as np
import jax
import jax.numpy as jnp
from jax import lax
from jax.experimental import pallas as pl
from jax.experimental.pallas import tpu as pltpu

D_MODEL = 1024
DEPTH = 2
ATTN_WIDTH = 512
RET_WIDTH = 512
HEAD_DIM = 64
N_Q_HEADS = 8
N_KV_HEADS = 2
N_PAIRS = N_Q_HEADS // 2
BLOCK = 128
RET_HEADS = 4
RET_DK = 128
D_FF = 4 * D_MODEL
EPS = 1e-6
NEG_INF = -1e30
LOG2E = 1.4426950408889634
IN_WIDTH = 2816
OFF_AQ, OFF_AK, OFF_AV, OFF_RQ, OFF_RK, OFF_RV, OFF_RG = 0, 512, 640, 768, 1280, 1792, 2304

MIX_ROWS = 512
PROJ_CHUNK = 256
FF_CHUNK = 1024
ADA_COLS = 1536
VMEM_LIMIT = 60 * 1024 * 1024

F32 = jnp.float32
BF16 = jnp.bfloat16


@functools.lru_cache(maxsize=None)
def _constant_tables():
    q_pos = np.arange(BLOCK)[:, None]
    k_pos = np.arange(2 * BLOCK)[None, :]
    dist = q_pos + BLOCK - k_pos
    valid = (dist >= 0) & (dist < BLOCK)
    valid_first = valid & (k_pos >= BLOCK)
    slopes = np.exp2(-8.0 * np.arange(1, N_Q_HEADS + 1, dtype=np.float64) / N_Q_HEADS)
    bias = -slopes[:, None, None] * dist[None].astype(np.float64)
    bias_tab = np.stack([np.where(v[None], bias * LOG2E, NEG_INF) for v in (valid_first, valid)])
    bias_tab = bias_tab.astype(np.float32)

    log_gamma = np.log1p(-np.exp2(-5.0 - np.arange(RET_HEADS, dtype=np.float64)))
    idx = np.arange(BLOCK, dtype=np.float64)
    rel = idx[:, None] - idx[None, :]
    k_scale = RET_DK ** -0.5
    decay_in = np.where(rel >= 0, np.exp(log_gamma[:, None, None] * np.maximum(rel, 0.0)), 0.0) * k_scale
    q_decay = np.exp(log_gamma[:, None] * (idx[None, :] + 1.0))
    k_decay = np.exp(log_gamma[:, None] * (BLOCK - 1.0 - idx[None, :])) * k_scale
    q_decay = np.broadcast_to(q_decay[:, :, None], (RET_HEADS, BLOCK, BLOCK))
    k_decay = np.broadcast_to(k_decay[:, :, None], (RET_HEADS, BLOCK, BLOCK))
    chunk_decay = np.exp(log_gamma * BLOCK)
    return (bias_tab, decay_in.astype(np.float32), np.ascontiguousarray(q_decay, np.float32),
            np.ascontiguousarray(k_decay, np.float32), chunk_decay.astype(np.float32))


def _ada_kernel(c_ref, w_ref, b_ref, o_ref):
    c = c_ref[...]
    c_act = c * (1.0 / (1.0 + jnp.exp(-c)))
    o_ref[...] = jnp.dot(c_act.astype(BF16), w_ref[...].astype(BF16),
                         preferred_element_type=F32) + b_ref[...]


def _ada_call(c_pad, w_ada, b_ada):
    rows = c_pad.shape[0]
    n = w_ada.shape[-1]
    return pl.pallas_call(
        _ada_kernel,
        grid=(DEPTH, n // ADA_COLS),
        in_specs=[
            pl.BlockSpec((rows, D_MODEL), lambda l, j: (0, 0)),
            pl.BlockSpec((None, D_MODEL, ADA_COLS), lambda l, j: (l, 0, j)),
            pl.BlockSpec((None, 1, ADA_COLS), lambda l, j: (l, 0, j)),
        ],
        out_specs=pl.BlockSpec((None, rows, ADA_COLS), lambda l, j: (l, 0, j)),
        out_shape=jax.ShapeDtypeStruct((DEPTH, rows, n), F32),
        compiler_params=pltpu.CompilerParams(
            dimension_semantics=("arbitrary", "arbitrary"), vmem_limit_bytes=VMEM_LIMIT),
        name="ada",
    )(c_pad, w_ada, b_ada.reshape(DEPTH, 1, n))


def _mix_slot(chunks, proj_ref, mixed_ref, kwin_ref, kwin_prev, vwin_ref, vwin_prev,
              first_tab, first_keep, gq_ref, gk_ref, sinks_ref, rg_ref,
              bias_ref, din_ref, qdec_ref, kdec_ref, cdec_ref, s_ref):
    chunks = list(chunks)
    n_chunks = len(chunks)
    n_points = (MIX_ROWS // BLOCK) * (2 * N_KV_HEADS + RET_HEADS)
    point = [0]

    def issue_chunk():
        point[0] += 1
        while chunks and (n_chunks - len(chunks)) * n_points < point[0] * n_chunks:
            chunks.pop(0)()

    def head_norm(a, gain_row):
        lo = lax.broadcasted_iota(jnp.int32, a.shape, 1) < HEAD_DIM
        a2 = a * a
        s_lo = jnp.sum(jnp.where(lo, a2, 0.0), axis=-1, keepdims=True)
        s_hi = jnp.sum(jnp.where(lo, 0.0, a2), axis=-1, keepdims=True)
        mean_sq = jnp.where(lo, s_lo, s_hi) * (1.0 / HEAD_DIM)
        return a * lax.rsqrt(mean_sq + EPS) * gain_row

    issue_chunk()

    kn = head_norm(proj_ref[:, OFF_AK:OFF_AK + 128], gk_ref[...])
    kn_t = kn.T.astype(BF16)
    av = proj_ref[:, OFF_AV:OFF_AV + 128]
    av_rot = pltpu.roll(av, HEAD_DIM, axis=1)
    lo_t = lax.broadcasted_iota(jnp.int32, av.shape, 1) < HEAD_DIM
    v_parts = (jnp.where(lo_t, av, 0.0), jnp.where(lo_t, 0.0, av_rot),
               jnp.where(lo_t, av_rot, 0.0), jnp.where(lo_t, 0.0, av))
    for g in range(N_KV_HEADS):
        kwin_ref[g, :, 0:BLOCK] = kwin_prev[g, :, MIX_ROWS:MIX_ROWS + BLOCK]
        kt = kn_t[64 * g:64 * g + 64, :]
        kwin_ref[g, 0:64, BLOCK:BLOCK + MIX_ROWS] = kt
        kwin_ref[g, 64:128, BLOCK:BLOCK + MIX_ROWS] = kt
    for n in range(2 * N_KV_HEADS):
        vwin_ref[n, 0:BLOCK, 0:128] = vwin_prev[n, MIX_ROWS:MIX_ROWS + BLOCK, 0:128]
        vwin_ref[n, BLOCK:BLOCK + MIX_ROWS, 0:128] = v_parts[n].astype(BF16)

    lo = lax.broadcasted_iota(jnp.int32, (BLOCK, 2 * HEAD_DIM), 1) < HEAD_DIM
    gq = gq_ref[...] * (HEAD_DIM ** -0.5 * LOG2E)
    for j in range(MIX_ROWS // BLOCK):
        rows = pl.ds(j * BLOCK, BLOCK)
        win = slice(j * BLOCK, (j + 2) * BLOCK)
        tab = first_tab if j == 0 else 1

        for g in range(N_KV_HEADS):
            if j > 0 or g > 0:
                issue_chunk()
            qs = []
            for pp in range(2):
                p = 2 * g + pp
                qn = head_norm(proj_ref[rows, OFF_AQ + 128 * p:OFF_AQ + 128 * p + 128], gq)
                qs.append(jnp.where(lo, qn, 0.0).astype(BF16))
                qs.append(jnp.where(lo, 0.0, qn).astype(BF16))
            s_all = jnp.dot(jnp.concatenate(qs, axis=0), kwin_ref[g, :, win],
                            preferred_element_type=F32)
            es, ms, sinks = [], [], []
            for hh in range(4):
                head = 4 * g + hh
                sinks.append(sinks_ref[head] * LOG2E)
                sh = s_all[128 * hh:128 * hh + 128, :] + bias_ref[tab, head]
                ms.append(jnp.maximum(jnp.max(sh, axis=-1, keepdims=True), sinks[hh]))
                es.append(jnp.exp2(sh - ms[hh]).astype(BF16))
            issue_chunk()
            o_all = (jnp.dot(jnp.concatenate([es[0], es[2]], axis=0), vwin_ref[2 * g, win, :],
                             preferred_element_type=F32)
                     + jnp.dot(jnp.concatenate([es[1], es[3]], axis=0), vwin_ref[2 * g + 1, win, :],
                               preferred_element_type=F32))
            for pp in range(2):
                p = 2 * g + pp
                m_pair = jnp.where(lo, ms[2 * pp], ms[2 * pp + 1])
                sink_pair = jnp.where(lo, sinks[2 * pp], sinks[2 * pp + 1])
                denom = o_all[128 * pp:128 * pp + 128, 128:256] + jnp.exp2(sink_pair - m_pair)
                o = o_all[128 * pp:128 * pp + 128, 0:128] / denom
                mixed_ref[rows, 128 * p:128 * p + 128] = o.astype(BF16)

        for hd in range(RET_HEADS):
            issue_chunk()
            c0 = 128 * hd
            q_f = proj_ref[rows, OFF_RQ + c0:OFF_RQ + c0 + 128]
            q = q_f.astype(BF16)
            k = proj_ref[rows, OFF_RK + c0:OFF_RK + c0 + 128]
            v = proj_ref[rows, OFF_RV + c0:OFF_RV + c0 + 128].astype(BF16)
            gate = proj_ref[rows, OFF_RG + c0:OFF_RG + c0 + 128]
            inner = lax.dot_general(q, k.astype(BF16), (((1,), (1,)), ((), ())),
                                    preferred_element_type=F32) * din_ref[hd]
            state = s_ref[hd] * first_keep if j == 0 else s_ref[hd]
            o = jnp.dot(jnp.concatenate([inner.astype(BF16), (q_f * qdec_ref[hd]).astype(BF16)], axis=1),
                        jnp.concatenate([v, state.astype(BF16)], axis=0), preferred_element_type=F32)
            kd = (k * kdec_ref[hd]).astype(BF16)
            kv = lax.dot_general(kd, v, (((0,), (0,)), ((), ())), preferred_element_type=F32)
            s_ref[hd] = cdec_ref[hd] * state + kv
            mu = jnp.mean(o, axis=-1, keepdims=True)
            d = o - mu
            var = jnp.mean(d * d, axis=-1, keepdims=True)
            on = d * lax.rsqrt(var + EPS) * rg_ref[:, c0:c0 + 128]
            y = gate * (1.0 / (1.0 + jnp.exp(-gate))) * on
            mixed_ref[rows, ATTN_WIDTH + c0:ATTN_WIDTH + c0 + 128] = y.astype(BF16)

    assert not chunks


def _mix_kernel(tiles_per_seq,
                xn_ref, xr_ref, ada_n_ref, ada_r_ref, g1_ref, win_ref, gq_ref, gk_ref, sinks_ref,
                rg_ref, wout_ref, bias_ref, din_ref, qdec_ref, kdec_ref, cdec_ref,
                o_ref,
                proj_a, proj_b, mixed_a, mixed_b, hn_ref, kwin_a, kwin_b, vwin_a, vwin_b, s_ref):
    i = pl.program_id(0)

    @pl.when(i == 0)
    def _():
        proj_b[...] = jnp.zeros_like(proj_b)
        mixed_a[...] = jnp.zeros_like(mixed_a)
        kwin_a[...] = jnp.zeros_like(kwin_a)
        lane = lax.broadcasted_iota(jnp.int32, vwin_a.shape[1:], 1)
        for n in range(2 * N_KV_HEADS):
            sums_at = 2 * HEAD_DIM + HEAD_DIM * (n % 2)
            ones = jnp.where((lane >= sums_at) & (lane < sums_at + HEAD_DIM), 1.0, 0.0).astype(BF16)
            vwin_a[n] = ones
            vwin_b[n] = ones
        s_ref[...] = jnp.zeros_like(s_ref)

    consts = (gq_ref, gk_ref, sinks_ref, rg_ref, bias_ref, din_ref, qdec_ref, kdec_ref, cdec_ref, s_ref)

    def slot(half, proj_cur, proj_next, mixed_cur, mixed_prev, kwin, kwin_prev, vwin, vwin_prev,
             first_tab, first_keep):
        def out_chunk(c):
            cols = slice(c * PROJ_CHUNK, (c + 1) * PROJ_CHUNK)
            y = jnp.dot(mixed_prev[...], wout_ref[:, cols], preferred_element_type=F32)
            o_ref[half, :, cols] = xr_ref[half, :, cols] + ada_r_ref[2:3, cols] * y

        def norm_next():
            x = xn_ref[half]
            ms = jnp.mean(x * x, axis=-1, keepdims=True)
            h = x * lax.rsqrt(ms + EPS) * g1_ref[...]
            hn_ref[...] = (h * (1.0 + ada_n_ref[1:2, :]) + ada_n_ref[0:1, :]).astype(BF16)

        def in_chunk(c):
            cols = slice(c * PROJ_CHUNK, (c + 1) * PROJ_CHUNK)
            proj_next[:, cols] = jnp.dot(hn_ref[...], win_ref[:, cols], preferred_element_type=F32)

        def first_chunk():
            out_chunk(0)
            norm_next()

        chunks = [first_chunk]
        chunks += [functools.partial(out_chunk, c) for c in range(1, D_MODEL // PROJ_CHUNK)]
        chunks += [functools.partial(in_chunk, c) for c in range(IN_WIDTH // PROJ_CHUNK)]
        _mix_slot(chunks, proj_cur, mixed_cur, kwin, kwin_prev, vwin, vwin_prev,
                  first_tab, first_keep, *consts)

    slot(0, proj_b, proj_a, mixed_b, mixed_a, kwin_b, kwin_a, vwin_b, vwin_a, 1, 1.0)
    seq_start = (2 * i) % tiles_per_seq == 0
    slot(1, proj_a, proj_b, mixed_a, mixed_b, kwin_a, kwin_b, vwin_a, vwin_b,
         jnp.where(seq_start, 0, 1), jnp.where(seq_start, 0.0, 1.0))


def _mix_call(layer, x, ada, g1, w_in, gq, gk, sinks, rg, w_out, tables):
    b, s, d = x.shape
    tiles_per_seq = s // MIX_ROWS
    pairs_per_seq = tiles_per_seq // 2
    n_pairs = b * pairs_per_seq
    bias_tab, din, qdec, kdec, cdec = tables
    x_pairs = x.reshape(n_pairs, 2, MIX_ROWS, d)

    nxt = lambda i: jnp.minimum(i, n_pairs - 1)
    res = lambda i: jnp.maximum(i - 1, 0)
    lay2 = lambda i: (layer, 0, 0)
    const3 = lambda i: (0, 0, 0)
    const4 = lambda i: (0, 0, 0, 0)
    single = pl.Buffered(1)
    smem = pl.BlockSpec(memory_space=pltpu.SMEM)
    out = pl.pallas_call(
        functools.partial(_mix_kernel, tiles_per_seq),
        grid=(n_pairs + 1,),
        in_specs=[
            pl.BlockSpec((None, 2, MIX_ROWS, d), lambda i: (nxt(i), 0, 0, 0)),
            pl.BlockSpec((None, 2, MIX_ROWS, d), lambda i: (res(i), 0, 0, 0)),
            pl.BlockSpec((None, None, 6, d), lambda i: (layer, nxt(i) // pairs_per_seq, 0, 0)),
            pl.BlockSpec((None, None, 6, d), lambda i: (layer, res(i) // pairs_per_seq, 0, 0)),
            pl.BlockSpec((None, 1, d), lay2),
            pl.BlockSpec((None, d, IN_WIDTH), lay2, pipeline_mode=single),
            pl.BlockSpec((None, 1, 128), lay2),
            pl.BlockSpec((None, 1, 128), lay2),
            smem,
            pl.BlockSpec((None, 1, RET_WIDTH), lay2),
            pl.BlockSpec((None, d, d), lay2, pipeline_mode=single),
            pl.BlockSpec(bias_tab.shape, const4, pipeline_mode=single),
            pl.BlockSpec(din.shape, const3, pipeline_mode=single),
            pl.BlockSpec(qdec.shape, const3, pipeline_mode=single),
            pl.BlockSpec(kdec.shape, const3, pipeline_mode=single),
            smem,
        ],
        out_specs=pl.BlockSpec((None, 2, MIX_ROWS, d), lambda i: (res(i), 0, 0, 0)),
        out_shape=jax.ShapeDtypeStruct((n_pairs, 2, MIX_ROWS, d), F32),
        scratch_shapes=[
            pltpu.VMEM((MIX_ROWS, IN_WIDTH), F32),
            pltpu.VMEM((MIX_ROWS, IN_WIDTH), F32),
            pltpu.VMEM((MIX_ROWS, d), BF16),
            pltpu.VMEM((MIX_ROWS, d), BF16),
            pltpu.VMEM((MIX_ROWS, d), BF16),
            pltpu.VMEM((N_KV_HEADS, 2 * HEAD_DIM, BLOCK + MIX_ROWS), BF16),
            pltpu.VMEM((N_KV_HEADS, 2 * HEAD_DIM, BLOCK + MIX_ROWS), BF16),
            pltpu.VMEM((2 * N_KV_HEADS, BLOCK + MIX_ROWS, 4 * HEAD_DIM), BF16),
            pltpu.VMEM((2 * N_KV_HEADS, BLOCK + MIX_ROWS, 4 * HEAD_DIM), BF16),
            pltpu.VMEM((RET_HEADS, 128, 128), F32),
        ],
        compiler_params=pltpu.CompilerParams(
            dimension_semantics=("arbitrary",), vmem_limit_bytes=VMEM_LIMIT),
        name="mix",
    )(x_pairs, x_pairs, ada, ada, g1, w_in, gq, gk, sinks, rg, w_out, bias_tab, din, qdec, kdec, cdec)
    return out.reshape(b, s, d)


MLP_ROWS = 1024


def _mlp_kernel(x_ref, ada_ref, g2_ref, w1_ref, w2_ref, o_ref):
    x = x_ref[...]
    ms = jnp.mean(x * x, axis=-1, keepdims=True)
    h = x * lax.rsqrt(ms + EPS) * g2_ref[...]
    h = (h * (1.0 + ada_ref[4:5, :]) + ada_ref[3:4, :]).astype(BF16)
    acc = None
    for c in range(D_FF // FF_CHUNK):
        cols = slice(c * FF_CHUNK, (c + 1) * FF_CHUNK)
        a = jnp.dot(h, w1_ref[:, cols], preferred_element_type=F32)
        a = jnp.maximum(a, 0.0)
        part = jnp.dot((a * a).astype(BF16), w2_ref[cols, :], preferred_element_type=F32)
        acc = part if acc is None else acc + part
    o_ref[...] = x + ada_ref[5:6, :] * acc


def _mlp_call(layer, x, ada, g2, w1, w2):
    b, s, d = x.shape
    lay2 = lambda i, j: (layer, 0, 0)
    single = pl.Buffered(1)
    return pl.pallas_call(
        _mlp_kernel,
        grid=(b, s // MLP_ROWS),
        in_specs=[
            pl.BlockSpec((None, MLP_ROWS, d), lambda i, j: (i, j, 0)),
            pl.BlockSpec((None, None, 6, d), lambda i, j: (layer, i, 0, 0)),
            pl.BlockSpec((None, 1, d), lay2),
            pl.BlockSpec((None, d, D_FF), lay2, pipeline_mode=single),
            pl.BlockSpec((None, D_FF, d), lay2, pipeline_mode=single),
        ],
        out_specs=pl.BlockSpec((None, MLP_ROWS, d), lambda i, j: (i, j, 0)),
        out_shape=jax.ShapeDtypeStruct((b, s, d), F32),
        compiler_params=pltpu.CompilerParams(
            dimension_semantics=("arbitrary", "arbitrary"), vmem_limit_bytes=VMEM_LIMIT),
        name="mlp",
    )(x, ada, g2, w1, w2)


def kernel(x, c, norm1_g, norm2_g, w_ada, b_ada, w_in, q_norm_g, k_norm_g, sinks, ret_norm_g,
           w_out, w_mlp1, w_mlp2):
    b = x.shape[0]
    tables = tuple(jnp.asarray(t) for t in _constant_tables())
    c_pad = jnp.pad(c, ((0, 8 - b), (0, 0)))
    ada = _ada_call(c_pad, w_ada, b_ada)[:, :b].reshape(DEPTH, b, 6, D_MODEL)
    g1 = norm1_g[:, None, :]
    g2 = norm2_g[:, None, :]
    gq = jnp.tile(q_norm_g, (1, 2))[:, None, :]
    gk = jnp.tile(k_norm_g, (1, 2))[:, None, :]
    rg = ret_norm_g[:, None, :]
    w_in_b, w_out_b = w_in.astype(BF16), w_out.astype(BF16)
    w1_b, w2_b = w_mlp1.astype(BF16), w_mlp2.astype(BF16)
    for l in range(DEPTH):
        x = _mix_call(l, x, ada, g1, w_in_b, gq, gk, sinks[l], rg, w_out_b, tables)
        x = _mlp_call(l, x, ada, g2, w1_b, w2_b)
    return x
```

```python
import functools

import numpy as np
import jax
import jax.numpy as jnp
from jax import lax
from jax.experimental import pallas as pl
from jax.experimental.pallas import tpu as pltpu

D_MODEL = 1024
DEPTH = 2
ATTN_WIDTH = 512
RET_WIDTH = 512
HEAD_DIM = 64
N_Q_HEADS = 8
N_KV_HEADS = 2
N_PAIRS = N_Q_HEADS // 2
BLOCK = 128
RET_HEADS = 4
RET_DK = 128
D_FF = 4 * D_MODEL
EPS = 1e-6
NEG_INF = -1e30
LOG2E = 1.4426950408889634
IN_WIDTH = 2816
OFF_AQ, OFF_AK, OFF_AV, OFF_RQ, OFF_RK, OFF_RV, OFF_RG = 0, 512, 640, 768, 1280, 1792, 2304

MIX_ROWS = 256
PROJ_CHUNK = 256
FF_CHUNK = 1024
ADA_COLS = 1536
VMEM_LIMIT = 60 * 1024 * 1024

F32 = jnp.float32
BF16 = jnp.bfloat16


@functools.lru_cache(maxsize=None)
def _constant_tables():
    q_pos = np.arange(BLOCK)[:, None]
    k_pos = np.arange(2 * BLOCK)[None, :]
    dist = q_pos + BLOCK - k_pos
    valid = (dist >= 0) & (dist < BLOCK)
    valid_first = valid & (k_pos >= BLOCK)
    slopes = np.exp2(-8.0 * np.arange(1, N_Q_HEADS + 1, dtype=np.float64) / N_Q_HEADS)
    bias = -slopes[:, None, None] * dist[None].astype(np.float64)
    bias_tab = np.stack([np.where(v[None], bias * LOG2E, NEG_INF) for v in (valid_first, valid)])
    bias_tab = bias_tab.astype(np.float32)

    log_gamma = np.log1p(-np.exp2(-5.0 - np.arange(RET_HEADS, dtype=np.float64)))
    idx = np.arange(BLOCK, dtype=np.float64)
    rel = idx[:, None] - idx[None, :]
    k_scale = RET_DK ** -0.5
    decay_in = np.where(rel >= 0, np.exp(log_gamma[:, None, None] * np.maximum(rel, 0.0)), 0.0) * k_scale
    q_decay = np.exp(log_gamma[:, None] * (idx[None, :] + 1.0))
    k_decay = np.exp(log_gamma[:, None] * (BLOCK - 1.0 - idx[None, :])) * k_scale
    q_decay = np.broadcast_to(q_decay[:, :, None], (RET_HEADS, BLOCK, BLOCK))
    k_decay = np.broadcast_to(k_decay[:, :, None], (RET_HEADS, BLOCK, BLOCK))
    chunk_decay = np.exp(log_gamma * BLOCK)
    return (bias_tab, decay_in.astype(np.float32), np.ascontiguousarray(q_decay, np.float32),
            np.ascontiguousarray(k_decay, np.float32), chunk_decay.astype(np.float32))


def _ada_kernel(c_ref, w_ref, b_ref, o_ref):
    c = c_ref[...]
    c_act = c * (1.0 / (1.0 + jnp.exp(-c)))
    o_ref[...] = jnp.dot(c_act.astype(BF16), w_ref[...].astype(BF16),
                         preferred_element_type=F32) + b_ref[...]


def _ada_call(c_pad, w_ada, b_ada):
    rows = c_pad.shape[0]
    n = w_ada.shape[-1]
    return pl.pallas_call(
        _ada_kernel,
        grid=(DEPTH, n // ADA_COLS),
        in_specs=[
            pl.BlockSpec((rows, D_MODEL), lambda l, j: (0, 0)),
            pl.BlockSpec((None, D_MODEL, ADA_COLS), lambda l, j: (l, 0, j)),
            pl.BlockSpec((None, 1, ADA_COLS), lambda l, j: (l, 0, j)),
        ],
        out_specs=pl.BlockSpec((None, rows, ADA_COLS), lambda l, j: (l, 0, j)),
        out_shape=jax.ShapeDtypeStruct((DEPTH, rows, n), F32),
        compiler_params=pltpu.CompilerParams(
            dimension_semantics=("arbitrary", "arbitrary"), vmem_limit_bytes=VMEM_LIMIT),
        name="ada",
    )(c_pad, w_ada, b_ada.reshape(DEPTH, 1, n))


def _mix_slot(chunks, proj_ref, mixed_ref, kwin_ref, kwin_prev, vwin_ref, vwin_prev,
              first_tab, first_keep, gq_ref, gk_ref, sinks_ref, rg_ref,
              bias_ref, din_ref, qdec_ref, kdec_ref, cdec_ref, s_ref):
    chunks = list(chunks)
    n_chunks = len(chunks)
    n_points = (MIX_ROWS // BLOCK) * (2 * N_KV_HEADS + RET_HEADS)
    point = [0]

    def issue_chunk():
        point[0] += 1
        while chunks and (n_chunks - len(chunks)) * n_points < point[0] * n_chunks:
            chunks.pop(0)()

    def head_norm(a, gain_row):
        lo = lax.broadcasted_iota(jnp.int32, a.shape, 1) < HEAD_DIM
        a2 = a * a
        s_lo = jnp.sum(jnp.where(lo, a2, 0.0), axis=-1, keepdims=True)
        s_hi = jnp.sum(jnp.where(lo, 0.0, a2), axis=-1, keepdims=True)
        mean_sq = jnp.where(lo, s_lo, s_hi) * (1.0 / HEAD_DIM)
        return a * lax.rsqrt(mean_sq + EPS) * gain_row

    issue_chunk()

    kn = head_norm(proj_ref[:, OFF_AK:OFF_AK + 128], gk_ref[...])
    kn_t = kn.T.astype(BF16)
    av = proj_ref[:, OFF_AV:OFF_AV + 128]
    av_rot = pltpu.roll(av, HEAD_DIM, axis=1)
    lo_t = lax.broadcasted_iota(jnp.int32, av.shape, 1) < HEAD_DIM
    v_parts = (jnp.where(lo_t, av, 0.0), jnp.where(lo_t, 0.0, av_rot),
               jnp.where(lo_t, av_rot, 0.0), jnp.where(lo_t, 0.0, av))
    for g in range(N_KV_HEADS):
        kwin_ref[g, :, 0:BLOCK] = kwin_prev[g, :, MIX_ROWS:MIX_ROWS + BLOCK]
        kt = kn_t[64 * g:64 * g + 64, :]
        kwin_ref[g, 0:64, BLOCK:BLOCK + MIX_ROWS] = kt
        kwin_ref[g, 64:128, BLOCK:BLOCK + MIX_ROWS] = kt
    for n in range(2 * N_KV_HEADS):
        vwin_ref[n, 0:BLOCK, 0:128] = vwin_prev[n, MIX_ROWS:MIX_ROWS + BLOCK, 0:128]
        vwin_ref[n, BLOCK:BLOCK + MIX_ROWS, 0:128] = v_parts[n].astype(BF16)

    lo = lax.broadcasted_iota(jnp.int32, (BLOCK, 2 * HEAD_DIM), 1) < HEAD_DIM
    gq = gq_ref[...] * (HEAD_DIM ** -0.5 * LOG2E)
    for j in range(MIX_ROWS // BLOCK):
        rows = pl.ds(j * BLOCK, BLOCK)
        win = slice(j * BLOCK, (j + 2) * BLOCK)
        tab = first_tab if j == 0 else 1

        for g in range(N_KV_HEADS):
            if j > 0 or g > 0:
                issue_chunk()
            qs = []
            for pp in range(2):
                p = 2 * g + pp
                qn = head_norm(proj_ref[rows, OFF_AQ + 128 * p:OFF_AQ + 128 * p + 128], gq)
                qs.append(jnp.where(lo, qn, 0.0).astype(BF16))
                qs.append(jnp.where(lo, 0.0, qn).astype(BF16))
            s_all = jnp.dot(jnp.concatenate(qs, axis=0), kwin_ref[g, :, win],
                            preferred_element_type=F32)
            es, ms, sinks = [], [], []
            for hh in range(4):
                head = 4 * g + hh
                sinks.append(sinks_ref[head] * LOG2E)
                sh = s_all[128 * hh:128 * hh + 128, :] + bias_ref[tab, head]
                ms.append(jnp.maximum(jnp.max(sh, axis=-1, keepdims=True), sinks[hh]))
                es.append(jnp.exp2(sh - ms[hh]).astype(BF16))
            issue_chunk()
            o_all = (jnp.dot(jnp.concatenate([es[0], es[2]], axis=0), vwin_ref[2 * g, win, :],
                             preferred_element_type=F32)
                     + jnp.dot(jnp.concatenate([es[1], es[3]], axis=0), vwin_ref[2 * g + 1, win, :],
                               preferred_element_type=F32))
            for pp in range(2):
                p = 2 * g + pp
                m_pair = jnp.where(lo, ms[2 * pp], ms[2 * pp + 1])
                sink_pair = jnp.where(lo, sinks[2 * pp], sinks[2 * pp + 1])
                denom = o_all[128 * pp:128 * pp + 128, 128:256] + jnp.exp2(sink_pair - m_pair)
                o = o_all[128 * pp:128 * pp + 128, 0:128] / denom
                mixed_ref[rows, 128 * p:128 * p + 128] = o.astype(BF16)

        for hd in range(RET_HEADS):
            issue_chunk()
            c0 = 128 * hd
            q_f = proj_ref[rows, OFF_RQ + c0:OFF_RQ + c0 + 128]
            q = q_f.astype(BF16)
            k = proj_ref[rows, OFF_RK + c0:OFF_RK + c0 + 128]
            v = proj_ref[rows, OFF_RV + c0:OFF_RV + c0 + 128].astype(BF16)
            gate = proj_ref[rows, OFF_RG + c0:OFF_RG + c0 + 128]
            inner = lax.dot_general(q, k.astype(BF16), (((1,), (1,)), ((), ())),
                                    preferred_element_type=F32) * din_ref[hd]
            state = s_ref[hd] * first_keep if j == 0 else s_ref[hd]
            o = jnp.dot(jnp.concatenate([inner.astype(BF16), (q_f * qdec_ref[hd]).astype(BF16)], axis=1),
                        jnp.concatenate([v, state.astype(BF16)], axis=0), preferred_element_type=F32)
            kd = (k * kdec_ref[hd]).astype(BF16)
            kv = lax.dot_general(kd, v, (((0,), (0,)), ((), ())), preferred_element_type=F32)
            s_ref[hd] = cdec_ref[hd] * state + kv
            mu = jnp.mean(o, axis=-1, keepdims=True)
            d = o - mu
            var = jnp.mean(d * d, axis=-1, keepdims=True)
            on = d * lax.rsqrt(var + EPS) * rg_ref[:, c0:c0 + 128]
            y = gate * (1.0 / (1.0 + jnp.exp(-gate))) * on
            mixed_ref[rows, ATTN_WIDTH + c0:ATTN_WIDTH + c0 + 128] = y.astype(BF16)

    assert not chunks


def _mix_kernel(tiles_per_seq,
                xn_ref, xr_ref, ada_n_ref, ada_r_ref, g1_ref, win_ref, gq_ref, gk_ref, sinks_ref,
                rg_ref, wout_ref, bias_ref, din_ref, qdec_ref, kdec_ref, cdec_ref,
                o_ref,
                proj_a, proj_b, mixed_a, mixed_b, hn_ref, kwin_a, kwin_b, vwin_a, vwin_b, s_ref):
    i = pl.program_id(0)

    @pl.when(i == 0)
    def _():
        proj_b[...] = jnp.zeros_like(proj_b)
        mixed_a[...] = jnp.zeros_like(mixed_a)
        kwin_a[...] = jnp.zeros_like(kwin_a)
        lane = lax.broadcasted_iota(jnp.int32, vwin_a.shape[1:], 1)
        for n in range(2 * N_KV_HEADS):
            sums_at = 2 * HEAD_DIM + HEAD_DIM * (n % 2)
            ones = jnp.where((lane >= sums_at) & (lane < sums_at + HEAD_DIM), 1.0, 0.0).astype(BF16)
            vwin_a[n] = ones
            vwin_b[n] = ones
        s_ref[...] = jnp.zeros_like(s_ref)

    consts = (gq_ref, gk_ref, sinks_ref, rg_ref, bias_ref, din_ref, qdec_ref, kdec_ref, cdec_ref, s_ref)

    def slot(half, proj_cur, proj_next, mixed_cur, mixed_prev, kwin, kwin_prev, vwin, vwin_prev,
             first_tab, first_keep):
        def out_chunk(c):
            cols = slice(c * PROJ_CHUNK, min((c + 1) * PROJ_CHUNK, D_MODEL))
            y = jnp.dot(mixed_prev[...], wout_ref[:, cols], preferred_element_type=F32)
            o_ref[half, :, cols] = xr_ref[half, :, cols] + ada_r_ref[2:3, cols] * y

        def norm_next():
            x = xn_ref[half]
            ms = jnp.mean(x * x, axis=-1, keepdims=True)
            h = x * lax.rsqrt(ms + EPS) * g1_ref[...]
            hn_ref[...] = (h * (1.0 + ada_n_ref[1:2, :]) + ada_n_ref[0:1, :]).astype(BF16)

        def in_chunk(c):
            cols = slice(c * PROJ_CHUNK, min((c + 1) * PROJ_CHUNK, IN_WIDTH))
            proj_next[:, cols] = jnp.dot(hn_ref[...], win_ref[:, cols], preferred_element_type=F32)

        def first_chunk():
            out_chunk(0)
            norm_next()

        chunks = [first_chunk]
        chunks += [functools.partial(out_chunk, c) for c in range(1, pl.cdiv(D_MODEL, PROJ_CHUNK))]
        chunks += [functools.partial(in_chunk, c) for c in range(pl.cdiv(IN_WIDTH, PROJ_CHUNK))]
        _mix_slot(chunks, proj_cur, mixed_cur, kwin, kwin_prev, vwin, vwin_prev,
                  first_tab, first_keep, *consts)

    slot(0, proj_b, proj_a, mixed_b, mixed_a, kwin_b, kwin_a, vwin_b, vwin_a, 1, 1.0)
    seq_start = (2 * i) % tiles_per_seq == 0
    slot(1, proj_a, proj_b, mixed_a, mixed_b, kwin_a, kwin_b, vwin_a, vwin_b,
         jnp.where(seq_start, 0, 1), jnp.where(seq_start, 0.0, 1.0))


def _mix_call(layer, x, ada, g1, w_in, gq, gk, sinks, rg, w_out, tables):
    b, s, d = x.shape
    tiles_per_seq = s // MIX_ROWS
    pairs_per_seq = tiles_per_seq // 2
    n_pairs = b * pairs_per_seq
    bias_tab, din, qdec, kdec, cdec = tables
    x_pairs = x.reshape(n_pairs, 2, MIX_ROWS, d)

    nxt = lambda i: jnp.minimum(i, n_pairs - 1)
    res = lambda i: jnp.maximum(i - 1, 0)
    lay2 = lambda i: (layer, 0, 0)
    const3 = lambda i: (0, 0, 0)
    const4 = lambda i: (0, 0, 0, 0)
    single = pl.Buffered(1)
    smem = pl.BlockSpec(memory_space=pltpu.SMEM)
    out = pl.pallas_call(
        functools.partial(_mix_kernel, tiles_per_seq),
        grid=(n_pairs + 1,),
        in_specs=[
            pl.BlockSpec((None, 2, MIX_ROWS, d), lambda i: (nxt(i), 0, 0, 0)),
            pl.BlockSpec((None, 2, MIX_ROWS, d), lambda i: (res(i), 0, 0, 0)),
            pl.BlockSpec((None, None, 6, d), lambda i: (layer, nxt(i) // pairs_per_seq, 0, 0)),
            pl.BlockSpec((None, None, 6, d), lambda i: (layer, res(i) // pairs_per_seq, 0, 0)),
            pl.BlockSpec((None, 1, d), lay2),
            pl.BlockSpec((None, d, IN_WIDTH), lay2, pipeline_mode=single),
            pl.BlockSpec((None, 1, 128), lay2),
            pl.BlockSpec((None, 1, 128), lay2),
            smem,
            pl.BlockSpec((None, 1, RET_WIDTH), lay2),
            pl.BlockSpec((None, d, d), lay2, pipeline_mode=single),
            pl.BlockSpec(bias_tab.shape, const4, pipeline_mode=single),
            pl.BlockSpec(din.shape, const3, pipeline_mode=single),
            pl.BlockSpec(qdec.shape, const3, pipeline_mode=single),
            pl.BlockSpec(kdec.shape, const3, pipeline_mode=single),
            smem,
        ],
        out_specs=pl.BlockSpec((None, 2, MIX_ROWS, d), lambda i: (res(i), 0, 0, 0)),
        out_shape=jax.ShapeDtypeStruct((n_pairs, 2, MIX_ROWS, d), F32),
        scratch_shapes=[
            pltpu.VMEM((MIX_ROWS, IN_WIDTH), F32),
            pltpu.VMEM((MIX_ROWS, IN_WIDTH), F32),
            pltpu.VMEM((MIX_ROWS, d), BF16),
            pltpu.VMEM((MIX_ROWS, d), BF16),
            pltpu.VMEM((MIX_ROWS, d), BF16),
            pltpu.VMEM((N_KV_HEADS, 2 * HEAD_DIM, BLOCK + MIX_ROWS), BF16),
            pltpu.VMEM((N_KV_HEADS, 2 * HEAD_DIM, BLOCK + MIX_ROWS), BF16),
            pltpu.VMEM((2 * N_KV_HEADS, BLOCK + MIX_ROWS, 4 * HEAD_DIM), BF16),
            pltpu.VMEM((2 * N_KV_HEADS, BLOCK + MIX_ROWS, 4 * HEAD_DIM), BF16),
            pltpu.VMEM((RET_HEADS, 128, 128), F32),
        ],
        compiler_params=pltpu.CompilerParams(
            dimension_semantics=("arbitrary",), vmem_limit_bytes=VMEM_LIMIT),
        name="mix",
    )(x_pairs, x_pairs, ada, ada, g1, w_in, gq, gk, sinks, rg, w_out, bias_tab, din, qdec, kdec, cdec)
    return out.reshape(b, s, d)


MLP_ROWS = 1024


def _mlp_kernel(x_ref, ada_ref, g2_ref, w1_ref, w2_ref, o_ref):
    x = x_ref[...]
    ms = jnp.mean(x * x, axis=-1, keepdims=True)
    h = x * lax.rsqrt(ms + EPS) * g2_ref[...]
    h = (h * (1.0 + ada_ref[4:5, :]) + ada_ref[3:4, :]).astype(BF16)
    acc = None
    for c in range(D_FF // FF_CHUNK):
        cols = slice(c * FF_CHUNK, (c + 1) * FF_CHUNK)
        a = jnp.dot(h, w1_ref[:, cols], preferred_element_type=F32)
        a = jnp.maximum(a, 0.0)
        part = jnp.dot((a * a).astype(BF16), w2_ref[cols, :], preferred_element_type=F32)
        acc = part if acc is None else acc + part
    o_ref[...] = x + ada_ref[5:6, :] * acc


def _mlp_call(layer, x, ada, g2, w1, w2):
    b, s, d = x.shape
    lay2 = lambda i, j: (layer, 0, 0)
    single = pl.Buffered(1)
    return pl.pallas_call(
        _mlp_kernel,
        grid=(b, s // MLP_ROWS),
        in_specs=[
            pl.BlockSpec((None, MLP_ROWS, d), lambda i, j: (i, j, 0)),
            pl.BlockSpec((None, None, 6, d), lambda i, j: (layer, i, 0, 0)),
            pl.BlockSpec((None, 1, d), lay2),
            pl.BlockSpec((None, d, D_FF), lay2, pipeline_mode=single),
            pl.BlockSpec((None, D_FF, d), lay2, pipeline_mode=single),
        ],
        out_specs=pl.BlockSpec((None, MLP_ROWS, d), lambda i, j: (i, j, 0)),
        out_shape=jax.ShapeDtypeStruct((b, s, d), F32),
        compiler_params=pltpu.CompilerParams(
            dimension_semantics=("arbitrary", "arbitrary"), vmem_limit_bytes=VMEM_LIMIT),
        name="mlp",
    )(x, ada, g2, w1, w2)


def kernel(x, c, norm1_g, norm2_g, w_ada, b_ada, w_in, q_norm_g, k_norm_g, sinks, ret_norm_g,
           w_out, w_mlp1, w_mlp2):
    b = x.shape[0]
    tables = tuple(jnp.asarray(t) for t in _constant_tables())
    c_pad = jnp.pad(c, ((0, 8 - b), (0, 0)))
    ada = _ada_call(c_pad, w_ada, b_ada)[:, :b].reshape(DEPTH, b, 6, D_MODEL)
    g1 = norm1_g[:, None, :]
    g2 = norm2_g[:, None, :]
    gq = jnp.tile(q_norm_g, (1, 2))[:, None, :]
    gk = jnp.tile(k_norm_g, (1, 2))[:, None, :]
    rg = ret_norm_g[:, None, :]
    w_in_b, w_out_b = w_in.astype(BF16), w_out.astype(BF16)
    w1_b, w2_b = w_mlp1.astype(BF16), w_mlp2.astype(BF16)
    for l in range(DEPTH):
        x = _mix_call(l, x, ada, g1, w_in_b, gq, gk, sinks[l], rg, w_out_b, tables)
        x = _mlp_call(l, x, ada, g2, w1_b, w2_b)
    return x
```

```python
import functools

import numpy as np
import jax
import jax.numpy as jnp
from jax import lax
from jax.experimental import pallas as pl
from jax.experimental.pallas import tpu as pltpu

D_MODEL = 1024
DEPTH = 2
ATTN_WIDTH = 512
RET_WIDTH = 512
HEAD_DIM = 64
N_Q_HEADS = 8
N_KV_HEADS = 2
N_PAIRS = N_Q_HEADS // 2
BLOCK = 128
RET_HEADS = 4
RET_DK = 128
D_FF = 4 * D_MODEL
EPS = 1e-6
NEG_INF = -1e30
LOG2E = 1.4426950408889634
IN_WIDTH = 2816
OFF_AQ, OFF_AK, OFF_AV, OFF_RQ, OFF_RK, OFF_RV, OFF_RG = 0, 512, 640, 768, 1280, 1792, 2304

MIX_ROWS = 256
PROJ_CHUNK = 256
FF_CHUNK = 1024
ADA_COLS = 1536
VMEM_LIMIT = 60 * 1024 * 1024

F32 = jnp.float32
BF16 = jnp.bfloat16


@functools.lru_cache(maxsize=None)
def _constant_tables():
    q_pos = np.arange(BLOCK)[:, None]
    k_pos = np.arange(2 * BLOCK)[None, :]
    dist = q_pos + BLOCK - k_pos
    valid = (dist >= 0) & (dist < BLOCK)
    valid_first = valid & (k_pos >= BLOCK)
    slopes = np.exp2(-8.0 * np.arange(1, N_Q_HEADS + 1, dtype=np.float64) / N_Q_HEADS)
    bias = -slopes[:, None, None] * dist[None].astype(np.float64)
    bias_tab = np.stack([np.where(v[None], bias * LOG2E, NEG_INF) for v in (valid_first, valid)])
    bias_tab = bias_tab.astype(np.float32)

    log_gamma = np.log1p(-np.exp2(-5.0 - np.arange(RET_HEADS, dtype=np.float64)))
    idx = np.arange(BLOCK, dtype=np.float64)
    rel = idx[:, None] - idx[None, :]
    k_scale = RET_DK ** -0.5
    decay_in = np.where(rel >= 0, np.exp(log_gamma[:, None, None] * np.maximum(rel, 0.0)), 0.0) * k_scale
    q_decay = np.exp(log_gamma[:, None] * (idx[None, :] + 1.0))
    k_decay = np.exp(log_gamma[:, None] * (BLOCK - 1.0 - idx[None, :])) * k_scale
    q_decay = np.broadcast_to(q_decay[:, :, None], (RET_HEADS, BLOCK, BLOCK))
    k_decay = np.broadcast_to(k_decay[:, :, None], (RET_HEADS, BLOCK, BLOCK))
    chunk_decay = np.exp(log_gamma * BLOCK)
    return (bias_tab, decay_in.astype(np.float32), np.ascontiguousarray(q_decay, np.float32),
            np.ascontiguousarray(k_decay, np.float32), chunk_decay.astype(np.float32))


def _ada_kernel(c_ref, w_ref, b_ref, o_ref):
    c = c_ref[...]
    c_act = c * (1.0 / (1.0 + jnp.exp(-c)))
    o_ref[...] = jnp.dot(c_act.astype(BF16), w_ref[...].astype(BF16),
                         preferred_element_type=F32) + b_ref[...]


def _ada_call(c_pad, w_ada, b_ada):
    rows = c_pad.shape[0]
    n = w_ada.shape[-1]
    return pl.pallas_call(
        _ada_kernel,
        grid=(DEPTH, n // ADA_COLS),
        in_specs=[
            pl.BlockSpec((rows, D_MODEL), lambda l, j: (0, 0)),
            pl.BlockSpec((None, D_MODEL, ADA_COLS), lambda l, j: (l, 0, j)),
            pl.BlockSpec((None, 1, ADA_COLS), lambda l, j: (l, 0, j)),
        ],
        out_specs=pl.BlockSpec((None, rows, ADA_COLS), lambda l, j: (l, 0, j)),
        out_shape=jax.ShapeDtypeStruct((DEPTH, rows, n), F32),
        compiler_params=pltpu.CompilerParams(
            dimension_semantics=("arbitrary", "arbitrary"), vmem_limit_bytes=VMEM_LIMIT),
        name="ada",
    )(c_pad, w_ada, b_ada.reshape(DEPTH, 1, n))


def _mix_slot(chunks, proj_ref, mixed_ref, kwin_ref, kwin_prev, vwin_ref, vwin_prev,
              first_tab, first_keep, gq_ref, gk_ref, sinks_ref, rg_ref,
              bias_ref, din_ref, qdec_ref, kdec_ref, cdec_ref, s_ref):
    chunks = list(chunks)
    n_chunks = len(chunks)
    n_points = (MIX_ROWS // BLOCK) * (2 * N_KV_HEADS + RET_HEADS)
    point = [0]

    def issue_chunk():
        point[0] += 1
        while chunks and (n_chunks - len(chunks)) * n_points < point[0] * n_chunks:
            chunks.pop(0)()

    def head_norm(a, gain_row):
        lo = lax.broadcasted_iota(jnp.int32, a.shape, 1) < HEAD_DIM
        a2 = a * a
        s_lo = jnp.sum(jnp.where(lo, a2, 0.0), axis=-1, keepdims=True)
        s_hi = jnp.sum(jnp.where(lo, 0.0, a2), axis=-1, keepdims=True)
        mean_sq = jnp.where(lo, s_lo, s_hi) * (1.0 / HEAD_DIM)
        return a * lax.rsqrt(mean_sq + EPS) * gain_row

    issue_chunk()

    kn = head_norm(proj_ref[:, OFF_AK:OFF_AK + 128], gk_ref[...])
    kn_t = kn.T.astype(BF16)
    av = proj_ref[:, OFF_AV:OFF_AV + 128]
    av_rot = pltpu.roll(av, HEAD_DIM, axis=1)
    lo_t = lax.broadcasted_iota(jnp.int32, av.shape, 1) < HEAD_DIM
    v_parts = (jnp.where(lo_t, av, 0.0), jnp.where(lo_t, 0.0, av_rot),
               jnp.where(lo_t, av_rot, 0.0), jnp.where(lo_t, 0.0, av))
    for g in range(N_KV_HEADS):
        kwin_ref[g, :, 0:BLOCK] = kwin_prev[g, :, MIX_ROWS:MIX_ROWS + BLOCK]
        kt = kn_t[64 * g:64 * g + 64, :]
        kwin_ref[g, 0:64, BLOCK:BLOCK + MIX_ROWS] = kt
        kwin_ref[g, 64:128, BLOCK:BLOCK + MIX_ROWS] = kt
    for n in range(2 * N_KV_HEADS):
        vwin_ref[n, 0:BLOCK, 0:128] = vwin_prev[n, MIX_ROWS:MIX_ROWS + BLOCK, 0:128]
        vwin_ref[n, BLOCK:BLOCK + MIX_ROWS, 0:128] = v_parts[n].astype(BF16)

    lo = lax.broadcasted_iota(jnp.int32, (BLOCK, 2 * HEAD_DIM), 1) < HEAD_DIM
    gq = gq_ref[...] * (HEAD_DIM ** -0.5 * LOG2E)
    for j in range(MIX_ROWS // BLOCK):
        rows = pl.ds(j * BLOCK, BLOCK)
        win = slice(j * BLOCK, (j + 2) * BLOCK)
        tab = first_tab if j == 0 else 1

        for g in range(N_KV_HEADS):
            if j > 0 or g > 0:
                issue_chunk()
            qs = []
            for pp in range(2):
                p = 2 * g + pp
                qn = head_norm(proj_ref[rows, OFF_AQ + 128 * p:OFF_AQ + 128 * p + 128], gq)
                qs.append(jnp.where(lo, qn, 0.0).astype(BF16))
                qs.append(jnp.where(lo, 0.0, qn).astype(BF16))
            s_all = jnp.dot(jnp.concatenate(qs, axis=0), kwin_ref[g, :, win],
                            preferred_element_type=F32)
            es, ms, sinks = [], [], []
            for hh in range(4):
                head = 4 * g + hh
                sinks.append(sinks_ref[head] * LOG2E)
                sh = s_all[128 * hh:128 * hh + 128, :] + bias_ref[tab, head]
                ms.append(jnp.maximum(jnp.max(sh, axis=-1, keepdims=True), sinks[hh]))
                es.append(jnp.exp2(sh - ms[hh]).astype(BF16))
            issue_chunk()
            o_all = (jnp.dot(jnp.concatenate([es[0], es[2]], axis=0), vwin_ref[2 * g, win, :],
                             preferred_element_type=F32)
                     + jnp.dot(jnp.concatenate([es[1], es[3]], axis=0), vwin_ref[2 * g + 1, win, :],
                               preferred_element_type=F32))
            for pp in range(2):
                p = 2 * g + pp
                m_pair = jnp.where(lo, ms[2 * pp], ms[2 * pp + 1])
                sink_pair = jnp.where(lo, sinks[2 * pp], sinks[2 * pp + 1])
                denom = o_all[128 * pp:128 * pp + 128, 128:256] + jnp.exp2(sink_pair - m_pair)
                o = o_all[128 * pp:128 * pp + 128, 0:128] / denom
                mixed_ref[rows, 128 * p:128 * p + 128] = o.astype(BF16)

        for hd in range(RET_HEADS):
            issue_chunk()
            c0 = 128 * hd
            q_f = proj_ref[rows, OFF_RQ + c0:OFF_RQ + c0 + 128]
            q = q_f.astype(BF16)
            k = proj_ref[rows, OFF_RK + c0:OFF_RK + c0 + 128]
            v = proj_ref[rows, OFF_RV + c0:OFF_RV + c0 + 128].astype(BF16)
            gate = proj_ref[rows, OFF_RG + c0:OFF_RG + c0 + 128]
            inner = lax.dot_general(q, k.astype(BF16), (((1,), (1,)), ((), ())),
                                    preferred_element_type=F32) * din_ref[hd]
            state = s_ref[hd] * first_keep if j == 0 else s_ref[hd]
            o = jnp.dot(jnp.concatenate([inner.astype(BF16), (q_f * qdec_ref[hd]).astype(BF16)], axis=1),
                        jnp.concatenate([v, state.astype(BF16)], axis=0), preferred_element_type=F32)
            kd = (k * kdec_ref[hd]).astype(BF16)
            kv = lax.dot_general(kd, v, (((0,), (0,)), ((), ())), preferred_element_type=F32)
            s_ref[hd] = cdec_ref[hd] * state + kv
            mu = jnp.mean(o, axis=-1, keepdims=True)
            d = o - mu
            var = jnp.mean(d * d, axis=-1, keepdims=True)
            on = d * lax.rsqrt(var + EPS) * rg_ref[:, c0:c0 + 128]
            y = gate * (1.0 / (1.0 + jnp.exp(-gate))) * on
            mixed_ref[rows, ATTN_WIDTH + c0:ATTN_WIDTH + c0 + 128] = y.astype(BF16)

    assert not chunks


def _mix_kernel(tiles_per_seq,
                xn_ref, xr_ref, ada_n_ref, ada_r_ref, g1_ref, win_ref, gq_ref, gk_ref, sinks_ref,
                rg_ref, wout_ref, bias_ref, din_ref, qdec_ref, kdec_ref, cdec_ref,
                o_ref,
                proj_a, proj_b, mixed_a, mixed_b, hn_ref, kwin_a, kwin_b, vwin_a, vwin_b, s_ref):
    i = pl.program_id(0)

    @pl.when(i == 0)
    def _():
        proj_b[...] = jnp.zeros_like(proj_b)
        mixed_a[...] = jnp.zeros_like(mixed_a)
        kwin_a[...] = jnp.zeros_like(kwin_a)
        lane = lax.broadcasted_iota(jnp.int32, vwin_a.shape[1:], 1)
        for n in range(2 * N_KV_HEADS):
            sums_at = 2 * HEAD_DIM + HEAD_DIM * (n % 2)
            ones = jnp.where((lane >= sums_at) & (lane < sums_at + HEAD_DIM), 1.0, 0.0).astype(BF16)
            vwin_a[n] = ones
            vwin_b[n] = ones
        s_ref[...] = jnp.zeros_like(s_ref)

    consts = (gq_ref, gk_ref, sinks_ref, rg_ref, bias_ref, din_ref, qdec_ref, kdec_ref, cdec_ref, s_ref)

    def slot(half, proj_cur, proj_next, mixed_cur, mixed_prev, kwin, kwin_prev, vwin, vwin_prev,
             first_tab, first_keep):
        def out_chunk(c):
            cols = slice(c * PROJ_CHUNK, min((c + 1) * PROJ_CHUNK, D_MODEL))
            y = jnp.dot(mixed_prev[...], wout_ref[:, cols], preferred_element_type=F32)
            o_ref[half, :, cols] = xr_ref[half, :, cols] + ada_r_ref[2:3, cols] * y

        def norm_next():
            x = xn_ref[half]
            ms = jnp.mean(x * x, axis=-1, keepdims=True)
            h = x * lax.rsqrt(ms + EPS) * g1_ref[...]
            hn_ref[...] = (h * (1.0 + ada_n_ref[1:2, :]) + ada_n_ref[0:1, :]).astype(BF16)

        def in_chunk(c):
            cols = slice(c * PROJ_CHUNK, min((c + 1) * PROJ_CHUNK, IN_WIDTH))
            proj_next[:, cols] = jnp.dot(hn_ref[...], win_ref[:, cols], preferred_element_type=F32)

        def first_chunk():
            out_chunk(0)
            norm_next()

        chunks = [first_chunk]
        chunks += [functools.partial(out_chunk, c) for c in range(1, pl.cdiv(D_MODEL, PROJ_CHUNK))]
        chunks += [functools.partial(in_chunk, c) for c in range(pl.cdiv(IN_WIDTH, PROJ_CHUNK))]
        _mix_slot(chunks, proj_cur, mixed_cur, kwin, kwin_prev, vwin, vwin_prev,
                  first_tab, first_keep, *consts)

    slot(0, proj_b, proj_a, mixed_b, mixed_a, kwin_b, kwin_a, vwin_b, vwin_a, 1, 1.0)
    seq_start = (2 * i) % tiles_per_seq == 0
    slot(1, proj_a, proj_b, mixed_a, mixed_b, kwin_a, kwin_b, vwin_a, vwin_b,
         jnp.where(seq_start, 0, 1), jnp.where(seq_start, 0.0, 1.0))


def _mix_call(layer, x, ada, g1, w_in, gq, gk, sinks, rg, w_out, tables):
    b, s, d = x.shape
    tiles_per_seq = s // MIX_ROWS
    pairs_per_seq = tiles_per_seq // 2
    n_pairs = b * pairs_per_seq
    bias_tab, din, qdec, kdec, cdec = tables
    x_pairs = x.reshape(n_pairs, 2, MIX_ROWS, d)

    nxt = lambda i: jnp.minimum(i, n_pairs - 1)
    res = lambda i: jnp.maximum(i - 1, 0)
    lay2 = lambda i: (layer, 0, 0)
    const3 = lambda i: (0, 0, 0)
    const4 = lambda i: (0, 0, 0, 0)
    single = pl.Buffered(1)
    smem = pl.BlockSpec(memory_space=pltpu.SMEM)
    out = pl.pallas_call(
        functools.partial(_mix_kernel, tiles_per_seq),
        grid=(n_pairs + 1,),
        in_specs=[
            pl.BlockSpec((None, 2, MIX_ROWS, d), lambda i: (nxt(i), 0, 0, 0)),
            pl.BlockSpec((None, 2, MIX_ROWS, d), lambda i: (res(i), 0, 0, 0)),
            pl.BlockSpec((None, None, 6, d), lambda i: (layer, nxt(i) // pairs_per_seq, 0, 0)),
            pl.BlockSpec((None, None, 6, d), lambda i: (layer, res(i) // pairs_per_seq, 0, 0)),
            pl.BlockSpec((None, 1, d), lay2),
            pl.BlockSpec((None, d, IN_WIDTH), lay2, pipeline_mode=single),
            pl.BlockSpec((None, 1, 128), lay2),
            pl.BlockSpec((None, 1, 128), lay2),
            smem,
            pl.BlockSpec((None, 1, RET_WIDTH), lay2),
            pl.BlockSpec((None, d, d), lay2, pipeline_mode=single),
            pl.BlockSpec(bias_tab.shape, const4, pipeline_mode=single),
            pl.BlockSpec(din.shape, const3, pipeline_mode=single),
            pl.BlockSpec(qdec.shape, const3, pipeline_mode=single),
            pl.BlockSpec(kdec.shape, const3, pipeline_mode=single),
            smem,
        ],
        out_specs=pl.BlockSpec((None, 2, MIX_ROWS, d), lambda i: (res(i), 0, 0, 0)),
        out_shape=jax.ShapeDtypeStruct((n_pairs, 2, MIX_ROWS, d), F32),
        scratch_shapes=[
            pltpu.VMEM((MIX_ROWS, IN_WIDTH), F32),
            pltpu.VMEM((MIX_ROWS, IN_WIDTH), F32),
            pltpu.VMEM((MIX_ROWS, d), BF16),
            pltpu.VMEM((MIX_ROWS, d), BF16),
            pltpu.VMEM((MIX_ROWS, d), BF16),
            pltpu.VMEM((N_KV_HEADS, 2 * HEAD_DIM, BLOCK + MIX_ROWS), BF16),
            pltpu.VMEM((N_KV_HEADS, 2 * HEAD_DIM, BLOCK + MIX_ROWS), BF16),
            pltpu.VMEM((2 * N_KV_HEADS, BLOCK + MIX_ROWS, 4 * HEAD_DIM), BF16),
            pltpu.VMEM((2 * N_KV_HEADS, BLOCK + MIX_ROWS, 4 * HEAD_DIM), BF16),
            pltpu.VMEM((RET_HEADS, 128, 128), F32),
        ],
        compiler_params=pltpu.CompilerParams(
            dimension_semantics=("arbitrary",), vmem_limit_bytes=VMEM_LIMIT),
        name="mix",
    )(x_pairs, x_pairs, ada, ada, g1, w_in, gq, gk, sinks, rg, w_out, bias_tab, din, qdec, kdec, cdec)
    return out.reshape(b, s, d)


MLP_ROWS = 1024
W_STAGE_CHUNKS = 16


def _load_weights_as_bf16(layer, w_hbm, stage_ref, sem_ref, wb_ref):
    slab = wb_ref.shape[0] // W_STAGE_CHUNKS

    def slab_copy(c):
        return pltpu.make_async_copy(w_hbm.at[layer, pl.ds(c * slab, slab), :],
                                     stage_ref.at[c % 2], sem_ref.at[c % 2])

    slab_copy(0).start()
    for c in range(W_STAGE_CHUNKS):
        if c + 1 < W_STAGE_CHUNKS:
            slab_copy(c + 1).start()
        slab_copy(c).wait()
        wb_ref[pl.ds(c * slab, slab), :] = stage_ref[c % 2].astype(BF16)


def _mlp_kernel(layer, x_ref, ada_ref, g2_ref, w1_hbm, w2_hbm, o_ref,
                w1_ref, w2_ref, stage1_ref, stage2_ref, sem1_ref, sem2_ref):
    @pl.when((pl.program_id(0) == 0) & (pl.program_id(1) == 0))
    def _():
        _load_weights_as_bf16(layer, w1_hbm, stage1_ref, sem1_ref, w1_ref)
        _load_weights_as_bf16(layer, w2_hbm, stage2_ref, sem2_ref, w2_ref)

    x = x_ref[...]
    ms = jnp.mean(x * x, axis=-1, keepdims=True)
    h = x * lax.rsqrt(ms + EPS) * g2_ref[...]
    h = (h * (1.0 + ada_ref[4:5, :]) + ada_ref[3:4, :]).astype(BF16)
    acc = None
    for c in range(D_FF // FF_CHUNK):
        cols = slice(c * FF_CHUNK, (c + 1) * FF_CHUNK)
        a = jnp.dot(h, w1_ref[:, cols], preferred_element_type=F32)
        a = jnp.maximum(a, 0.0)
        part = jnp.dot((a * a).astype(BF16), w2_ref[cols, :], preferred_element_type=F32)
        acc = part if acc is None else acc + part
    o_ref[...] = x + ada_ref[5:6, :] * acc


def _mlp_call(layer, x, ada, g2, w1, w2):
    b, s, d = x.shape
    lay2 = lambda i, j: (layer, 0, 0)
    hbm = pl.BlockSpec(memory_space=pl.ANY)
    return pl.pallas_call(
        functools.partial(_mlp_kernel, layer),
        grid=(b, s // MLP_ROWS),
        in_specs=[
            pl.BlockSpec((None, MLP_ROWS, d), lambda i, j: (i, j, 0)),
            pl.BlockSpec((None, None, 6, d), lambda i, j: (layer, i, 0, 0)),
            pl.BlockSpec((None, 1, d), lay2),
            hbm,
            hbm,
        ],
        out_specs=pl.BlockSpec((None, MLP_ROWS, d), lambda i, j: (i, j, 0)),
        out_shape=jax.ShapeDtypeStruct((b, s, d), F32),
        scratch_shapes=[
            pltpu.VMEM((d, D_FF), BF16),
            pltpu.VMEM((D_FF, d), BF16),
            pltpu.VMEM((2, d // W_STAGE_CHUNKS, D_FF), F32),
            pltpu.VMEM((2, D_FF // W_STAGE_CHUNKS, d), F32),
            pltpu.SemaphoreType.DMA((2,)),
            pltpu.SemaphoreType.DMA((2,)),
        ],
        compiler_params=pltpu.CompilerParams(
            dimension_semantics=("arbitrary", "arbitrary"), vmem_limit_bytes=VMEM_LIMIT),
        name="mlp",
    )(x, ada, g2, w1, w2)


def kernel(x, c, norm1_g, norm2_g, w_ada, b_ada, w_in, q_norm_g, k_norm_g, sinks, ret_norm_g,
           w_out, w_mlp1, w_mlp2):
    b = x.shape[0]
    tables = tuple(jnp.asarray(t) for t in _constant_tables())
    c_pad = jnp.pad(c, ((0, 8 - b), (0, 0)))
    ada = _ada_call(c_pad, w_ada, b_ada)[:, :b].reshape(DEPTH, b, 6, D_MODEL)
    g1 = norm1_g[:, None, :]
    g2 = norm2_g[:, None, :]
    gq = jnp.tile(q_norm_g, (1, 2))[:, None, :]
    gk = jnp.tile(k_norm_g, (1, 2))[:, None, :]
    rg = ret_norm_g[:, None, :]
    w_in_b, w_out_b = w_in.astype(BF16), w_out.astype(BF16)
    for l in range(DEPTH):
        x = _mix_call(l, x, ada, g1, w_in_b, gq, gk, sinks[l], rg, w_out_b, tables)
        x = _mlp_call(l, x, ada, g2, w_mlp1, w_mlp2)
    return x
```

```python
import functools

import numpy as np
import jax
import jax.numpy as jnp
from jax import lax
from jax.experimental import pallas as pl
from jax.experimental.pallas import tpu as pltpu

D_MODEL = 1024
DEPTH = 2
ATTN_WIDTH = 512
RET_WIDTH = 512
HEAD_DIM = 64
N_Q_HEADS = 8
N_KV_HEADS = 2
N_PAIRS = N_Q_HEADS // 2
BLOCK = 128
RET_HEADS = 4
RET_DK = 128
D_FF = 4 * D_MODEL
EPS = 1e-6
NEG_INF = -1e30
LOG2E = 1.4426950408889634
IN_WIDTH = 2816
OFF_AQ, OFF_AK, OFF_AV, OFF_RQ, OFF_RK, OFF_RV, OFF_RG = 0, 512, 640, 768, 1280, 1792, 2304

MIX_ROWS = 256
PROJ_CHUNK = 256
FF_CHUNK = 1024
ADA_COLS = 1536
VMEM_LIMIT = 60 * 1024 * 1024

F32 = jnp.float32
BF16 = jnp.bfloat16


@functools.lru_cache(maxsize=None)
def _constant_tables():
    q_pos = np.arange(BLOCK)[:, None]
    k_pos = np.arange(2 * BLOCK)[None, :]
    dist = q_pos + BLOCK - k_pos
    valid = (dist >= 0) & (dist < BLOCK)
    valid_first = valid & (k_pos >= BLOCK)
    slopes = np.exp2(-8.0 * np.arange(1, N_Q_HEADS + 1, dtype=np.float64) / N_Q_HEADS)
    bias = -slopes[:, None, None] * dist[None].astype(np.float64)
    bias_tab = np.stack([np.where(v[None], bias * LOG2E, NEG_INF) for v in (valid_first, valid)])
    bias_tab = bias_tab.astype(np.float32)

    log_gamma = np.log1p(-np.exp2(-5.0 - np.arange(RET_HEADS, dtype=np.float64)))
    idx = np.arange(BLOCK, dtype=np.float64)
    rel = idx[:, None] - idx[None, :]
    k_scale = RET_DK ** -0.5
    decay_in = np.where(rel >= 0, np.exp(log_gamma[:, None, None] * np.maximum(rel, 0.0)), 0.0) * k_scale
    q_decay = np.exp(log_gamma[:, None] * (idx[None, :] + 1.0))
    k_decay = np.exp(log_gamma[:, None] * (BLOCK - 1.0 - idx[None, :])) * k_scale
    q_decay = np.broadcast_to(q_decay[:, :, None], (RET_HEADS, BLOCK, BLOCK))
    k_decay = np.broadcast_to(k_decay[:, :, None], (RET_HEADS, BLOCK, BLOCK))
    chunk_decay = np.exp(log_gamma * BLOCK)
    return (bias_tab, decay_in.astype(np.float32), np.ascontiguousarray(q_decay, np.float32),
            np.ascontiguousarray(k_decay, np.float32), chunk_decay.astype(np.float32))


def _ada_kernel(c_ref, w_ref, b_ref, o_ref):
    c = c_ref[...]
    c_act = c * (1.0 / (1.0 + jnp.exp(-c)))
    o_ref[...] = jnp.dot(c_act.astype(BF16), w_ref[...].astype(BF16),
                         preferred_element_type=F32) + b_ref[...]


def _ada_call(c_pad, w_ada, b_ada):
    rows = c_pad.shape[0]
    n = w_ada.shape[-1]
    return pl.pallas_call(
        _ada_kernel,
        grid=(DEPTH, n // ADA_COLS),
        in_specs=[
            pl.BlockSpec((rows, D_MODEL), lambda l, j: (0, 0)),
            pl.BlockSpec((None, D_MODEL, ADA_COLS), lambda l, j: (l, 0, j)),
            pl.BlockSpec((None, 1, ADA_COLS), lambda l, j: (l, 0, j)),
        ],
        out_specs=pl.BlockSpec((None, rows, ADA_COLS), lambda l, j: (l, 0, j)),
        out_shape=jax.ShapeDtypeStruct((DEPTH, rows, n), F32),
        compiler_params=pltpu.CompilerParams(
            dimension_semantics=("arbitrary", "arbitrary"), vmem_limit_bytes=VMEM_LIMIT),
        name="ada",
    )(c_pad, w_ada, b_ada.reshape(DEPTH, 1, n))


def _mix_slot(chunks, proj_ref, mixed_ref, kwin_ref, kwin_prev, vwin_ref, vwin_prev,
              first_tab, first_keep, gq_ref, gk_ref, sinks_ref, rg_ref,
              bias_ref, din_ref, qdec_ref, kdec_ref, cdec_ref, s_ref):
    chunks = list(chunks)
    n_chunks = len(chunks)
    n_points = (MIX_ROWS // BLOCK) * (2 * N_KV_HEADS + RET_HEADS)
    point = [0]

    def issue_chunk():
        point[0] += 1
        while chunks and (n_chunks - len(chunks)) * n_points < point[0] * n_chunks:
            chunks.pop(0)()

    def head_norm(a, gain_row):
        lo = lax.broadcasted_iota(jnp.int32, a.shape, 1) < HEAD_DIM
        a2 = a * a
        s_lo = jnp.sum(jnp.where(lo, a2, 0.0), axis=-1, keepdims=True)
        s_hi = jnp.sum(jnp.where(lo, 0.0, a2), axis=-1, keepdims=True)
        mean_sq = jnp.where(lo, s_lo, s_hi) * (1.0 / HEAD_DIM)
        return a * lax.rsqrt(mean_sq + EPS) * gain_row

    issue_chunk()

    kn = head_norm(proj_ref[:, OFF_AK:OFF_AK + 128], gk_ref[...])
    kn_t = kn.T.astype(BF16)
    av = proj_ref[:, OFF_AV:OFF_AV + 128]
    av_rot = pltpu.roll(av, HEAD_DIM, axis=1)
    lo_t = lax.broadcasted_iota(jnp.int32, av.shape, 1) < HEAD_DIM
    v_parts = (jnp.where(lo_t, av, 0.0), jnp.where(lo_t, 0.0, av_rot),
               jnp.where(lo_t, av_rot, 0.0), jnp.where(lo_t, 0.0, av))
    for g in range(N_KV_HEADS):
        kwin_ref[g, :, 0:BLOCK] = kwin_prev[g, :, MIX_ROWS:MIX_ROWS + BLOCK]
        kt = kn_t[64 * g:64 * g + 64, :]
        kwin_ref[g, 0:64, BLOCK:BLOCK + MIX_ROWS] = kt
        kwin_ref[g, 64:128, BLOCK:BLOCK + MIX_ROWS] = kt
    for n in range(2 * N_KV_HEADS):
        vwin_ref[n, 0:BLOCK, 0:128] = vwin_prev[n, MIX_ROWS:MIX_ROWS + BLOCK, 0:128]
        vwin_ref[n, BLOCK:BLOCK + MIX_ROWS, 0:128] = v_parts[n].astype(BF16)

    lo = lax.broadcasted_iota(jnp.int32, (BLOCK, 2 * HEAD_DIM), 1) < HEAD_DIM
    gq = gq_ref[...] * (HEAD_DIM ** -0.5 * LOG2E)
    for j in range(MIX_ROWS // BLOCK):
        rows = pl.ds(j * BLOCK, BLOCK)
        win = slice(j * BLOCK, (j + 2) * BLOCK)
        tab = first_tab if j == 0 else 1

        for g in range(N_KV_HEADS):
            if j > 0 or g > 0:
                issue_chunk()
            qs = []
            for pp in range(2):
                p = 2 * g + pp
                qn = head_norm(proj_ref[rows, OFF_AQ + 128 * p:OFF_AQ + 128 * p + 128], gq)
                qs.append(jnp.where(lo, qn, 0.0).astype(BF16))
                qs.append(jnp.where(lo, 0.0, qn).astype(BF16))
            s_all = jnp.dot(jnp.concatenate(qs, axis=0), kwin_ref[g, :, win],
                            preferred_element_type=F32)
            es, ms, sinks = [], [], []
            for hh in range(4):
                head = 4 * g + hh
                sinks.append(sinks_ref[head] * LOG2E)
                sh = s_all[128 * hh:128 * hh + 128, :] + bias_ref[tab, head]
                ms.append(jnp.maximum(jnp.max(sh, axis=-1, keepdims=True), sinks[hh]))
                es.append(jnp.exp2(sh - ms[hh]).astype(BF16))
            issue_chunk()
            o_all = (jnp.dot(jnp.concatenate([es[0], es[2]], axis=0), vwin_ref[2 * g, win, :],
                             preferred_element_type=F32)
                     + jnp.dot(jnp.concatenate([es[1], es[3]], axis=0), vwin_ref[2 * g + 1, win, :],
                               preferred_element_type=F32))
            for pp in range(2):
                p = 2 * g + pp
                m_pair = jnp.where(lo, ms[2 * pp], ms[2 * pp + 1])
                sink_pair = jnp.where(lo, sinks[2 * pp], sinks[2 * pp + 1])
                denom = o_all[128 * pp:128 * pp + 128, 128:256] + jnp.exp2(sink_pair - m_pair)
                o = o_all[128 * pp:128 * pp + 128, 0:128] / denom
                mixed_ref[rows, 128 * p:128 * p + 128] = o.astype(BF16)

        for hd in range(RET_HEADS):
            issue_chunk()
            c0 = 128 * hd
            q_f = proj_ref[rows, OFF_RQ + c0:OFF_RQ + c0 + 128]
            q = q_f.astype(BF16)
            k = proj_ref[rows, OFF_RK + c0:OFF_RK + c0 + 128]
            v = proj_ref[rows, OFF_RV + c0:OFF_RV + c0 + 128].astype(BF16)
            gate = proj_ref[rows, OFF_RG + c0:OFF_RG + c0 + 128]
            inner = lax.dot_general(q, k.astype(BF16), (((1,), (1,)), ((), ())),
                                    preferred_element_type=F32) * din_ref[hd]
            state = s_ref[hd] * first_keep if j == 0 else s_ref[hd]
            o = jnp.dot(jnp.concatenate([inner.astype(BF16), (q_f * qdec_ref[hd]).astype(BF16)], axis=1),
                        jnp.concatenate([v, state.astype(BF16)], axis=0), preferred_element_type=F32)
            kd = (k * kdec_ref[hd]).astype(BF16)
            kv = lax.dot_general(kd, v, (((0,), (0,)), ((), ())), preferred_element_type=F32)
            s_ref[hd] = cdec_ref[hd] * state + kv
            mu = jnp.mean(o, axis=-1, keepdims=True)
            d = o - mu
            var = jnp.mean(d * d, axis=-1, keepdims=True)
            on = d * lax.rsqrt(var + EPS) * rg_ref[:, c0:c0 + 128]
            y = gate * (1.0 / (1.0 + jnp.exp(-gate))) * on
            mixed_ref[rows, ATTN_WIDTH + c0:ATTN_WIDTH + c0 + 128] = y.astype(BF16)

    assert not chunks


def _mix_kernel(tiles_per_seq,
                xn_ref, ada_n_ref, ada_r_ref, g1_ref, win_ref, gq_ref, gk_ref, sinks_ref,
                rg_ref, wout_ref, bias_ref, din_ref, qdec_ref, kdec_ref, cdec_ref,
                o_ref,
                proj_a, proj_b, mixed_a, mixed_b, hn_ref, xres_ref, kwin_a, kwin_b, vwin_a, vwin_b, s_ref):
    i = pl.program_id(0)

    @pl.when(i == 0)
    def _():
        proj_b[...] = jnp.zeros_like(proj_b)
        mixed_a[...] = jnp.zeros_like(mixed_a)
        xres_ref[...] = jnp.zeros_like(xres_ref)
        kwin_a[...] = jnp.zeros_like(kwin_a)
        lane = lax.broadcasted_iota(jnp.int32, vwin_a.shape[1:], 1)
        for n in range(2 * N_KV_HEADS):
            sums_at = 2 * HEAD_DIM + HEAD_DIM * (n % 2)
            ones = jnp.where((lane >= sums_at) & (lane < sums_at + HEAD_DIM), 1.0, 0.0).astype(BF16)
            vwin_a[n] = ones
            vwin_b[n] = ones
        s_ref[...] = jnp.zeros_like(s_ref)

    consts = (gq_ref, gk_ref, sinks_ref, rg_ref, bias_ref, din_ref, qdec_ref, kdec_ref, cdec_ref, s_ref)

    def slot(half, proj_cur, proj_next, mixed_cur, mixed_prev, kwin, kwin_prev, vwin, vwin_prev,
             first_tab, first_keep):
        def out_chunk(c):
            cols = slice(c * PROJ_CHUNK, min((c + 1) * PROJ_CHUNK, D_MODEL))
            y = jnp.dot(mixed_prev[...], wout_ref[:, cols], preferred_element_type=F32)
            o_ref[half, :, cols] = xres_ref[half, :, cols] + ada_r_ref[2:3, cols] * y
            xres_ref[half, :, cols] = xn_ref[half, :, cols]

        def norm_next():
            x = xn_ref[half]
            ms = jnp.mean(x * x, axis=-1, keepdims=True)
            h = x * lax.rsqrt(ms + EPS) * g1_ref[...]
            hn_ref[...] = (h * (1.0 + ada_n_ref[1:2, :]) + ada_n_ref[0:1, :]).astype(BF16)

        def in_chunk(c):
            cols = slice(c * PROJ_CHUNK, min((c + 1) * PROJ_CHUNK, IN_WIDTH))
            proj_next[:, cols] = jnp.dot(hn_ref[...], win_ref[:, cols], preferred_element_type=F32)

        def first_chunk():
            out_chunk(0)
            norm_next()

        chunks = [first_chunk]
        chunks += [functools.partial(out_chunk, c) for c in range(1, pl.cdiv(D_MODEL, PROJ_CHUNK))]
        chunks += [functools.partial(in_chunk, c) for c in range(pl.cdiv(IN_WIDTH, PROJ_CHUNK))]
        _mix_slot(chunks, proj_cur, mixed_cur, kwin, kwin_prev, vwin, vwin_prev,
                  first_tab, first_keep, *consts)

    slot(0, proj_b, proj_a, mixed_b, mixed_a, kwin_b, kwin_a, vwin_b, vwin_a, 1, 1.0)
    seq_start = (2 * i) % tiles_per_seq == 0
    slot(1, proj_a, proj_b, mixed_a, mixed_b, kwin_a, kwin_b, vwin_a, vwin_b,
         jnp.where(seq_start, 0, 1), jnp.where(seq_start, 0.0, 1.0))


def _mix_call(layer, x, ada, g1, w_in, gq, gk, sinks, rg, w_out, tables):
    b, s, d = x.shape
    tiles_per_seq = s // MIX_ROWS
    pairs_per_seq = tiles_per_seq // 2
    n_pairs = b * pairs_per_seq
    bias_tab, din, qdec, kdec, cdec = tables
    x_pairs = x.reshape(n_pairs, 2, MIX_ROWS, d)

    nxt = lambda i: jnp.minimum(i, n_pairs - 1)
    res = lambda i: jnp.maximum(i - 1, 0)
    lay2 = lambda i: (layer, 0, 0)
    const3 = lambda i: (0, 0, 0)
    const4 = lambda i: (0, 0, 0, 0)
    single = pl.Buffered(1)
    smem = pl.BlockSpec(memory_space=pltpu.SMEM)
    out = pl.pallas_call(
        functools.partial(_mix_kernel, tiles_per_seq),
        grid=(n_pairs + 1,),
        in_specs=[
            pl.BlockSpec((None, 2, MIX_ROWS, d), lambda i: (nxt(i), 0, 0, 0)),
            pl.BlockSpec((None, None, 6, d), lambda i: (layer, nxt(i) // pairs_per_seq, 0, 0)),
            pl.BlockSpec((None, None, 6, d), lambda i: (layer, res(i) // pairs_per_seq, 0, 0)),
            pl.BlockSpec((None, 1, d), lay2),
            pl.BlockSpec((None, d, IN_WIDTH), lay2, pipeline_mode=single),
            pl.BlockSpec((None, 1, 128), lay2),
            pl.BlockSpec((None, 1, 128), lay2),
            smem,
            pl.BlockSpec((None, 1, RET_WIDTH), lay2),
            pl.BlockSpec((None, d, d), lay2, pipeline_mode=single),
            pl.BlockSpec(bias_tab.shape, const4, pipeline_mode=single),
            pl.BlockSpec(din.shape, const3, pipeline_mode=single),
            pl.BlockSpec(qdec.shape, const3, pipeline_mode=single),
            pl.BlockSpec(kdec.shape, const3, pipeline_mode=single),
            smem,
        ],
        out_specs=pl.BlockSpec((None, 2, MIX_ROWS, d), lambda i: (res(i), 0, 0, 0)),
        out_shape=jax.ShapeDtypeStruct((n_pairs, 2, MIX_ROWS, d), F32),
        scratch_shapes=[
            pltpu.VMEM((MIX_ROWS, IN_WIDTH), F32),
            pltpu.VMEM((MIX_ROWS, IN_WIDTH), F32),
            pltpu.VMEM((MIX_ROWS, d), BF16),
            pltpu.VMEM((MIX_ROWS, d), BF16),
            pltpu.VMEM((MIX_ROWS, d), BF16),
            pltpu.VMEM((2, MIX_ROWS, d), F32),
            pltpu.VMEM((N_KV_HEADS, 2 * HEAD_DIM, BLOCK + MIX_ROWS), BF16),
            pltpu.VMEM((N_KV_HEADS, 2 * HEAD_DIM, BLOCK + MIX_ROWS), BF16),
            pltpu.VMEM((2 * N_KV_HEADS, BLOCK + MIX_ROWS, 4 * HEAD_DIM), BF16),
            pltpu.VMEM((2 * N_KV_HEADS, BLOCK + MIX_ROWS, 4 * HEAD_DIM), BF16),
            pltpu.VMEM((RET_HEADS, 128, 128), F32),
        ],
        compiler_params=pltpu.CompilerParams(
            dimension_semantics=("arbitrary",), vmem_limit_bytes=VMEM_LIMIT),
        name="mix",
    )(x_pairs, ada, ada, g1, w_in, gq, gk, sinks, rg, w_out, bias_tab, din, qdec, kdec, cdec)
    return out.reshape(b, s, d)


MLP_ROWS = 1024


def _mlp_kernel(x_ref, ada_ref, g2_ref, w1_ref, w2_ref, o_ref):
    x = x_ref[...]
    ms = jnp.mean(x * x, axis=-1, keepdims=True)
    h = x * lax.rsqrt(ms + EPS) * g2_ref[...]
    h = (h * (1.0 + ada_ref[4:5, :]) + ada_ref[3:4, :]).astype(BF16)
    acc = None
    for c in range(D_FF // FF_CHUNK):
        cols = slice(c * FF_CHUNK, (c + 1) * FF_CHUNK)
        a = jnp.dot(h, w1_ref[:, cols], preferred_element_type=F32)
        a = jnp.maximum(a, 0.0)
        part = jnp.dot((a * a).astype(BF16), w2_ref[cols, :], preferred_element_type=F32)
        acc = part if acc is None else acc + part
    o_ref[...] = x + ada_ref[5:6, :] * acc


def _mlp_call(layer, x, ada, g2, w1, w2):
    b, s, d = x.shape
    lay2 = lambda i, j: (layer, 0, 0)
    single = pl.Buffered(1)
    return pl.pallas_call(
        _mlp_kernel,
        grid=(b, s // MLP_ROWS),
        in_specs=[
            pl.BlockSpec((None, MLP_ROWS, d), lambda i, j: (i, j, 0)),
            pl.BlockSpec((None, None, 6, d), lambda i, j: (layer, i, 0, 0)),
            pl.BlockSpec((None, 1, d), lay2),
            pl.BlockSpec((None, d, D_FF), lay2, pipeline_mode=single),
            pl.BlockSpec((None, D_FF, d), lay2, pipeline_mode=single),
        ],
        out_specs=pl.BlockSpec((None, MLP_ROWS, d), lambda i, j: (i, j, 0)),
        out_shape=jax.ShapeDtypeStruct((b, s, d), F32),
        compiler_params=pltpu.CompilerParams(
            dimension_semantics=("arbitrary", "arbitrary"), vmem_limit_bytes=VMEM_LIMIT),
        name="mlp",
    )(x, ada, g2, w1, w2)


def kernel(x, c, norm1_g, norm2_g, w_ada, b_ada, w_in, q_norm_g, k_norm_g, sinks, ret_norm_g,
           w_out, w_mlp1, w_mlp2):
    b = x.shape[0]
    tables = tuple(jnp.asarray(t) for t in _constant_tables())
    c_pad = jnp.pad(c, ((0, 8 - b), (0, 0)))
    ada = _ada_call(c_pad, w_ada, b_ada)[:, :b].reshape(DEPTH, b, 6, D_MODEL)
    g1 = norm1_g[:, None, :]
    g2 = norm2_g[:, None, :]
    gq = jnp.tile(q_norm_g, (1, 2))[:, None, :]
    gk = jnp.tile(k_norm_g, (1, 2))[:, None, :]
    rg = ret_norm_g[:, None, :]
    w_in_b, w_out_b = w_in.astype(BF16), w_out.astype(BF16)
    w1_b, w2_b = w_mlp1.astype(BF16), w_mlp2.astype(BF16)
    for l in range(DEPTH):
        x = _mix_call(l, x, ada, g1, w_in_b, gq, gk, sinks[l], rg, w_out_b, tables)
        x = _mlp_call(l, x, ada, g2, w1_b, w2_b)
    return x
```

```python
import functools

import numpy as np
import jax
import jax.numpy as jnp
from jax import lax
from jax.experimental import pallas as pl
from jax.experimental.pallas import tpu as pltpu

D_MODEL = 1024
DEPTH = 2
ATTN_WIDTH = 512
RET_WIDTH = 512
HEAD_DIM = 64
N_Q_HEADS = 8
N_KV_HEADS = 2
N_PAIRS = N_Q_HEADS // 2
BLOCK = 128
RET_HEADS = 4
RET_DK = 128
D_FF = 4 * D_MODEL
EPS = 1e-6
NEG_INF = -1e30
LOG2E = 1.4426950408889634
IN_WIDTH = 2816
OFF_AQ, OFF_AK, OFF_AV, OFF_RQ, OFF_RK, OFF_RV, OFF_RG = 0, 512, 640, 768, 1280, 1792, 2304

MIX_ROWS = 256
PROJ_CHUNK = 256
FF_CHUNK = 1024
ADA_COLS = 1536
VMEM_LIMIT = 60 * 1024 * 1024

F32 = jnp.float32
BF16 = jnp.bfloat16


@functools.lru_cache(maxsize=None)
def _constant_tables():
    q_pos = np.arange(BLOCK)[:, None]
    k_pos = np.arange(2 * BLOCK)[None, :]
    dist = q_pos + BLOCK - k_pos
    valid = (dist >= 0) & (dist < BLOCK)
    valid_first = valid & (k_pos >= BLOCK)
    slopes = np.exp2(-8.0 * np.arange(1, N_Q_HEADS + 1, dtype=np.float64) / N_Q_HEADS)
    bias = -slopes[:, None, None] * dist[None].astype(np.float64)
    bias_tab = np.stack([np.where(v[None], bias * LOG2E, NEG_INF) for v in (valid_first, valid)])
    bias_tab = bias_tab.astype(np.float32)

    log_gamma = np.log1p(-np.exp2(-5.0 - np.arange(RET_HEADS, dtype=np.float64)))
    idx = np.arange(BLOCK, dtype=np.float64)
    rel = idx[:, None] - idx[None, :]
    k_scale = RET_DK ** -0.5
    decay_in = np.where(rel >= 0, np.exp(log_gamma[:, None, None] * np.maximum(rel, 0.0)), 0.0) * k_scale
    q_decay = np.exp(log_gamma[:, None] * (idx[None, :] + 1.0))
    k_decay = np.exp(log_gamma[:, None] * (BLOCK - 1.0 - idx[None, :])) * k_scale
    q_decay = np.broadcast_to(q_decay[:, :, None], (RET_HEADS, BLOCK, BLOCK))
    k_decay = np.broadcast_to(k_decay[:, :, None], (RET_HEADS, BLOCK, BLOCK))
    chunk_decay = np.exp(log_gamma * BLOCK)
    return (bias_tab, decay_in.astype(np.float32), np.ascontiguousarray(q_decay, np.float32),
            np.ascontiguousarray(k_decay, np.float32), chunk_decay.astype(np.float32))


def _ada_kernel(c_ref, w_ref, b_ref, o_ref):
    c = c_ref[...]
    c_act = c * (1.0 / (1.0 + jnp.exp(-c)))
    o_ref[...] = jnp.dot(c_act.astype(BF16), w_ref[...].astype(BF16),
                         preferred_element_type=F32) + b_ref[...]


def _ada_call(c_pad, w_ada, b_ada):
    rows = c_pad.shape[0]
    n = w_ada.shape[-1]
    return pl.pallas_call(
        _ada_kernel,
        grid=(DEPTH, n // ADA_COLS),
        in_specs=[
            pl.BlockSpec((rows, D_MODEL), lambda l, j: (0, 0)),
            pl.BlockSpec((None, D_MODEL, ADA_COLS), lambda l, j: (l, 0, j)),
            pl.BlockSpec((None, 1, ADA_COLS), lambda l, j: (l, 0, j)),
        ],
        out_specs=pl.BlockSpec((None, rows, ADA_COLS), lambda l, j: (l, 0, j)),
        out_shape=jax.ShapeDtypeStruct((DEPTH, rows, n), F32),
        compiler_params=pltpu.CompilerParams(
            dimension_semantics=("arbitrary", "arbitrary"), vmem_limit_bytes=VMEM_LIMIT),
        name="ada",
    )(c_pad, w_ada, b_ada.reshape(DEPTH, 1, n))


def _mix_slot(chunks, proj_ref, mixed_ref, kwin_ref, kwin_prev, vwin_ref, vwin_prev,
              first_tab, first_keep, gq_ref, gk_ref, sinks_ref, rg_ref,
              bias_ref, din_ref, qdec_ref, kdec_ref, cdec_ref, s_ref):
    chunks = list(chunks)
    n_chunks = len(chunks)
    n_points = (MIX_ROWS // BLOCK) * (2 * N_KV_HEADS + RET_HEADS)
    point = [0]

    def issue_chunk():
        point[0] += 1
        while chunks and (n_chunks - len(chunks)) * n_points < point[0] * n_chunks:
            chunks.pop(0)()

    def head_norm(a, gain_row):
        lo = lax.broadcasted_iota(jnp.int32, a.shape, 1) < HEAD_DIM
        a2 = a * a
        s_lo = jnp.sum(jnp.where(lo, a2, 0.0), axis=-1, keepdims=True)
        s_hi = jnp.sum(jnp.where(lo, 0.0, a2), axis=-1, keepdims=True)
        mean_sq = jnp.where(lo, s_lo, s_hi) * (1.0 / HEAD_DIM)
        return a * lax.rsqrt(mean_sq + EPS) * gain_row

    issue_chunk()

    kn = head_norm(proj_ref[:, OFF_AK:OFF_AK + 128], gk_ref[...])
    kn_t = kn.T.astype(BF16)
    av = proj_ref[:, OFF_AV:OFF_AV + 128]
    av_rot = pltpu.roll(av, HEAD_DIM, axis=1)
    lo_t = lax.broadcasted_iota(jnp.int32, av.shape, 1) < HEAD_DIM
    v_parts = (jnp.where(lo_t, av, 0.0), jnp.where(lo_t, 0.0, av_rot),
               jnp.where(lo_t, av_rot, 0.0), jnp.where(lo_t, 0.0, av))
    for g in range(N_KV_HEADS):
        kwin_ref[g, :, 0:BLOCK] = kwin_prev[g, :, MIX_ROWS:MIX_ROWS + BLOCK]
        kt = kn_t[64 * g:64 * g + 64, :]
        kwin_ref[g, 0:64, BLOCK:BLOCK + MIX_ROWS] = kt
        kwin_ref[g, 64:128, BLOCK:BLOCK + MIX_ROWS] = kt
    for n in range(2 * N_KV_HEADS):
        vwin_ref[n, 0:BLOCK, 0:128] = vwin_prev[n, MIX_ROWS:MIX_ROWS + BLOCK, 0:128]
        vwin_ref[n, BLOCK:BLOCK + MIX_ROWS, 0:128] = v_parts[n].astype(BF16)

    lo = lax.broadcasted_iota(jnp.int32, (BLOCK, 2 * HEAD_DIM), 1) < HEAD_DIM
    gq = gq_ref[...] * (HEAD_DIM ** -0.5 * LOG2E)
    for j in range(MIX_ROWS // BLOCK):
        rows = pl.ds(j * BLOCK, BLOCK)
        win = slice(j * BLOCK, (j + 2) * BLOCK)
        tab = first_tab if j == 0 else 1

        for g in range(N_KV_HEADS):
            if j > 0 or g > 0:
                issue_chunk()
            qs = []
            for pp in range(2):
                p = 2 * g + pp
                qn = head_norm(proj_ref[rows, OFF_AQ + 128 * p:OFF_AQ + 128 * p + 128], gq)
                qs.append(jnp.where(lo, qn, 0.0).astype(BF16))
                qs.append(jnp.where(lo, 0.0, qn).astype(BF16))
            s_all = jnp.dot(jnp.concatenate(qs, axis=0), kwin_ref[g, :, win],
                            preferred_element_type=F32)
            es, ms, sinks = [], [], []
            for hh in range(4):
                head = 4 * g + hh
                sinks.append(sinks_ref[head] * LOG2E)
                sh = s_all[128 * hh:128 * hh + 128, :] + bias_ref[tab, head]
                ms.append(jnp.maximum(jnp.max(sh, axis=-1, keepdims=True), sinks[hh]))
                es.append(jnp.exp2(sh - ms[hh]).astype(BF16))
            issue_chunk()
            o_all = (jnp.dot(jnp.concatenate([es[0], es[2]], axis=0), vwin_ref[2 * g, win, :],
                             preferred_element_type=F32)
                     + jnp.dot(jnp.concatenate([es[1], es[3]], axis=0), vwin_ref[2 * g + 1, win, :],
                               preferred_element_type=F32))
            for pp in range(2):
                p = 2 * g + pp
                m_pair = jnp.where(lo, ms[2 * pp], ms[2 * pp + 1])
                sink_pair = jnp.where(lo, sinks[2 * pp], sinks[2 * pp + 1])
                denom = o_all[128 * pp:128 * pp + 128, 128:256] + jnp.exp2(sink_pair - m_pair)
                o = o_all[128 * pp:128 * pp + 128, 0:128] / denom
                mixed_ref[rows, 128 * p:128 * p + 128] = o.astype(BF16)

        for hd in range(RET_HEADS):
            issue_chunk()
            c0 = 128 * hd
            q_f = proj_ref[rows, OFF_RQ + c0:OFF_RQ + c0 + 128]
            q = q_f.astype(BF16)
            k = proj_ref[rows, OFF_RK + c0:OFF_RK + c0 + 128]
            v = proj_ref[rows, OFF_RV + c0:OFF_RV + c0 + 128].astype(BF16)
            gate = proj_ref[rows, OFF_RG + c0:OFF_RG + c0 + 128]
            inner = lax.dot_general(q, k.astype(BF16), (((1,), (1,)), ((), ())),
                                    preferred_element_type=F32) * din_ref[hd]
            state = s_ref[hd] * first_keep if j == 0 else s_ref[hd]
            o = jnp.dot(jnp.concatenate([inner.astype(BF16), (q_f * qdec_ref[hd]).astype(BF16)], axis=1),
                        jnp.concatenate([v, state.astype(BF16)], axis=0), preferred_element_type=F32)
            kd = (k * kdec_ref[hd]).astype(BF16)
            kv = lax.dot_general(kd, v, (((0,), (0,)), ((), ())), preferred_element_type=F32)
            s_ref[hd] = cdec_ref[hd] * state + kv
            mu = jnp.mean(o, axis=-1, keepdims=True)
            d = o - mu
            var = jnp.mean(d * d, axis=-1, keepdims=True)
            on = d * lax.rsqrt(var + EPS) * rg_ref[:, c0:c0 + 128]
            y = gate * (1.0 / (1.0 + jnp.exp(-gate))) * on
            mixed_ref[rows, ATTN_WIDTH + c0:ATTN_WIDTH + c0 + 128] = y.astype(BF16)

    assert not chunks


def _mix_kernel(tiles_per_seq,
                xn_ref, xr_ref, ada_n_ref, ada_r_ref, g1_ref, win_ref, gq_ref, gk_ref, sinks_ref,
                rg_ref, wout_ref, bias_ref, din_ref, qdec_ref, kdec_ref, cdec_ref, w1_f32_ref, w2_f32_ref,
                o_ref, w1_bf16_ref, w2_bf16_ref,
                proj_a, proj_b, mixed_a, mixed_b, hn_ref, kwin_a, kwin_b, vwin_a, vwin_b, s_ref):
    i = pl.program_id(0)

    w1_bf16_ref[...] = w1_f32_ref[...].astype(BF16)
    w2_bf16_ref[...] = w2_f32_ref[...].astype(BF16)

    @pl.when(i == 0)
    def _():
        proj_b[...] = jnp.zeros_like(proj_b)
        mixed_a[...] = jnp.zeros_like(mixed_a)
        kwin_a[...] = jnp.zeros_like(kwin_a)
        lane = lax.broadcasted_iota(jnp.int32, vwin_a.shape[1:], 1)
        for n in range(2 * N_KV_HEADS):
            sums_at = 2 * HEAD_DIM + HEAD_DIM * (n % 2)
            ones = jnp.where((lane >= sums_at) & (lane < sums_at + HEAD_DIM), 1.0, 0.0).astype(BF16)
            vwin_a[n] = ones
            vwin_b[n] = ones
        s_ref[...] = jnp.zeros_like(s_ref)

    consts = (gq_ref, gk_ref, sinks_ref, rg_ref, bias_ref, din_ref, qdec_ref, kdec_ref, cdec_ref, s_ref)

    def slot(half, proj_cur, proj_next, mixed_cur, mixed_prev, kwin, kwin_prev, vwin, vwin_prev,
             first_tab, first_keep):
        def out_chunk(c):
            cols = slice(c * PROJ_CHUNK, min((c + 1) * PROJ_CHUNK, D_MODEL))
            y = jnp.dot(mixed_prev[...], wout_ref[:, cols], preferred_element_type=F32)
            o_ref[half, :, cols] = xr_ref[half, :, cols] + ada_r_ref[2:3, cols] * y

        def norm_next():
            x = xn_ref[half]
            ms = jnp.mean(x * x, axis=-1, keepdims=True)
            h = x * lax.rsqrt(ms + EPS) * g1_ref[...]
            hn_ref[...] = (h * (1.0 + ada_n_ref[1:2, :]) + ada_n_ref[0:1, :]).astype(BF16)

        def in_chunk(c):
            cols = slice(c * PROJ_CHUNK, min((c + 1) * PROJ_CHUNK, IN_WIDTH))
            proj_next[:, cols] = jnp.dot(hn_ref[...], win_ref[:, cols], preferred_element_type=F32)

        def first_chunk():
            out_chunk(0)
            norm_next()

        chunks = [first_chunk]
        chunks += [functools.partial(out_chunk, c) for c in range(1, pl.cdiv(D_MODEL, PROJ_CHUNK))]
        chunks += [functools.partial(in_chunk, c) for c in range(pl.cdiv(IN_WIDTH, PROJ_CHUNK))]
        _mix_slot(chunks, proj_cur, mixed_cur, kwin, kwin_prev, vwin, vwin_prev,
                  first_tab, first_keep, *consts)

    slot(0, proj_b, proj_a, mixed_b, mixed_a, kwin_b, kwin_a, vwin_b, vwin_a, 1, 1.0)
    seq_start = (2 * i) % tiles_per_seq == 0
    slot(1, proj_a, proj_b, mixed_a, mixed_b, kwin_a, kwin_b, vwin_a, vwin_b,
         jnp.where(seq_start, 0, 1), jnp.where(seq_start, 0.0, 1.0))


def _mix_call(layer, x, ada, g1, w_in, gq, gk, sinks, rg, w_out, tables, w_mlp1, w_mlp2):
    b, s, d = x.shape
    tiles_per_seq = s // MIX_ROWS
    pairs_per_seq = tiles_per_seq // 2
    n_pairs = b * pairs_per_seq
    bias_tab, din, qdec, kdec, cdec = tables
    x_pairs = x.reshape(n_pairs, 2, MIX_ROWS, d)

    nxt = lambda i: jnp.minimum(i, n_pairs - 1)
    res = lambda i: jnp.maximum(i - 1, 0)
    lay2 = lambda i: (layer, 0, 0)
    const2 = lambda i: (0, 0)
    const3 = lambda i: (0, 0, 0)
    const4 = lambda i: (0, 0, 0, 0)
    single = pl.Buffered(1)
    smem = pl.BlockSpec(memory_space=pltpu.SMEM)
    slab1, slab2 = w_mlp1.shape[1] // n_pairs, w_mlp2.shape[1] // n_pairs
    out, w1_b, w2_b = pl.pallas_call(
        functools.partial(_mix_kernel, tiles_per_seq),
        grid=(n_pairs + 1,),
        in_specs=[
            pl.BlockSpec((None, 2, MIX_ROWS, d), lambda i: (nxt(i), 0, 0, 0)),
            pl.BlockSpec((None, 2, MIX_ROWS, d), lambda i: (res(i), 0, 0, 0)),
            pl.BlockSpec((None, None, 6, d), lambda i: (layer, nxt(i) // pairs_per_seq, 0, 0)),
            pl.BlockSpec((None, None, 6, d), lambda i: (layer, res(i) // pairs_per_seq, 0, 0)),
            pl.BlockSpec((None, 1, d), lay2),
            pl.BlockSpec((d, IN_WIDTH), const2, pipeline_mode=single),
            pl.BlockSpec((None, 1, 128), lay2),
            pl.BlockSpec((None, 1, 128), lay2),
            smem,
            pl.BlockSpec((None, 1, RET_WIDTH), lay2),
            pl.BlockSpec((d, d), const2, pipeline_mode=single),
            pl.BlockSpec(bias_tab.shape, const4, pipeline_mode=single),
            pl.BlockSpec(din.shape, const3, pipeline_mode=single),
            pl.BlockSpec(qdec.shape, const3, pipeline_mode=single),
            pl.BlockSpec(kdec.shape, const3, pipeline_mode=single),
            smem,
            pl.BlockSpec((None, slab1, D_FF), lambda i: (layer, nxt(i), 0)),
            pl.BlockSpec((None, slab2, d), lambda i: (layer, nxt(i), 0)),
        ],
        out_specs=[
            pl.BlockSpec((None, 2, MIX_ROWS, d), lambda i: (res(i), 0, 0, 0)),
            pl.BlockSpec((slab1, D_FF), lambda i: (nxt(i), 0)),
            pl.BlockSpec((slab2, d), lambda i: (nxt(i), 0)),
        ],
        out_shape=[
            jax.ShapeDtypeStruct((n_pairs, 2, MIX_ROWS, d), F32),
            jax.ShapeDtypeStruct(w_mlp1.shape[1:], BF16),
            jax.ShapeDtypeStruct(w_mlp2.shape[1:], BF16),
        ],
        scratch_shapes=[
            pltpu.VMEM((MIX_ROWS, IN_WIDTH), F32),
            pltpu.VMEM((MIX_ROWS, IN_WIDTH), F32),
            pltpu.VMEM((MIX_ROWS, d), BF16),
            pltpu.VMEM((MIX_ROWS, d), BF16),
            pltpu.VMEM((MIX_ROWS, d), BF16),
            pltpu.VMEM((N_KV_HEADS, 2 * HEAD_DIM, BLOCK + MIX_ROWS), BF16),
            pltpu.VMEM((N_KV_HEADS, 2 * HEAD_DIM, BLOCK + MIX_ROWS), BF16),
            pltpu.VMEM((2 * N_KV_HEADS, BLOCK + MIX_ROWS, 4 * HEAD_DIM), BF16),
            pltpu.VMEM((2 * N_KV_HEADS, BLOCK + MIX_ROWS, 4 * HEAD_DIM), BF16),
            pltpu.VMEM((RET_HEADS, 128, 128), F32),
        ],
        compiler_params=pltpu.CompilerParams(
            dimension_semantics=("arbitrary",), vmem_limit_bytes=VMEM_LIMIT),
        name="mix",
    )(x_pairs, x_pairs, ada, ada, g1, w_in, gq, gk, sinks, rg, w_out, bias_tab, din, qdec, kdec, cdec,
      w_mlp1, w_mlp2)
    return out.reshape(b, s, d), w1_b, w2_b


MLP_ROWS = 1024


def _mlp_kernel(x_ref, ada_ref, g2_ref, w1_ref, w2_ref, *rest):
    if len(rest) == 1:
        (o_ref,) = rest
    else:
        win_f32_ref, wout_f32_ref, o_ref, win_bf16_ref, wout_bf16_ref = rest
        win_bf16_ref[...] = win_f32_ref[...].astype(BF16)
        wout_bf16_ref[...] = wout_f32_ref[...].astype(BF16)
    x = x_ref[...]
    ms = jnp.mean(x * x, axis=-1, keepdims=True)
    h = x * lax.rsqrt(ms + EPS) * g2_ref[...]
    h = (h * (1.0 + ada_ref[4:5, :]) + ada_ref[3:4, :]).astype(BF16)
    acc = None
    for c in range(D_FF // FF_CHUNK):
        cols = slice(c * FF_CHUNK, (c + 1) * FF_CHUNK)
        a = jnp.dot(h, w1_ref[:, cols], preferred_element_type=F32)
        a = jnp.maximum(a, 0.0)
        part = jnp.dot((a * a).astype(BF16), w2_ref[cols, :], preferred_element_type=F32)
        acc = part if acc is None else acc + part
    o_ref[...] = x + ada_ref[5:6, :] * acc


def _mlp_call(layer, x, ada, g2, w1, w2, w_in=None, w_out=None):
    b, s, d = x.shape
    steps_per_seq = s // MLP_ROWS
    n_steps = b * steps_per_seq
    lay2 = lambda i, j: (layer, 0, 0)
    const2 = lambda i, j: (0, 0)
    single = pl.Buffered(1)
    in_specs = [
        pl.BlockSpec((None, MLP_ROWS, d), lambda i, j: (i, j, 0)),
        pl.BlockSpec((None, None, 6, d), lambda i, j: (layer, i, 0, 0)),
        pl.BlockSpec((None, 1, d), lay2),
        pl.BlockSpec((d, D_FF), const2, pipeline_mode=single),
        pl.BlockSpec((D_FF, d), const2, pipeline_mode=single),
    ]
    out_specs = [pl.BlockSpec((None, MLP_ROWS, d), lambda i, j: (i, j, 0))]
    out_shape = [jax.ShapeDtypeStruct((b, s, d), F32)]
    operands = [x, ada, g2, w1, w2]
    if w_in is not None:
        for w in (w_in, w_out):
            slab = w.shape[1] // n_steps
            step = lambda i, j: i * steps_per_seq + j
            in_specs.append(pl.BlockSpec((None, slab, w.shape[2]), lambda i, j: (layer + 1, step(i, j), 0)))
            out_specs.append(pl.BlockSpec((slab, w.shape[2]), lambda i, j: (step(i, j), 0)))
            out_shape.append(jax.ShapeDtypeStruct(w.shape[1:], BF16))
            operands.append(w)
    return pl.pallas_call(
        _mlp_kernel,
        grid=(b, steps_per_seq),
        in_specs=in_specs,
        out_specs=out_specs,
        out_shape=out_shape,
        compiler_params=pltpu.CompilerParams(
            dimension_semantics=("arbitrary", "arbitrary"), vmem_limit_bytes=VMEM_LIMIT),
        name="mlp",
    )(*operands)


def kernel(x, c, norm1_g, norm2_g, w_ada, b_ada, w_in, q_norm_g, k_norm_g, sinks, ret_norm_g,
           w_out, w_mlp1, w_mlp2):
    b = x.shape[0]
    tables = tuple(jnp.asarray(t) for t in _constant_tables())
    c_pad = jnp.pad(c, ((0, 8 - b), (0, 0)))
    ada = _ada_call(c_pad, w_ada, b_ada)[:, :b].reshape(DEPTH, b, 6, D_MODEL)
    g1 = norm1_g[:, None, :]
    g2 = norm2_g[:, None, :]
    gq = jnp.tile(q_norm_g, (1, 2))[:, None, :]
    gk = jnp.tile(k_norm_g, (1, 2))[:, None, :]
    rg = ret_norm_g[:, None, :]
    w_in_b, w_out_b = w_in[0].astype(BF16), w_out[0].astype(BF16)
    for l in range(DEPTH):
        x, w1_b, w2_b = _mix_call(l, x, ada, g1, w_in_b, gq, gk, sinks[l], rg, w_out_b, tables,
                                  w_mlp1, w_mlp2)
        if l + 1 < DEPTH:
            x, w_in_b, w_out_b = _mlp_call(l, x, ada, g2, w1_b, w2_b, w_in, w_out)
        else:
            (x,) = _mlp_call(l, x, ada, g2, w1_b, w2_b)
    return x
```

```python
import functools

import numpy as np
import jax
import jax.numpy as jnp
from jax import lax
from jax.experimental import pallas as pl
from jax.experimental.pallas import tpu as pltpu

D_MODEL = 1024
DEPTH = 2
ATTN_WIDTH = 512
RET_WIDTH = 512
HEAD_DIM = 64
N_Q_HEADS = 8
N_KV_HEADS = 2
N_PAIRS = N_Q_HEADS // 2
BLOCK = 128
RET_HEADS = 4
RET_DK = 128
D_FF = 4 * D_MODEL
EPS = 1e-6
NEG_INF = -1e30
LOG2E = 1.4426950408889634
IN_WIDTH = 2816
OFF_AQ, OFF_AK, OFF_AV, OFF_RQ, OFF_RK, OFF_RV, OFF_RG = 0, 512, 640, 768, 1280, 1792, 2304

MIX_ROWS = 256
PROJ_CHUNK = 256
FF_CHUNK = 1024
ADA_COLS = 1536
VMEM_LIMIT = 60 * 1024 * 1024

F32 = jnp.float32
BF16 = jnp.bfloat16


@functools.lru_cache(maxsize=None)
def _constant_tables():
    q_pos = np.arange(BLOCK)[:, None]
    k_pos = np.arange(2 * BLOCK)[None, :]
    dist = q_pos + BLOCK - k_pos
    valid = (dist >= 0) & (dist < BLOCK)
    valid_first = valid & (k_pos >= BLOCK)
    slopes = np.exp2(-8.0 * np.arange(1, N_Q_HEADS + 1, dtype=np.float64) / N_Q_HEADS)
    bias = -slopes[:, None, None] * dist[None].astype(np.float64)
    bias_tab = np.stack([np.where(v[None], bias * LOG2E, NEG_INF) for v in (valid_first, valid)])
    bias_tab = bias_tab.astype(np.float32)

    log_gamma = np.log1p(-np.exp2(-5.0 - np.arange(RET_HEADS, dtype=np.float64)))
    idx = np.arange(BLOCK, dtype=np.float64)
    rel = idx[:, None] - idx[None, :]
    k_scale = RET_DK ** -0.5
    decay_in = np.where(rel >= 0, np.exp(log_gamma[:, None, None] * np.maximum(rel, 0.0)), 0.0) * k_scale
    q_decay = np.exp(log_gamma[:, None] * (idx[None, :] + 1.0))
    k_decay = np.exp(log_gamma[:, None] * (BLOCK - 1.0 - idx[None, :])) * k_scale
    q_decay = np.broadcast_to(q_decay[:, :, None], (RET_HEADS, BLOCK, BLOCK))
    k_decay = np.broadcast_to(k_decay[:, :, None], (RET_HEADS, BLOCK, BLOCK))
    chunk_decay = np.exp(log_gamma * BLOCK)
    ret_tab = np.stack([decay_in, q_decay, k_decay]).astype(np.float32)
    return bias_tab, ret_tab, chunk_decay.astype(np.float32)


def _ada_kernel(c_ref, w_ref, b_ref, o_ref):
    c = c_ref[...]
    c_act = c * (1.0 / (1.0 + jnp.exp(-c)))
    o_ref[...] = jnp.dot(c_act.astype(BF16), w_ref[...].astype(BF16),
                         preferred_element_type=F32) + b_ref[...]


def _ada_call(c_pad, w_ada, b_ada):
    rows = c_pad.shape[0]
    n = w_ada.shape[-1]
    return pl.pallas_call(
        _ada_kernel,
        grid=(DEPTH, n // ADA_COLS),
        in_specs=[
            pl.BlockSpec((rows, D_MODEL), lambda l, j: (0, 0)),
            pl.BlockSpec((None, D_MODEL, ADA_COLS), lambda l, j: (l, 0, j)),
            pl.BlockSpec((None, 1, ADA_COLS), lambda l, j: (l, 0, j)),
        ],
        out_specs=pl.BlockSpec((None, rows, ADA_COLS), lambda l, j: (l, 0, j)),
        out_shape=jax.ShapeDtypeStruct((DEPTH, rows, n), F32),
        compiler_params=pltpu.CompilerParams(
            dimension_semantics=("arbitrary", "arbitrary"), vmem_limit_bytes=VMEM_LIMIT),
        name="ada",
    )(c_pad, w_ada, b_ada.reshape(DEPTH, 1, n))


def _head_norm(a, gain_row):
    lo = lax.broadcasted_iota(jnp.int32, a.shape, 1) < HEAD_DIM
    a2 = a * a
    s_lo = jnp.sum(jnp.where(lo, a2, 0.0), axis=-1, keepdims=True)
    s_hi = jnp.sum(jnp.where(lo, 0.0, a2), axis=-1, keepdims=True)
    mean_sq = jnp.where(lo, s_lo, s_hi) * (1.0 / HEAD_DIM)
    return a * lax.rsqrt(mean_sq + EPS) * gain_row


def _pair_gain(g_ref, layer):
    row = g_ref[layer:layer + 1, :]
    return jnp.concatenate([row, row], axis=1)


def _build_windows(proj_ref, gk_row, kwin_ref, kwin_prev, vwin_ref, vwin_prev):
    kn = _head_norm(proj_ref[:, OFF_AK:OFF_AK + 128], gk_row)
    kn_t = kn.T.astype(BF16)
    av = proj_ref[:, OFF_AV:OFF_AV + 128]
    av_rot = pltpu.roll(av, HEAD_DIM, axis=1)
    lo_t = lax.broadcasted_iota(jnp.int32, av.shape, 1) < HEAD_DIM
    v_parts = (jnp.where(lo_t, av, 0.0), jnp.where(lo_t, 0.0, av_rot),
               jnp.where(lo_t, av_rot, 0.0), jnp.where(lo_t, 0.0, av))
    for g in range(N_KV_HEADS):
        kwin_ref[g, :, 0:BLOCK] = kwin_prev[g, :, MIX_ROWS:MIX_ROWS + BLOCK]
        kt = kn_t[64 * g:64 * g + 64, :]
        kwin_ref[g, 0:64, BLOCK:BLOCK + MIX_ROWS] = kt
        kwin_ref[g, 64:128, BLOCK:BLOCK + MIX_ROWS] = kt
    for n in range(2 * N_KV_HEADS):
        vwin_ref[n, 0:BLOCK, 0:128] = vwin_prev[n, MIX_ROWS:MIX_ROWS + BLOCK, 0:128]
        vwin_ref[n, BLOCK:BLOCK + MIX_ROWS, 0:128] = v_parts[n].astype(BF16)


def _mix_slot(chunks, proj_ref, mixed_ref, kwin_ref, vwin_ref,
              first_tab, first_keep, layer, gq_ref, sinks_ref, rg_ref,
              bias_ref, ret_ref, cdec_ref, s_ref):
    chunks = list(chunks)
    n_chunks = len(chunks)
    n_points = 5 + 5 * (MIX_ROWS // BLOCK)
    point = [0]

    def issue_chunk():
        point[0] += 1
        while chunks and (n_chunks - len(chunks)) * n_points < point[0] * n_chunks:
            chunks.pop(0)()

    issue_chunk()

    lo = lax.broadcasted_iota(jnp.int32, (BLOCK, 2 * HEAD_DIM), 1) < HEAD_DIM
    gq = _pair_gain(gq_ref, layer) * (HEAD_DIM ** -0.5 * LOG2E)
    n_blocks = MIX_ROWS // BLOCK
    rows = [pl.ds(j * BLOCK, BLOCK) for j in range(n_blocks)]
    wins = [slice(j * BLOCK, (j + 2) * BLOCK) for j in range(n_blocks)]
    tabs = [first_tab if j == 0 else 1 for j in range(n_blocks)]

    s_all = {}
    for j in range(n_blocks):
        for g in range(N_KV_HEADS):
            qs = []
            for pp in range(2):
                p = 2 * g + pp
                qn = _head_norm(proj_ref[rows[j], OFF_AQ + 128 * p:OFF_AQ + 128 * p + 128], gq)
                qs.append(jnp.where(lo, qn, 0.0).astype(BF16))
                qs.append(jnp.where(lo, 0.0, qn).astype(BF16))
            s_all[j, g] = jnp.dot(jnp.concatenate(qs, axis=0), kwin_ref[g, :, wins[j]],
                                  preferred_element_type=F32)
    issue_chunk()
    sinks = [sinks_ref[layer, head] * LOG2E for head in range(N_Q_HEADS)]
    es, ms = {}, {}
    for j in range(n_blocks):
        for head in range(N_Q_HEADS):
            g, hh = divmod(head, 4)
            sh = s_all[j, g][128 * hh:128 * hh + 128, :] + bias_ref[tabs[j], head]
            ms[j, head] = jnp.maximum(jnp.max(sh, axis=-1, keepdims=True), sinks[head])
            es[j, head] = jnp.exp2(sh - ms[j, head]).astype(BF16)
            if head % 4 == 3:
                issue_chunk()
    o_all = {}
    for j in range(n_blocks):
        for g in range(N_KV_HEADS):
            h0 = 4 * g
            o_all[j, g] = (
                jnp.dot(jnp.concatenate([es[j, h0], es[j, h0 + 2]], axis=0), vwin_ref[2 * g, wins[j], :],
                        preferred_element_type=F32)
                + jnp.dot(jnp.concatenate([es[j, h0 + 1], es[j, h0 + 3]], axis=0),
                          vwin_ref[2 * g + 1, wins[j], :], preferred_element_type=F32))
    issue_chunk()
    for j in range(n_blocks):
        for p in range(N_PAIRS):
            g, pp = divmod(p, 2)
            m_pair = jnp.where(lo, ms[j, 2 * p], ms[j, 2 * p + 1])
            sink_pair = jnp.where(lo, sinks[2 * p], sinks[2 * p + 1])
            denom = o_all[j, g][128 * pp:128 * pp + 128, 128:256] + jnp.exp2(sink_pair - m_pair)
            o = o_all[j, g][128 * pp:128 * pp + 128, 0:128] / denom
            mixed_ref[rows[j], 128 * p:128 * p + 128] = o.astype(BF16)
        issue_chunk()

    heads = {}
    for j in range(n_blocks):
        for hd in range(RET_HEADS):
            c0 = 128 * hd
            q_f = proj_ref[rows[j], OFF_RQ + c0:OFF_RQ + c0 + 128]
            k = proj_ref[rows[j], OFF_RK + c0:OFF_RK + c0 + 128]
            v = proj_ref[rows[j], OFF_RV + c0:OFF_RV + c0 + 128].astype(BF16)
            inner = lax.dot_general(q_f.astype(BF16), k.astype(BF16), (((1,), (1,)), ((), ())),
                                    preferred_element_type=F32)
            kd = (k * ret_ref[2, hd]).astype(BF16)
            kv = lax.dot_general(kd, v, (((0,), (0,)), ((), ())), preferred_element_type=F32)
            heads[j, hd] = (q_f, v, inner, kv)
    issue_chunk()
    outs = {}
    for j in range(n_blocks):
        for hd in range(RET_HEADS):
            q_f, v, inner, kv = heads[j, hd]
            state = s_ref[hd] * first_keep if j == 0 else s_ref[hd]
            lhs = jnp.concatenate([(inner * ret_ref[0, hd]).astype(BF16), (q_f * ret_ref[1, hd]).astype(BF16)],
                                  axis=1)
            outs[j, hd] = jnp.dot(lhs, jnp.concatenate([v, state.astype(BF16)], axis=0),
                                  preferred_element_type=F32)
            s_ref[hd] = cdec_ref[hd] * state + kv
    issue_chunk()
    for j in range(n_blocks):
        for hd in range(RET_HEADS):
            c0 = 128 * hd
            o = outs[j, hd]
            gate = proj_ref[rows[j], OFF_RG + c0:OFF_RG + c0 + 128]
            mu = jnp.mean(o, axis=-1, keepdims=True)
            d = o - mu
            var = jnp.mean(d * d, axis=-1, keepdims=True)
            on = d * lax.rsqrt(var + EPS) * rg_ref[layer:layer + 1, c0:c0 + 128]
            y = gate * (1.0 / (1.0 + jnp.exp(-gate))) * on
            mixed_ref[rows[j], ATTN_WIDTH + c0:ATTN_WIDTH + c0 + 128] = y.astype(BF16)
            if hd % 2 == 1:
                issue_chunk()

    assert not chunks


def _mix_kernel(layer, tiles_per_seq,
                xn_ref, xr_ref, ada_n_ref, ada_r_ref, g1_ref, win_ref, gq_ref, gk_ref, sinks_ref,
                rg_ref, wout_ref, bias_ref, ret_ref, cdec_ref, w1_f32_ref, w2_f32_ref,
                o_ref, w1_bf16_ref, w2_bf16_ref,
                proj_a, proj_b, mixed_a, mixed_b, hn_ref, kwin_a, kwin_b, vwin_a, vwin_b, s_ref):
    i = pl.program_id(0)

    w1_bf16_ref[...] = w1_f32_ref[...].astype(BF16)
    w2_bf16_ref[...] = w2_f32_ref[...].astype(BF16)

    @pl.when(i == 0)
    def _():
        proj_b[...] = jnp.zeros_like(proj_b)
        mixed_a[...] = jnp.zeros_like(mixed_a)
        kwin_a[...] = jnp.zeros_like(kwin_a)
        kwin_b[...] = jnp.zeros_like(kwin_b)
        lane = lax.broadcasted_iota(jnp.int32, vwin_a.shape[1:], 1)
        for n in range(2 * N_KV_HEADS):
            sums_at = 2 * HEAD_DIM + HEAD_DIM * (n % 2)
            ones = jnp.where((lane >= sums_at) & (lane < sums_at + HEAD_DIM), 1.0, 0.0).astype(BF16)
            vwin_a[n] = ones
            vwin_b[n] = ones
        s_ref[...] = jnp.zeros_like(s_ref)

    consts = (layer, gq_ref, sinks_ref, rg_ref, bias_ref, ret_ref, cdec_ref, s_ref)

    def slot(half, proj_cur, proj_next, mixed_cur, mixed_prev, kwin, kwin_next, vwin, vwin_next,
             first_tab, first_keep):
        def out_chunk(c):
            cols = slice(c * PROJ_CHUNK, min((c + 1) * PROJ_CHUNK, D_MODEL))
            y = jnp.dot(mixed_prev[...], wout_ref[:, cols], preferred_element_type=F32)
            o_ref[half, :, cols] = xr_ref[half, :, cols] + ada_r_ref[2:3, cols] * y

        def norm_next():
            x = xn_ref[half]
            ms = jnp.mean(x * x, axis=-1, keepdims=True)
            h = x * lax.rsqrt(ms + EPS) * g1_ref[layer:layer + 1, :]
            hn_ref[...] = (h * (1.0 + ada_n_ref[1:2, :]) + ada_n_ref[0:1, :]).astype(BF16)

        def in_chunk(c):
            cols = slice(c * PROJ_CHUNK, min((c + 1) * PROJ_CHUNK, IN_WIDTH))
            proj_next[:, cols] = jnp.dot(hn_ref[...], win_ref[:, cols], preferred_element_type=F32)
            if cols.start <= OFF_AK and OFF_AV + 2 * HEAD_DIM <= cols.stop:
                _build_windows(proj_next, _pair_gain(gk_ref, layer), kwin_next, kwin, vwin_next, vwin)

        def first_chunk():
            out_chunk(0)
            norm_next()
            out_chunk(1)

        chunks = [first_chunk]
        chunks += [functools.partial(out_chunk, c) for c in range(2, pl.cdiv(D_MODEL, PROJ_CHUNK))]
        chunks += [functools.partial(in_chunk, c) for c in range(pl.cdiv(IN_WIDTH, PROJ_CHUNK))]
        _mix_slot(chunks, proj_cur, mixed_cur, kwin, vwin, first_tab, first_keep, *consts)

    slot(0, proj_b, proj_a, mixed_b, mixed_a, kwin_b, kwin_a, vwin_b, vwin_a, 1, 1.0)
    seq_start = (2 * i) % tiles_per_seq == 0
    slot(1, proj_a, proj_b, mixed_a, mixed_b, kwin_a, kwin_b, vwin_a, vwin_b,
         jnp.where(seq_start, 0, 1), jnp.where(seq_start, 0.0, 1.0))


def _mix_call(layer, x, ada, g1, w_in, gq, gk, sinks, rg, w_out, tables, w_mlp1, w_mlp2):
    b, s, d = x.shape
    tiles_per_seq = s // MIX_ROWS
    pairs_per_seq = tiles_per_seq // 2
    n_pairs = b * pairs_per_seq
    bias_tab, ret_tab, cdec = tables
    x_pairs = x.reshape(n_pairs, 2, MIX_ROWS, d)

    nxt = lambda i: jnp.minimum(i, n_pairs - 1)
    res = lambda i: jnp.maximum(i - 1, 0)
    const2 = lambda i: (0, 0)
    const4 = lambda i: (0, 0, 0, 0)
    single = pl.Buffered(1)
    smem = pl.BlockSpec(memory_space=pltpu.SMEM)
    slab1, slab2 = w_mlp1.shape[1] // n_pairs, w_mlp2.shape[1] // n_pairs
    out, w1_b, w2_b = pl.pallas_call(
        functools.partial(_mix_kernel, layer, tiles_per_seq),
        grid=(n_pairs + 1,),
        in_specs=[
            pl.BlockSpec((None, 2, MIX_ROWS, d), lambda i: (nxt(i), 0, 0, 0)),
            pl.BlockSpec((None, 2, MIX_ROWS, d), lambda i: (res(i), 0, 0, 0)),
            pl.BlockSpec((None, None, 6, d), lambda i: (layer, nxt(i) // pairs_per_seq, 0, 0)),
            pl.BlockSpec((None, None, 6, d), lambda i: (layer, res(i) // pairs_per_seq, 0, 0)),
            pl.BlockSpec(g1.shape, const2),
            pl.BlockSpec((d, IN_WIDTH), const2, pipeline_mode=single),
            pl.BlockSpec(gq.shape, const2),
            pl.BlockSpec(gk.shape, const2),
            smem,
            pl.BlockSpec(rg.shape, const2),
            pl.BlockSpec((d, d), const2, pipeline_mode=single),
            pl.BlockSpec(bias_tab.shape, const4, pipeline_mode=single),
            pl.BlockSpec(ret_tab.shape, const4, pipeline_mode=single),
            smem,
            pl.BlockSpec((None, slab1, D_FF), lambda i: (layer, nxt(i), 0)),
            pl.BlockSpec((None, slab2, d), lambda i: (layer, nxt(i), 0)),
        ],
        out_specs=[
            pl.BlockSpec((None, 2, MIX_ROWS, d), lambda i: (res(i), 0, 0, 0)),
            pl.BlockSpec((slab1, D_FF), lambda i: (nxt(i), 0)),
            pl.BlockSpec((slab2, d), lambda i: (nxt(i), 0)),
        ],
        out_shape=[
            jax.ShapeDtypeStruct((n_pairs, 2, MIX_ROWS, d), F32),
            jax.ShapeDtypeStruct(w_mlp1.shape[1:], BF16),
            jax.ShapeDtypeStruct(w_mlp2.shape[1:], BF16),
        ],
        scratch_shapes=[
            pltpu.VMEM((MIX_ROWS, IN_WIDTH), F32),
            pltpu.VMEM((MIX_ROWS, IN_WIDTH), F32),
            pltpu.VMEM((MIX_ROWS, d), BF16),
            pltpu.VMEM((MIX_ROWS, d), BF16),
            pltpu.VMEM((MIX_ROWS, d), BF16),
            pltpu.VMEM((N_KV_HEADS, 2 * HEAD_DIM, BLOCK + MIX_ROWS), BF16),
            pltpu.VMEM((N_KV_HEADS, 2 * HEAD_DIM, BLOCK + MIX_ROWS), BF16),
            pltpu.VMEM((2 * N_KV_HEADS, BLOCK + MIX_ROWS, 4 * HEAD_DIM), BF16),
            pltpu.VMEM((2 * N_KV_HEADS, BLOCK + MIX_ROWS, 4 * HEAD_DIM), BF16),
            pltpu.VMEM((RET_HEADS, 128, 128), F32),
        ],
        compiler_params=pltpu.CompilerParams(
            dimension_semantics=("arbitrary",), vmem_limit_bytes=VMEM_LIMIT),
        name="mix",
    )(x_pairs, x_pairs, ada, ada, g1, w_in, gq, gk, sinks, rg, w_out, bias_tab, ret_tab, cdec,
      w_mlp1, w_mlp2)
    return out.reshape(b, s, d), w1_b, w2_b


MLP_ROWS = 1024


def _mlp_kernel(layer, x_ref, ada_ref, g2_ref, w1_ref, w2_ref, *rest):
    if len(rest) == 1:
        (o_ref,) = rest
    else:
        win_f32_ref, wout_f32_ref, o_ref, win_bf16_ref, wout_bf16_ref = rest
        win_bf16_ref[...] = win_f32_ref[...].astype(BF16)
        wout_bf16_ref[...] = wout_f32_ref[...].astype(BF16)
    x = x_ref[...]
    ms = jnp.mean(x * x, axis=-1, keepdims=True)
    h = x * lax.rsqrt(ms + EPS) * g2_ref[layer:layer + 1, :]
    h = (h * (1.0 + ada_ref[4:5, :]) + ada_ref[3:4, :]).astype(BF16)
    acc = None
    for c in range(D_FF // FF_CHUNK):
        cols = slice(c * FF_CHUNK, (c + 1) * FF_CHUNK)
        a = jnp.dot(h, w1_ref[:, cols], preferred_element_type=F32)
        a = jnp.maximum(a, 0.0)
        part = jnp.dot((a * a).astype(BF16), w2_ref[cols, :], preferred_element_type=F32)
        acc = part if acc is None else acc + part
    o_ref[...] = x + ada_ref[5:6, :] * acc


def _mlp_call(layer, x, ada, g2, w1, w2, w_in=None, w_out=None):
    b, s, d = x.shape
    steps_per_seq = s // MLP_ROWS
    n_steps = b * steps_per_seq
    const2 = lambda i, j: (0, 0)
    single = pl.Buffered(1)
    in_specs = [
        pl.BlockSpec((None, MLP_ROWS, d), lambda i, j: (i, j, 0)),
        pl.BlockSpec((None, None, 6, d), lambda i, j: (layer, i, 0, 0)),
        pl.BlockSpec(g2.shape, const2),
        pl.BlockSpec((d, D_FF), const2, pipeline_mode=single),
        pl.BlockSpec((D_FF, d), const2, pipeline_mode=single),
    ]
    out_specs = [pl.BlockSpec((None, MLP_ROWS, d), lambda i, j: (i, j, 0))]
    out_shape = [jax.ShapeDtypeStruct((b, s, d), F32)]
    operands = [x, ada, g2, w1, w2]
    if w_in is not None:
        for w in (w_in, w_out):
            slab = w.shape[1] // n_steps
            step = lambda i, j: i * steps_per_seq + j
            in_specs.append(pl.BlockSpec((None, slab, w.shape[2]), lambda i, j: (layer + 1, step(i, j), 0)))
            out_specs.append(pl.BlockSpec((slab, w.shape[2]), lambda i, j: (step(i, j), 0)))
            out_shape.append(jax.ShapeDtypeStruct(w.shape[1:], BF16))
            operands.append(w)
    return pl.pallas_call(
        functools.partial(_mlp_kernel, layer),
        grid=(b, steps_per_seq),
        in_specs=in_specs,
        out_specs=out_specs,
        out_shape=out_shape,
        compiler_params=pltpu.CompilerParams(
            dimension_semantics=("arbitrary", "arbitrary"), vmem_limit_bytes=VMEM_LIMIT),
        name="mlp",
    )(*operands)


def kernel(x, c, norm1_g, norm2_g, w_ada, b_ada, w_in, q_norm_g, k_norm_g, sinks, ret_norm_g,
           w_out, w_mlp1, w_mlp2):
    b = x.shape[0]
    tables = tuple(jnp.asarray(t) for t in _constant_tables())
    c_pad = jnp.pad(c, ((0, 8 - b), (0, 0)))
    ada = _ada_call(c_pad, w_ada, b_ada)[:, :b].reshape(DEPTH, b, 6, D_MODEL)
    w_in_b, w_out_b = w_in[0].astype(BF16), w_out[0].astype(BF16)
    for l in range(DEPTH):
        x, w1_b, w2_b = _mix_call(l, x, ada, norm1_g, w_in_b, q_norm_g, k_norm_g, sinks, ret_norm_g,
                                  w_out_b, tables, w_mlp1, w_mlp2)
        if l + 1 < DEPTH:
            x, w_in_b, w_out_b = _mlp_call(l, x, ada, norm2_g, w1_b, w2_b, w_in, w_out)
        else:
            (x,) = _mlp_call(l, x, ada, norm2_g, w1_b, w2_b)
    return x
```

```python
import functools

import numpy as np
import jax
import jax.numpy as jnp
from jax import lax
from jax.experimental import pallas as pl
from jax.experimental.pallas import tpu as pltpu

D_MODEL = 1024
DEPTH = 2
ATTN_WIDTH = 512
RET_WIDTH = 512
HEAD_DIM = 64
N_Q_HEADS = 8
N_KV_HEADS = 2
N_PAIRS = N_Q_HEADS // 2
BLOCK = 128
RET_HEADS = 4
RET_DK = 128
D_FF = 4 * D_MODEL
EPS = 1e-6
NEG_INF = -1e30
LOG2E = 1.4426950408889634
IN_WIDTH = 2816
OFF_AQ, OFF_AK, OFF_AV, OFF_RQ, OFF_RK, OFF_RV, OFF_RG = 0, 512, 640, 768, 1280, 1792, 2304

MIX_ROWS = 256
PROJ_CHUNK = 256
FF_CHUNK = 1024
ADA_COLS = 1536
VMEM_LIMIT = 60 * 1024 * 1024

F32 = jnp.float32
BF16 = jnp.bfloat16


@functools.lru_cache(maxsize=None)
def _constant_tables():
    q_pos = np.arange(BLOCK)[:, None]
    k_pos = np.arange(2 * BLOCK)[None, :]
    dist = q_pos + BLOCK - k_pos
    valid = (dist >= 0) & (dist < BLOCK)
    valid_first = valid & (k_pos >= BLOCK)
    slopes = np.exp2(-8.0 * np.arange(1, N_Q_HEADS + 1, dtype=np.float64) / N_Q_HEADS)
    bias = -slopes[:, None, None] * dist[None].astype(np.float64)
    bias_tab = np.stack([np.where(v[None], bias * LOG2E, NEG_INF) for v in (valid_first, valid)])
    bias_tab = np.ascontiguousarray(bias_tab.transpose(0, 1, 3, 2), np.float32)

    log_gamma = np.log1p(-np.exp2(-5.0 - np.arange(RET_HEADS, dtype=np.float64)))
    idx = np.arange(BLOCK, dtype=np.float64)
    rel = idx[:, None] - idx[None, :]
    k_scale = RET_DK ** -0.5
    decay_in = np.where(rel >= 0, np.exp(log_gamma[:, None, None] * np.maximum(rel, 0.0)), 0.0) * k_scale
    q_decay = np.exp(log_gamma[:, None] * (idx[None, :] + 1.0))
    k_decay = np.exp(log_gamma[:, None] * (BLOCK - 1.0 - idx[None, :])) * k_scale
    q_decay = np.broadcast_to(q_decay[:, :, None], (RET_HEADS, BLOCK, BLOCK))
    k_decay = np.broadcast_to(k_decay[:, :, None], (RET_HEADS, BLOCK, BLOCK))
    chunk_decay = np.exp(log_gamma * BLOCK)
    ret_tab = np.stack([decay_in, q_decay, k_decay]).astype(np.float32)
    return bias_tab, ret_tab, chunk_decay.astype(np.float32)


def _ada_kernel(c_ref, w_ref, b_ref, o_ref):
    c = c_ref[...]
    c_act = c * (1.0 / (1.0 + jnp.exp(-c)))
    o_ref[...] = jnp.dot(c_act.astype(BF16), w_ref[...].astype(BF16),
                         preferred_element_type=F32) + b_ref[...]


def _ada_call(c_pad, w_ada, b_ada):
    rows = c_pad.shape[0]
    n = w_ada.shape[-1]
    return pl.pallas_call(
        _ada_kernel,
        grid=(DEPTH, n // ADA_COLS),
        in_specs=[
            pl.BlockSpec((rows, D_MODEL), lambda l, j: (0, 0)),
            pl.BlockSpec((None, D_MODEL, ADA_COLS), lambda l, j: (l, 0, j)),
            pl.BlockSpec((None, 1, ADA_COLS), lambda l, j: (l, 0, j)),
        ],
        out_specs=pl.BlockSpec((None, rows, ADA_COLS), lambda l, j: (l, 0, j)),
        out_shape=jax.ShapeDtypeStruct((DEPTH, rows, n), F32),
        compiler_params=pltpu.CompilerParams(
            dimension_semantics=("arbitrary", "arbitrary"), vmem_limit_bytes=VMEM_LIMIT),
        name="ada",
    )(c_pad, w_ada, b_ada.reshape(DEPTH, 1, n))


def _head_norm(a, gain_row):
    lo = lax.broadcasted_iota(jnp.int32, a.shape, 1) < HEAD_DIM
    a2 = a * a
    s_lo = jnp.sum(jnp.where(lo, a2, 0.0), axis=-1, keepdims=True)
    s_hi = jnp.sum(jnp.where(lo, 0.0, a2), axis=-1, keepdims=True)
    mean_sq = jnp.where(lo, s_lo, s_hi) * (1.0 / HEAD_DIM)
    return a * lax.rsqrt(mean_sq + EPS) * gain_row


def _pair_gain(g_ref, layer):
    row = g_ref[layer:layer + 1, :]
    return jnp.concatenate([row, row], axis=1)


def _build_windows(proj_ref, gk_row, kwin_ref, kwin_prev, vwin_ref, vwin_prev):
    kn = _head_norm(proj_ref[:, OFF_AK:OFF_AK + 128], gk_row)
    kn_rot = pltpu.roll(kn, HEAD_DIM, axis=1)
    lo = lax.broadcasted_iota(jnp.int32, kn.shape, 1) < HEAD_DIM
    k_dup = (jnp.where(lo, kn, kn_rot), jnp.where(lo, kn_rot, kn))
    for g in range(N_KV_HEADS):
        kwin_ref[g, 0:BLOCK, :] = kwin_prev[g, MIX_ROWS:MIX_ROWS + BLOCK, :]
        kwin_ref[g, BLOCK:BLOCK + MIX_ROWS, :] = k_dup[g].astype(BF16)
    vwin_ref[:, 0:BLOCK] = vwin_prev[:, MIX_ROWS:MIX_ROWS + BLOCK]
    vwin_ref[:, BLOCK:BLOCK + MIX_ROWS] = proj_ref[:, OFF_AV:OFF_AV + 128].T.astype(BF16)


def _mix_slot(chunks, proj_ref, mixed_ref, kwin_ref, vwin_ref,
              first_tab, first_keep, layer, gq_ref, sinks_ref, rg_ref,
              bias_ref, ret_ref, cdec_ref, s_ref):
    chunks = list(chunks)
    n_chunks = len(chunks)
    n_points = 5 + 5 * (MIX_ROWS // BLOCK)
    point = [0]

    def issue_chunk():
        point[0] += 1
        while chunks and (n_chunks - len(chunks)) * n_points < point[0] * n_chunks:
            chunks.pop(0)()

    issue_chunk()

    lo = lax.broadcasted_iota(jnp.int32, (BLOCK, 2 * HEAD_DIM), 1) < HEAD_DIM
    gq = _pair_gain(gq_ref, layer) * (HEAD_DIM ** -0.5 * LOG2E)
    n_blocks = MIX_ROWS // BLOCK
    rows = [pl.ds(j * BLOCK, BLOCK) for j in range(n_blocks)]
    wins = [slice(j * BLOCK, (j + 2) * BLOCK) for j in range(n_blocks)]
    tabs = [first_tab if j == 0 else 1 for j in range(n_blocks)]

    s_t = {}
    for j in range(n_blocks):
        for g in range(N_KV_HEADS):
            qs = []
            for pp in range(2):
                p = 2 * g + pp
                qn = _head_norm(proj_ref[rows[j], OFF_AQ + 128 * p:OFF_AQ + 128 * p + 128], gq)
                qs.append(jnp.where(lo, qn, 0.0).astype(BF16))
                qs.append(jnp.where(lo, 0.0, qn).astype(BF16))
            s_t[j, g] = lax.dot_general(kwin_ref[g, wins[j], :], jnp.concatenate(qs, axis=0),
                                        (((1,), (1,)), ((), ())),
                                        preferred_element_type=F32)
    issue_chunk()
    sinks = [sinks_ref[layer, head] * LOG2E for head in range(N_Q_HEADS)]
    es, ms = {}, {}
    for j in range(n_blocks):
        for head in range(N_Q_HEADS):
            g, hh = divmod(head, 4)
            sh = s_t[j, g][:, 128 * hh:128 * hh + 128] + bias_ref[tabs[j], head]
            ms[j, head] = jnp.maximum(jnp.max(sh, axis=0, keepdims=True), sinks[head])
            es[j, head] = jnp.exp2(sh - ms[j, head]).astype(BF16)
            if head % 4 == 3:
                issue_chunk()
    ones = jnp.ones((HEAD_DIM, 2 * BLOCK), BF16)
    o_t = {}
    for j in range(n_blocks):
        for g in range(N_KV_HEADS):
            lhs = jnp.concatenate([vwin_ref[64 * g:64 * g + 64, wins[j]], ones], axis=0)
            p_t = jnp.concatenate([es[j, 4 * g + hh] for hh in range(4)], axis=1)
            o_t[j, g] = jnp.dot(lhs, p_t, preferred_element_type=F32)
    issue_chunk()
    for j in range(n_blocks):
        for p in range(N_PAIRS):
            g, pp = divmod(p, 2)
            halves = []
            for head in (2 * p, 2 * p + 1):
                cols = slice(128 * (head % 4), 128 * (head % 4) + 128)
                denom = o_t[j, g][HEAD_DIM:, cols] + jnp.exp2(sinks[head] - ms[j, head])
                halves.append(o_t[j, g][:HEAD_DIM, cols] / denom)
            o = jnp.concatenate(halves, axis=0).T
            mixed_ref[rows[j], 128 * p:128 * p + 128] = o.astype(BF16)
        issue_chunk()

    heads = {}
    for j in range(n_blocks):
        for hd in range(RET_HEADS):
            c0 = 128 * hd
            q_f = proj_ref[rows[j], OFF_RQ + c0:OFF_RQ + c0 + 128]
            k = proj_ref[rows[j], OFF_RK + c0:OFF_RK + c0 + 128]
            v = proj_ref[rows[j], OFF_RV + c0:OFF_RV + c0 + 128].astype(BF16)
            inner = lax.dot_general(q_f.astype(BF16), k.astype(BF16), (((1,), (1,)), ((), ())),
                                    preferred_element_type=F32)
            kd = (k * ret_ref[2, hd]).astype(BF16)
            kv = lax.dot_general(kd, v, (((0,), (0,)), ((), ())), preferred_element_type=F32)
            heads[j, hd] = (q_f, v, inner, kv)
    issue_chunk()
    outs = {}
    for j in range(n_blocks):
        for hd in range(RET_HEADS):
            q_f, v, inner, kv = heads[j, hd]
            state = s_ref[hd] * first_keep if j == 0 else s_ref[hd]
            lhs = jnp.concatenate([(inner * ret_ref[0, hd]).astype(BF16), (q_f * ret_ref[1, hd]).astype(BF16)],
                                  axis=1)
            outs[j, hd] = jnp.dot(lhs, jnp.concatenate([v, state.astype(BF16)], axis=0),
                                  preferred_element_type=F32)
            s_ref[hd] = cdec_ref[hd] * state + kv
    issue_chunk()
    for j in range(n_blocks):
        for hd in range(RET_HEADS):
            c0 = 128 * hd
            o = outs[j, hd]
            gate = proj_ref[rows[j], OFF_RG + c0:OFF_RG + c0 + 128]
            mu = jnp.mean(o, axis=-1, keepdims=True)
            d = o - mu
            var = jnp.mean(d * d, axis=-1, keepdims=True)
            on = d * lax.rsqrt(var + EPS) * rg_ref[layer:layer + 1, c0:c0 + 128]
            y = gate * (1.0 / (1.0 + jnp.exp(-gate))) * on
            mixed_ref[rows[j], ATTN_WIDTH + c0:ATTN_WIDTH + c0 + 128] = y.astype(BF16)
            if hd % 2 == 1:
                issue_chunk()

    assert not chunks


def _mix_kernel(layer, tiles_per_seq,
                xn_ref, xr_ref, ada_n_ref, ada_r_ref, g1_ref, win_ref, gq_ref, gk_ref, sinks_ref,
                rg_ref, wout_ref, bias_ref, ret_ref, cdec_ref, w1_f32_ref, w2_f32_ref,
                o_ref, w1_bf16_ref, w2_bf16_ref,
                proj_a, proj_b, mixed_a, mixed_b, hn_ref, kwin_a, kwin_b, vwin_a, vwin_b, s_ref):
    i = pl.program_id(0)

    w1_bf16_ref[...] = w1_f32_ref[...].astype(BF16)
    w2_bf16_ref[...] = w2_f32_ref[...].astype(BF16)

    @pl.when(i == 0)
    def _():
        proj_b[...] = jnp.zeros_like(proj_b)
        mixed_a[...] = jnp.zeros_like(mixed_a)
        kwin_a[...] = jnp.zeros_like(kwin_a)
        kwin_b[...] = jnp.zeros_like(kwin_b)
        vwin_a[...] = jnp.zeros_like(vwin_a)
        vwin_b[...] = jnp.zeros_like(vwin_b)
        s_ref[...] = jnp.zeros_like(s_ref)

    consts = (layer, gq_ref, sinks_ref, rg_ref, bias_ref, ret_ref, cdec_ref, s_ref)

    def slot(half, proj_cur, proj_next, mixed_cur, mixed_prev, kwin, kwin_next, vwin, vwin_next,
             first_tab, first_keep):
        def out_chunk(c):
            cols = slice(c * PROJ_CHUNK, min((c + 1) * PROJ_CHUNK, D_MODEL))
            y = jnp.dot(mixed_prev[...], wout_ref[:, cols], preferred_element_type=F32)
            o_ref[half, :, cols] = xr_ref[half, :, cols] + ada_r_ref[2:3, cols] * y

        def norm_next():
            x = xn_ref[half]
            ms = jnp.mean(x * x, axis=-1, keepdims=True)
            h = x * lax.rsqrt(ms + EPS) * g1_ref[layer:layer + 1, :]
            hn_ref[...] = (h * (1.0 + ada_n_ref[1:2, :]) + ada_n_ref[0:1, :]).astype(BF16)

        def in_chunk(c):
            cols = slice(c * PROJ_CHUNK, min((c + 1) * PROJ_CHUNK, IN_WIDTH))
            proj_next[:, cols] = jnp.dot(hn_ref[...], win_ref[:, cols], preferred_element_type=F32)
            if cols.start <= OFF_AK and OFF_AV + 2 * HEAD_DIM <= cols.stop:
                _build_windows(proj_next, _pair_gain(gk_ref, layer), kwin_next, kwin, vwin_next, vwin)

        def first_chunk():
            out_chunk(0)
            norm_next()
            out_chunk(1)

        chunks = [first_chunk]
        chunks += [functools.partial(out_chunk, c) for c in range(2, pl.cdiv(D_MODEL, PROJ_CHUNK))]
        chunks += [functools.partial(in_chunk, c) for c in range(pl.cdiv(IN_WIDTH, PROJ_CHUNK))]
        _mix_slot(chunks, proj_cur, mixed_cur, kwin, vwin, first_tab, first_keep, *consts)

    slot(0, proj_b, proj_a, mixed_b, mixed_a, kwin_b, kwin_a, vwin_b, vwin_a, 1, 1.0)
    seq_start = (2 * i) % tiles_per_seq == 0
    slot(1, proj_a, proj_b, mixed_a, mixed_b, kwin_a, kwin_b, vwin_a, vwin_b,
         jnp.where(seq_start, 0, 1), jnp.where(seq_start, 0.0, 1.0))


def _mix_call(layer, x, ada, g1, w_in, gq, gk, sinks, rg, w_out, tables, w_mlp1, w_mlp2):
    b, s, d = x.shape
    tiles_per_seq = s // MIX_ROWS
    pairs_per_seq = tiles_per_seq // 2
    n_pairs = b * pairs_per_seq
    bias_tab, ret_tab, cdec = tables
    x_pairs = x.reshape(n_pairs, 2, MIX_ROWS, d)

    nxt = lambda i: jnp.minimum(i, n_pairs - 1)
    res = lambda i: jnp.maximum(i - 1, 0)
    const2 = lambda i: (0, 0)
    const4 = lambda i: (0, 0, 0, 0)
    single = pl.Buffered(1)
    smem = pl.BlockSpec(memory_space=pltpu.SMEM)
    slab1, slab2 = w_mlp1.shape[1] // n_pairs, w_mlp2.shape[1] // n_pairs
    out, w1_b, w2_b = pl.pallas_call(
        functools.partial(_mix_kernel, layer, tiles_per_seq),
        grid=(n_pairs + 1,),
        in_specs=[
            pl.BlockSpec((None, 2, MIX_ROWS, d), lambda i: (nxt(i), 0, 0, 0)),
            pl.BlockSpec((None, 2, MIX_ROWS, d), lambda i: (res(i), 0, 0, 0)),
            pl.BlockSpec((None, None, 6, d), lambda i: (layer, nxt(i) // pairs_per_seq, 0, 0)),
            pl.BlockSpec((None, None, 6, d), lambda i: (layer, res(i) // pairs_per_seq, 0, 0)),
            pl.BlockSpec(g1.shape, const2),
            pl.BlockSpec((d, IN_WIDTH), const2, pipeline_mode=single),
            pl.BlockSpec(gq.shape, const2),
            pl.BlockSpec(gk.shape, const2),
            smem,
            pl.BlockSpec(rg.shape, const2),
            pl.BlockSpec((d, d), const2, pipeline_mode=single),
            pl.BlockSpec(bias_tab.shape, const4, pipeline_mode=single),
            pl.BlockSpec(ret_tab.shape, const4, pipeline_mode=single),
            smem,
            pl.BlockSpec((None, slab1, D_FF), lambda i: (layer, nxt(i), 0)),
            pl.BlockSpec((None, slab2, d), lambda i: (layer, nxt(i), 0)),
        ],
        out_specs=[
            pl.BlockSpec((None, 2, MIX_ROWS, d), lambda i: (res(i), 0, 0, 0)),
            pl.BlockSpec((slab1, D_FF), lambda i: (nxt(i), 0)),
            pl.BlockSpec((slab2, d), lambda i: (nxt(i), 0)),
        ],
        out_shape=[
            jax.ShapeDtypeStruct((n_pairs, 2, MIX_ROWS, d), F32),
            jax.ShapeDtypeStruct(w_mlp1.shape[1:], BF16),
            jax.ShapeDtypeStruct(w_mlp2.shape[1:], BF16),
        ],
        scratch_shapes=[
            pltpu.VMEM((MIX_ROWS, IN_WIDTH), F32),
            pltpu.VMEM((MIX_ROWS, IN_WIDTH), F32),
            pltpu.VMEM((MIX_ROWS, d), BF16),
            pltpu.VMEM((MIX_ROWS, d), BF16),
            pltpu.VMEM((MIX_ROWS, d), BF16),
            pltpu.VMEM((N_KV_HEADS, BLOCK + MIX_ROWS, 2 * HEAD_DIM), BF16),
            pltpu.VMEM((N_KV_HEADS, BLOCK + MIX_ROWS, 2 * HEAD_DIM), BF16),
            pltpu.VMEM((2 * HEAD_DIM, BLOCK + MIX_ROWS), BF16),
            pltpu.VMEM((2 * HEAD_DIM, BLOCK + MIX_ROWS), BF16),
            pltpu.VMEM((RET_HEADS, 128, 128), F32),
        ],
        compiler_params=pltpu.CompilerParams(
            dimension_semantics=("arbitrary",), vmem_limit_bytes=VMEM_LIMIT),
        name="mix",
    )(x_pairs, x_pairs, ada, ada, g1, w_in, gq, gk, sinks, rg, w_out, bias_tab, ret_tab, cdec,
      w_mlp1, w_mlp2)
    return out.reshape(b, s, d), w1_b, w2_b


MLP_ROWS = 1024


def _mlp_kernel(layer, x_ref, ada_ref, g2_ref, w1_ref, w2_ref, *rest):
    if len(rest) == 1:
        (o_ref,) = rest
    else:
        win_f32_ref, wout_f32_ref, o_ref, win_bf16_ref, wout_bf16_ref = rest
        win_bf16_ref[...] = win_f32_ref[...].astype(BF16)
        wout_bf16_ref[...] = wout_f32_ref[...].astype(BF16)
    x = x_ref[...]
    ms = jnp.mean(x * x, axis=-1, keepdims=True)
    h = x * lax.rsqrt(ms + EPS) * g2_ref[layer:layer + 1, :]
    h = (h * (1.0 + ada_ref[4:5, :]) + ada_ref[3:4, :]).astype(BF16)
    acc = None
    for c in range(D_FF // FF_CHUNK):
        cols = slice(c * FF_CHUNK, (c + 1) * FF_CHUNK)
        a = jnp.dot(h, w1_ref[:, cols], preferred_element_type=F32)
        a = jnp.maximum(a, 0.0)
        part = jnp.dot((a * a).astype(BF16), w2_ref[cols, :], preferred_element_type=F32)
        acc = part if acc is None else acc + part
    o_ref[...] = x + ada_ref[5:6, :] * acc


def _mlp_call(layer, x, ada, g2, w1, w2, w_in=None, w_out=None):
    b, s, d = x.shape
    steps_per_seq = s // MLP_ROWS
    n_steps = b * steps_per_seq
    const2 = lambda i, j: (0, 0)
    single = pl.Buffered(1)
    in_specs = [
        pl.BlockSpec((None, MLP_ROWS, d), lambda i, j: (i, j, 0)),
        pl.BlockSpec((None, None, 6, d), lambda i, j: (layer, i, 0, 0)),
        pl.BlockSpec(g2.shape, const2),
        pl.BlockSpec((d, D_FF), const2, pipeline_mode=single),
        pl.BlockSpec((D_FF, d), const2, pipeline_mode=single),
    ]
    out_specs = [pl.BlockSpec((None, MLP_ROWS, d), lambda i, j: (i, j, 0))]
    out_shape = [jax.ShapeDtypeStruct((b, s, d), F32)]
    operands = [x, ada, g2, w1, w2]
    if w_in is not None:
        for w in (w_in, w_out):
            slab = w.shape[1] // n_steps
            step = lambda i, j: i * steps_per_seq + j
            in_specs.append(pl.BlockSpec((None, slab, w.shape[2]), lambda i, j: (layer + 1, step(i, j), 0)))
            out_specs.append(pl.BlockSpec((slab, w.shape[2]), lambda i, j: (step(i, j), 0)))
            out_shape.append(jax.ShapeDtypeStruct(w.shape[1:], BF16))
            operands.append(w)
    return pl.pallas_call(
        functools.partial(_mlp_kernel, layer),
        grid=(b, steps_per_seq),
        in_specs=in_specs,
        out_specs=out_specs,
        out_shape=out_shape,
        compiler_params=pltpu.CompilerParams(
            dimension_semantics=("arbitrary", "arbitrary"), vmem_limit_bytes=VMEM_LIMIT),
        name="mlp",
    )(*operands)


def kernel(x, c, norm1_g, norm2_g, w_ada, b_ada, w_in, q_norm_g, k_norm_g, sinks, ret_norm_g,
           w_out, w_mlp1, w_mlp2):
    b = x.shape[0]
    tables = tuple(jnp.asarray(t) for t in _constant_tables())
    c_pad = jnp.pad(c, ((0, 8 - b), (0, 0)))
    ada = _ada_call(c_pad, w_ada, b_ada)[:, :b].reshape(DEPTH, b, 6, D_MODEL)
    w_in_b, w_out_b = w_in[0].astype(BF16), w_out[0].astype(BF16)
    for l in range(DEPTH):
        x, w1_b, w2_b = _mix_call(l, x, ada, norm1_g, w_in_b, q_norm_g, k_norm_g, sinks, ret_norm_g,
                                  w_out_b, tables, w_mlp1, w_mlp2)
        if l + 1 < DEPTH:
            x, w_in_b, w_out_b = _mlp_call(l, x, ada, norm2_g, w1_b, w2_b, w_in, w_out)
        else:
            (x,) = _mlp_call(l, x, ada, norm2_g, w1_b, w2_b)
    return x
```

```python
import functools

import numpy as np
import jax
import jax.numpy as jnp
from jax import lax
from jax.experimental import pallas as pl
from jax.experimental.pallas import tpu as pltpu

D_MODEL = 1024
DEPTH = 2
ATTN_WIDTH = 512
RET_WIDTH = 512
HEAD_DIM = 64
N_Q_HEADS = 8
N_KV_HEADS = 2
N_PAIRS = N_Q_HEADS // 2
BLOCK = 128
SUM_ROWS = 16
RET_HEADS = 4
RET_DK = 128
D_FF = 4 * D_MODEL
EPS = 1e-6
NEG_INF = -1e30
LOG2E = 1.4426950408889634
IN_WIDTH = 2816
OFF_AQ, OFF_AK, OFF_AV, OFF_RQ, OFF_RK, OFF_RV, OFF_RG = 0, 512, 640, 768, 1280, 1792, 2304

MIX_ROWS = 256
PROJ_CHUNK = 256
FF_CHUNK = 1024
ADA_COLS = 1536
VMEM_LIMIT = 60 * 1024 * 1024

F32 = jnp.float32
BF16 = jnp.bfloat16


@functools.lru_cache(maxsize=None)
def _constant_tables():
    q_pos = np.arange(BLOCK)[:, None]
    k_pos = np.arange(2 * BLOCK)[None, :]
    dist = q_pos + BLOCK - k_pos
    valid = (dist >= 0) & (dist < BLOCK)
    valid_first = valid & (k_pos >= BLOCK)
    slopes = np.exp2(-8.0 * np.arange(1, N_Q_HEADS + 1, dtype=np.float64) / N_Q_HEADS)
    bias = -slopes[:, None, None] * dist[None].astype(np.float64)
    bias_tab = np.stack([np.where(v[None], bias * LOG2E, NEG_INF) for v in (valid_first, valid)])
    bias_tab = np.ascontiguousarray(bias_tab.transpose(0, 1, 3, 2), np.float32)

    log_gamma = np.log1p(-np.exp2(-5.0 - np.arange(RET_HEADS, dtype=np.float64)))
    idx = np.arange(BLOCK, dtype=np.float64)
    rel = idx[:, None] - idx[None, :]
    k_scale = RET_DK ** -0.5
    decay_in = np.where(rel >= 0, np.exp(log_gamma[:, None, None] * np.maximum(rel, 0.0)), 0.0) * k_scale
    q_decay = np.exp(log_gamma[:, None] * (idx[None, :] + 1.0))
    k_decay = np.exp(log_gamma[:, None] * (BLOCK - 1.0 - idx[None, :])) * k_scale
    q_decay = np.broadcast_to(q_decay[:, :, None], (RET_HEADS, BLOCK, BLOCK))
    k_decay = np.broadcast_to(k_decay[:, :, None], (RET_HEADS, BLOCK, BLOCK))
    chunk_decay = np.exp(log_gamma * BLOCK)
    ret_tab = np.stack([decay_in, q_decay, k_decay]).astype(np.float32)
    return bias_tab, ret_tab, chunk_decay.astype(np.float32)


def _ada_kernel(c_ref, w_ref, b_ref, o_ref):
    c = c_ref[...]
    c_act = c * (1.0 / (1.0 + jnp.exp(-c)))
    o_ref[...] = jnp.dot(c_act.astype(BF16), w_ref[...].astype(BF16),
                         preferred_element_type=F32) + b_ref[...]


def _ada_call(c_pad, w_ada, b_ada):
    rows = c_pad.shape[0]
    n = w_ada.shape[-1]
    return pl.pallas_call(
        _ada_kernel,
        grid=(DEPTH, n // ADA_COLS),
        in_specs=[
            pl.BlockSpec((rows, D_MODEL), lambda l, j: (0, 0)),
            pl.BlockSpec((None, D_MODEL, ADA_COLS), lambda l, j: (l, 0, j)),
            pl.BlockSpec((None, 1, ADA_COLS), lambda l, j: (l, 0, j)),
        ],
        out_specs=pl.BlockSpec((None, rows, ADA_COLS), lambda l, j: (l, 0, j)),
        out_shape=jax.ShapeDtypeStruct((DEPTH, rows, n), F32),
        compiler_params=pltpu.CompilerParams(
            dimension_semantics=("arbitrary", "arbitrary"), vmem_limit_bytes=VMEM_LIMIT),
        name="ada",
    )(c_pad, w_ada, b_ada.reshape(DEPTH, 1, n))


def _head_norm(a, gain_row):
    lo = lax.broadcasted_iota(jnp.int32, a.shape, 1) < HEAD_DIM
    a2 = a * a
    s_lo = jnp.sum(jnp.where(lo, a2, 0.0), axis=-1, keepdims=True)
    s_hi = jnp.sum(jnp.where(lo, 0.0, a2), axis=-1, keepdims=True)
    mean_sq = jnp.where(lo, s_lo, s_hi) * (1.0 / HEAD_DIM)
    return a * lax.rsqrt(mean_sq + EPS) * gain_row


def _pair_gain(g_ref, layer):
    row = g_ref[layer:layer + 1, :]
    return jnp.concatenate([row, row], axis=1)


def _build_windows(proj_ref, gk_row, kwin_ref, kwin_prev, vwin_ref, vwin_prev):
    kn = _head_norm(proj_ref[:, OFF_AK:OFF_AK + 128], gk_row)
    kn_rot = pltpu.roll(kn, HEAD_DIM, axis=1)
    lo = lax.broadcasted_iota(jnp.int32, kn.shape, 1) < HEAD_DIM
    k_dup = (jnp.where(lo, kn, kn_rot), jnp.where(lo, kn_rot, kn))
    for g in range(N_KV_HEADS):
        kwin_ref[g, 0:BLOCK, :] = kwin_prev[g, MIX_ROWS:MIX_ROWS + BLOCK, :]
        kwin_ref[g, BLOCK:BLOCK + MIX_ROWS, :] = k_dup[g].astype(BF16)
    vwin_ref[:, 0:BLOCK] = vwin_prev[:, MIX_ROWS:MIX_ROWS + BLOCK]
    vwin_ref[:, BLOCK:BLOCK + MIX_ROWS] = proj_ref[:, OFF_AV:OFF_AV + 128].T.astype(BF16)


def _mix_slot(chunks, proj_ref, mixed_ref, kwin_ref, vwin_ref,
              first_tab, first_keep, layer, gq_ref, sinks_ref, rg_ref,
              bias_ref, ret_ref, cdec_ref, s_ref):
    chunks = list(chunks)
    n_chunks = len(chunks)
    n_points = 5 + 5 * (MIX_ROWS // BLOCK)
    point = [0]

    def issue_chunk():
        point[0] += 1
        while chunks and (n_chunks - len(chunks)) * n_points < point[0] * n_chunks:
            chunks.pop(0)()

    issue_chunk()

    lo = lax.broadcasted_iota(jnp.int32, (BLOCK, 2 * HEAD_DIM), 1) < HEAD_DIM
    gq = _pair_gain(gq_ref, layer) * (HEAD_DIM ** -0.5 * LOG2E)
    n_blocks = MIX_ROWS // BLOCK
    rows = [pl.ds(j * BLOCK, BLOCK) for j in range(n_blocks)]
    wins = [slice(j * BLOCK, (j + 2) * BLOCK) for j in range(n_blocks)]
    tabs = [first_tab if j == 0 else 1 for j in range(n_blocks)]

    s_t = {}
    for j in range(n_blocks):
        for g in range(N_KV_HEADS):
            qs = []
            for pp in range(2):
                p = 2 * g + pp
                qn = _head_norm(proj_ref[rows[j], OFF_AQ + 128 * p:OFF_AQ + 128 * p + 128], gq)
                qs.append(jnp.where(lo, qn, 0.0).astype(BF16))
                qs.append(jnp.where(lo, 0.0, qn).astype(BF16))
            s_t[j, g] = lax.dot_general(kwin_ref[g, wins[j], :], jnp.concatenate(qs, axis=0),
                                        (((1,), (1,)), ((), ())),
                                        preferred_element_type=F32)
    issue_chunk()
    sinks = [sinks_ref[layer, head] * LOG2E for head in range(N_Q_HEADS)]
    es, ms = {}, {}
    for j in range(n_blocks):
        for head in range(N_Q_HEADS):
            g, hh = divmod(head, 4)
            sh = s_t[j, g][:, 128 * hh:128 * hh + 128] + bias_ref[tabs[j], head]
            ms[j, head] = jnp.maximum(jnp.max(sh, axis=0, keepdims=True), sinks[head])
            es[j, head] = jnp.exp2(sh - ms[j, head]).astype(BF16)
            if head % 4 == 3:
                issue_chunk()
    ones = jnp.ones((SUM_ROWS, 2 * BLOCK), BF16)
    o_t = {}
    for j in range(n_blocks):
        for g in range(N_KV_HEADS):
            lhs = jnp.concatenate([vwin_ref[64 * g:64 * g + 64, wins[j]], ones], axis=0)
            p_t = jnp.concatenate([es[j, 4 * g + hh] for hh in range(4)], axis=1)
            o_t[j, g] = jnp.dot(lhs, p_t, preferred_element_type=F32)
    issue_chunk()
    for j in range(n_blocks):
        for p in range(N_PAIRS):
            g, pp = divmod(p, 2)
            halves = []
            for head in (2 * p, 2 * p + 1):
                cols = slice(128 * (head % 4), 128 * (head % 4) + 128)
                denom = o_t[j, g][HEAD_DIM:HEAD_DIM + 1, cols] + jnp.exp2(sinks[head] - ms[j, head])
                halves.append(o_t[j, g][:HEAD_DIM, cols] / denom)
            o = jnp.concatenate(halves, axis=0).T
            mixed_ref[rows[j], 128 * p:128 * p + 128] = o.astype(BF16)
        issue_chunk()

    heads = {}
    for j in range(n_blocks):
        for hd in range(RET_HEADS):
            c0 = 128 * hd
            q_f = proj_ref[rows[j], OFF_RQ + c0:OFF_RQ + c0 + 128]
            k = proj_ref[rows[j], OFF_RK + c0:OFF_RK + c0 + 128]
            v = proj_ref[rows[j], OFF_RV + c0:OFF_RV + c0 + 128].astype(BF16)
            inner = lax.dot_general(q_f.astype(BF16), k.astype(BF16), (((1,), (1,)), ((), ())),
                                    preferred_element_type=F32)
            kd = (k * ret_ref[2, hd]).astype(BF16)
            kv = lax.dot_general(kd, v, (((0,), (0,)), ((), ())), preferred_element_type=F32)
            heads[j, hd] = (q_f, v, inner, kv)
    issue_chunk()
    outs = {}
    for j in range(n_blocks):
        for hd in range(RET_HEADS):
            q_f, v, inner, kv = heads[j, hd]
            state = s_ref[hd] * first_keep if j == 0 else s_ref[hd]
            lhs = jnp.concatenate([(inner * ret_ref[0, hd]).astype(BF16), (q_f * ret_ref[1, hd]).astype(BF16)],
                                  axis=1)
            outs[j, hd] = jnp.dot(lhs, jnp.concatenate([v, state.astype(BF16)], axis=0),
                                  preferred_element_type=F32)
            s_ref[hd] = cdec_ref[hd] * state + kv
    issue_chunk()
    for j in range(n_blocks):
        for hd in range(RET_HEADS):
            c0 = 128 * hd
            o = outs[j, hd]
            gate = proj_ref[rows[j], OFF_RG + c0:OFF_RG + c0 + 128]
            mu = jnp.mean(o, axis=-1, keepdims=True)
            d = o - mu
            var = jnp.mean(d * d, axis=-1, keepdims=True)
            on = d * lax.rsqrt(var + EPS) * rg_ref[layer:layer + 1, c0:c0 + 128]
            y = gate * (1.0 / (1.0 + jnp.exp(-gate))) * on
            mixed_ref[rows[j], ATTN_WIDTH + c0:ATTN_WIDTH + c0 + 128] = y.astype(BF16)
            if hd % 2 == 1:
                issue_chunk()

    assert not chunks


def _mix_kernel(layer, tiles_per_seq,
                xn_ref, xr_ref, ada_n_ref, ada_r_ref, g1_ref, win_ref, gq_ref, gk_ref, sinks_ref,
                rg_ref, wout_ref, bias_ref, ret_ref, cdec_ref, w1_f32_ref, w2_f32_ref,
                o_ref, w1_bf16_ref, w2_bf16_ref,
                proj_a, proj_b, mixed_a, mixed_b, hn_ref, kwin_a, kwin_b, vwin_a, vwin_b, s_ref):
    i = pl.program_id(0)

    w1_bf16_ref[...] = w1_f32_ref[...].astype(BF16)
    w2_bf16_ref[...] = w2_f32_ref[...].astype(BF16)

    @pl.when(i == 0)
    def _():
        proj_b[...] = jnp.zeros_like(proj_b)
        mixed_a[...] = jnp.zeros_like(mixed_a)
        kwin_a[...] = jnp.zeros_like(kwin_a)
        kwin_b[...] = jnp.zeros_like(kwin_b)
        vwin_a[...] = jnp.zeros_like(vwin_a)
        vwin_b[...] = jnp.zeros_like(vwin_b)
        s_ref[...] = jnp.zeros_like(s_ref)

    consts = (layer, gq_ref, sinks_ref, rg_ref, bias_ref, ret_ref, cdec_ref, s_ref)

    def slot(half, proj_cur, proj_next, mixed_cur, mixed_prev, kwin, kwin_next, vwin, vwin_next,
             first_tab, first_keep):
        def out_chunk(c):
            cols = slice(c * PROJ_CHUNK, min((c + 1) * PROJ_CHUNK, D_MODEL))
            y = jnp.dot(mixed_prev[...], wout_ref[:, cols], preferred_element_type=F32)
            o_ref[half, :, cols] = xr_ref[half, :, cols] + ada_r_ref[2:3, cols] * y

        def norm_next():
            x = xn_ref[half]
            ms = jnp.mean(x * x, axis=-1, keepdims=True)
            h = x * lax.rsqrt(ms + EPS) * g1_ref[layer:layer + 1, :]
            hn_ref[...] = (h * (1.0 + ada_n_ref[1:2, :]) + ada_n_ref[0:1, :]).astype(BF16)

        def in_chunk(c):
            cols = slice(c * PROJ_CHUNK, min((c + 1) * PROJ_CHUNK, IN_WIDTH))
            proj_next[:, cols] = jnp.dot(hn_ref[...], win_ref[:, cols], preferred_element_type=F32)
            if cols.start <= OFF_AK and OFF_AV + 2 * HEAD_DIM <= cols.stop:
                _build_windows(proj_next, _pair_gain(gk_ref, layer), kwin_next, kwin, vwin_next, vwin)

        def first_chunk():
            out_chunk(0)
            norm_next()
            out_chunk(1)

        chunks = [first_chunk]
        chunks += [functools.partial(out_chunk, c) for c in range(2, pl.cdiv(D_MODEL, PROJ_CHUNK))]
        chunks += [functools.partial(in_chunk, c) for c in range(pl.cdiv(IN_WIDTH, PROJ_CHUNK))]
        _mix_slot(chunks, proj_cur, mixed_cur, kwin, vwin, first_tab, first_keep, *consts)

    slot(0, proj_b, proj_a, mixed_b, mixed_a, kwin_b, kwin_a, vwin_b, vwin_a, 1, 1.0)
    seq_start = (2 * i) % tiles_per_seq == 0
    slot(1, proj_a, proj_b, mixed_a, mixed_b, kwin_a, kwin_b, vwin_a, vwin_b,
         jnp.where(seq_start, 0, 1), jnp.where(seq_start, 0.0, 1.0))


def _mix_call(layer, x, ada, g1, w_in, gq, gk, sinks, rg, w_out, tables, w_mlp1, w_mlp2):
    b, s, d = x.shape
    tiles_per_seq = s // MIX_ROWS
    pairs_per_seq = tiles_per_seq // 2
    n_pairs = b * pairs_per_seq
    bias_tab, ret_tab, cdec = tables
    x_pairs = x.reshape(n_pairs, 2, MIX_ROWS, d)

    nxt = lambda i: jnp.minimum(i, n_pairs - 1)
    res = lambda i: jnp.maximum(i - 1, 0)
    const2 = lambda i: (0, 0)
    const4 = lambda i: (0, 0, 0, 0)
    single = pl.Buffered(1)
    smem = pl.BlockSpec(memory_space=pltpu.SMEM)
    slab1, slab2 = w_mlp1.shape[1] // n_pairs, w_mlp2.shape[1] // n_pairs
    out, w1_b, w2_b = pl.pallas_call(
        functools.partial(_mix_kernel, layer, tiles_per_seq),
        grid=(n_pairs + 1,),
        in_specs=[
            pl.BlockSpec((None, 2, MIX_ROWS, d), lambda i: (nxt(i), 0, 0, 0)),
            pl.BlockSpec((None, 2, MIX_ROWS, d), lambda i: (res(i), 0, 0, 0)),
            pl.BlockSpec((None, None, 6, d), lambda i: (layer, nxt(i) // pairs_per_seq, 0, 0)),
            pl.BlockSpec((None, None, 6, d), lambda i: (layer, res(i) // pairs_per_seq, 0, 0)),
            pl.BlockSpec(g1.shape, const2),
            pl.BlockSpec((d, IN_WIDTH), const2, pipeline_mode=single),
            pl.BlockSpec(gq.shape, const2),
            pl.BlockSpec(gk.shape, const2),
            smem,
            pl.BlockSpec(rg.shape, const2),
            pl.BlockSpec((d, d), const2, pipeline_mode=single),
            pl.BlockSpec(bias_tab.shape, const4, pipeline_mode=single),
            pl.BlockSpec(ret_tab.shape, const4, pipeline_mode=single),
            smem,
            pl.BlockSpec((None, slab1, D_FF), lambda i: (layer, nxt(i), 0)),
            pl.BlockSpec((None, slab2, d), lambda i: (layer, nxt(i), 0)),
        ],
        out_specs=[
            pl.BlockSpec((None, 2, MIX_ROWS, d), lambda i: (res(i), 0, 0, 0)),
            pl.BlockSpec((slab1, D_FF), lambda i: (nxt(i), 0)),
            pl.BlockSpec((slab2, d), lambda i: (nxt(i), 0)),
        ],
        out_shape=[
            jax.ShapeDtypeStruct((n_pairs, 2, MIX_ROWS, d), F32),
            jax.ShapeDtypeStruct(w_mlp1.shape[1:], BF16),
            jax.ShapeDtypeStruct(w_mlp2.shape[1:], BF16),
        ],
        scratch_shapes=[
            pltpu.VMEM((MIX_ROWS, IN_WIDTH), F32),
            pltpu.VMEM((MIX_ROWS, IN_WIDTH), F32),
            pltpu.VMEM((MIX_ROWS, d), BF16),
            pltpu.VMEM((MIX_ROWS, d), BF16),
            pltpu.VMEM((MIX_ROWS, d), BF16),
            pltpu.VMEM((N_KV_HEADS, BLOCK + MIX_ROWS, 2 * HEAD_DIM), BF16),
            pltpu.VMEM((N_KV_HEADS, BLOCK + MIX_ROWS, 2 * HEAD_DIM), BF16),
            pltpu.VMEM((2 * HEAD_DIM, BLOCK + MIX_ROWS), BF16),
            pltpu.VMEM((2 * HEAD_DIM, BLOCK + MIX_ROWS), BF16),
            pltpu.VMEM((RET_HEADS, 128, 128), F32),
        ],
        compiler_params=pltpu.CompilerParams(
            dimension_semantics=("arbitrary",), vmem_limit_bytes=VMEM_LIMIT),
        name="mix",
    )(x_pairs, x_pairs, ada, ada, g1, w_in, gq, gk, sinks, rg, w_out, bias_tab, ret_tab, cdec,
      w_mlp1, w_mlp2)
    return out.reshape(b, s, d), w1_b, w2_b


MLP_ROWS = 1024


def _mlp_kernel(layer, x_ref, ada_ref, g2_ref, w1_ref, w2_ref, *rest):
    if len(rest) == 1:
        (o_ref,) = rest
    else:
        win_f32_ref, wout_f32_ref, o_ref, win_bf16_ref, wout_bf16_ref = rest
        win_bf16_ref[...] = win_f32_ref[...].astype(BF16)
        wout_bf16_ref[...] = wout_f32_ref[...].astype(BF16)
    x = x_ref[...]
    ms = jnp.mean(x * x, axis=-1, keepdims=True)
    h = x * lax.rsqrt(ms + EPS) * g2_ref[layer:layer + 1, :]
    h = (h * (1.0 + ada_ref[4:5, :]) + ada_ref[3:4, :]).astype(BF16)
    acc = None
    for c in range(D_FF // FF_CHUNK):
        cols = slice(c * FF_CHUNK, (c + 1) * FF_CHUNK)
        a = jnp.dot(h, w1_ref[:, cols], preferred_element_type=F32)
        a = jnp.maximum(a, 0.0)
        part = jnp.dot((a * a).astype(BF16), w2_ref[cols, :], preferred_element_type=F32)
        acc = part if acc is None else acc + part
    o_ref[...] = x + ada_ref[5:6, :] * acc


def _mlp_call(layer, x, ada, g2, w1, w2, w_in=None, w_out=None):
    b, s, d = x.shape
    steps_per_seq = s // MLP_ROWS
    n_steps = b * steps_per_seq
    const2 = lambda i, j: (0, 0)
    single = pl.Buffered(1)
    in_specs = [
        pl.BlockSpec((None, MLP_ROWS, d), lambda i, j: (i, j, 0)),
        pl.BlockSpec((None, None, 6, d), lambda i, j: (layer, i, 0, 0)),
        pl.BlockSpec(g2.shape, const2),
        pl.BlockSpec((d, D_FF), const2, pipeline_mode=single),
        pl.BlockSpec((D_FF, d), const2, pipeline_mode=single),
    ]
    out_specs = [pl.BlockSpec((None, MLP_ROWS, d), lambda i, j: (i, j, 0))]
    out_shape = [jax.ShapeDtypeStruct((b, s, d), F32)]
    operands = [x, ada, g2, w1, w2]
    if w_in is not None:
        for w in (w_in, w_out):
            slab = w.shape[1] // n_steps
            step = lambda i, j: i * steps_per_seq + j
            in_specs.append(pl.BlockSpec((None, slab, w.shape[2]), lambda i, j: (layer + 1, step(i, j), 0)))
            out_specs.append(pl.BlockSpec((slab, w.shape[2]), lambda i, j: (step(i, j), 0)))
            out_shape.append(jax.ShapeDtypeStruct(w.shape[1:], BF16))
            operands.append(w)
    return pl.pallas_call(
        functools.partial(_mlp_kernel, layer),
        grid=(b, steps_per_seq),
        in_specs=in_specs,
        out_specs=out_specs,
        out_shape=out_shape,
        compiler_params=pltpu.CompilerParams(
            dimension_semantics=("arbitrary", "arbitrary"), vmem_limit_bytes=VMEM_LIMIT),
        name="mlp",
    )(*operands)


def kernel(x, c, norm1_g, norm2_g, w_ada, b_ada, w_in, q_norm_g, k_norm_g, sinks, ret_norm_g,
           w_out, w_mlp1, w_mlp2):
    b = x.shape[0]
    tables = tuple(jnp.asarray(t) for t in _constant_tables())
    c_pad = jnp.pad(c, ((0, 8 - b), (0, 0)))
    ada = _ada_call(c_pad, w_ada, b_ada)[:, :b].reshape(DEPTH, b, 6, D_MODEL)
    w_in_b, w_out_b = w_in[0].astype(BF16), w_out[0].astype(BF16)
    for l in range(DEPTH):
        x, w1_b, w2_b = _mix_call(l, x, ada, norm1_g, w_in_b, q_norm_g, k_norm_g, sinks, ret_norm_g,
                                  w_out_b, tables, w_mlp1, w_mlp2)
        if l + 1 < DEPTH:
            x, w_in_b, w_out_b = _mlp_call(l, x, ada, norm2_g, w1_b, w2_b, w_in, w_out)
        else:
            (x,) = _mlp_call(l, x, ada, norm2_g, w1_b, w2_b)
    return x
```

```python
import functools

import numpy as np
import jax
import jax.numpy as jnp
from jax import lax
from jax.experimental import pallas as pl
from jax.experimental.pallas import tpu as pltpu

D_MODEL = 1024
DEPTH = 2
ATTN_WIDTH = 512
RET_WIDTH = 512
HEAD_DIM = 64
N_Q_HEADS = 8
N_KV_HEADS = 2
N_PAIRS = N_Q_HEADS // 2
BLOCK = 128
RET_HEADS = 4
RET_DK = 128
D_FF = 4 * D_MODEL
EPS = 1e-6
NEG_INF = -1e30
LOG2E = 1.4426950408889634
IN_WIDTH = 2816
OFF_AQ, OFF_AK, OFF_AV, OFF_RQ, OFF_RK, OFF_RV, OFF_RG = 0, 512, 640, 768, 1280, 1792, 2304

MIX_ROWS = 256
PROJ_CHUNK = 256
FF_CHUNK = 1024
ADA_COLS = 1536
VMEM_LIMIT = 60 * 1024 * 1024

F32 = jnp.float32
BF16 = jnp.bfloat16


@functools.lru_cache(maxsize=None)
def _constant_tables():
    q_pos = np.arange(BLOCK)[:, None]
    k_pos = np.arange(2 * BLOCK)[None, :]
    dist = q_pos + BLOCK - k_pos
    valid = (dist >= 0) & (dist < BLOCK)
    valid_first = valid & (k_pos >= BLOCK)
    slopes = np.exp2(-8.0 * np.arange(1, N_Q_HEADS + 1, dtype=np.float64) / N_Q_HEADS)
    bias = -slopes[:, None, None] * dist[None].astype(np.float64)
    bias_tab = np.stack([np.where(v[None], bias * LOG2E, NEG_INF) for v in (valid_first, valid)])
    bias_tab = np.ascontiguousarray(bias_tab.transpose(0, 1, 3, 2), np.float32)

    log_gamma = np.log1p(-np.exp2(-5.0 - np.arange(RET_HEADS, dtype=np.float64)))
    idx = np.arange(BLOCK, dtype=np.float64)
    rel = idx[:, None] - idx[None, :]
    k_scale = RET_DK ** -0.5
    decay_in = np.where(rel >= 0, np.exp(log_gamma[:, None, None] * np.maximum(rel, 0.0)), 0.0) * k_scale
    q_decay = np.exp(log_gamma[:, None] * (idx[None, :] + 1.0))
    k_decay = np.exp(log_gamma[:, None] * (BLOCK - 1.0 - idx[None, :])) * k_scale
    q_decay = np.broadcast_to(q_decay[:, :, None], (RET_HEADS, BLOCK, BLOCK))
    k_decay = np.broadcast_to(k_decay[:, :, None], (RET_HEADS, BLOCK, BLOCK))
    chunk_decay = np.exp(log_gamma * BLOCK)
    ret_tab = np.stack([decay_in, q_decay, k_decay]).astype(np.float32)
    return bias_tab, ret_tab, chunk_decay.astype(np.float32)


def _ada_kernel(c_ref, w_ref, b_ref, o_ref):
    c = c_ref[...]
    c_act = c * (1.0 / (1.0 + jnp.exp(-c)))
    o_ref[...] = jnp.dot(c_act.astype(BF16), w_ref[...].astype(BF16),
                         preferred_element_type=F32) + b_ref[...]


def _ada_call(c_pad, w_ada, b_ada):
    rows = c_pad.shape[0]
    n = w_ada.shape[-1]
    return pl.pallas_call(
        _ada_kernel,
        grid=(DEPTH, n // ADA_COLS),
        in_specs=[
            pl.BlockSpec((rows, D_MODEL), lambda l, j: (0, 0)),
            pl.BlockSpec((None, D_MODEL, ADA_COLS), lambda l, j: (l, 0, j)),
            pl.BlockSpec((None, 1, ADA_COLS), lambda l, j: (l, 0, j)),
        ],
        out_specs=pl.BlockSpec((None, rows, ADA_COLS), lambda l, j: (l, 0, j)),
        out_shape=jax.ShapeDtypeStruct((DEPTH, rows, n), F32),
        compiler_params=pltpu.CompilerParams(
            dimension_semantics=("arbitrary", "arbitrary"), vmem_limit_bytes=VMEM_LIMIT),
        name="ada",
    )(c_pad, w_ada, b_ada.reshape(DEPTH, 1, n))


def _head_norm(a, gain_row):
    lo = lax.broadcasted_iota(jnp.int32, a.shape, 1) < HEAD_DIM
    a2 = a * a
    s_lo = jnp.sum(jnp.where(lo, a2, 0.0), axis=-1, keepdims=True)
    s_hi = jnp.sum(jnp.where(lo, 0.0, a2), axis=-1, keepdims=True)
    mean_sq = jnp.where(lo, s_lo, s_hi) * (1.0 / HEAD_DIM)
    return a * lax.rsqrt(mean_sq + EPS) * gain_row


def _pair_gain(g_ref, layer):
    row = g_ref[layer:layer + 1, :]
    return jnp.concatenate([row, row], axis=1)


def _build_windows(proj_ref, gk_row, kwin_ref, kwin_prev, vwin_ref, vwin_prev):
    kn = _head_norm(proj_ref[:, OFF_AK:OFF_AK + 128], gk_row)
    kn_rot = pltpu.roll(kn, HEAD_DIM, axis=1)
    lo = lax.broadcasted_iota(jnp.int32, kn.shape, 1) < HEAD_DIM
    k_dup = (jnp.where(lo, kn, kn_rot), jnp.where(lo, kn_rot, kn))
    for g in range(N_KV_HEADS):
        kwin_ref[g, 0:BLOCK, :] = kwin_prev[g, MIX_ROWS:MIX_ROWS + BLOCK, :]
        kwin_ref[g, BLOCK:BLOCK + MIX_ROWS, :] = k_dup[g].astype(BF16)
    vwin_ref[:, 0:BLOCK] = vwin_prev[:, MIX_ROWS:MIX_ROWS + BLOCK]
    vwin_ref[:, BLOCK:BLOCK + MIX_ROWS] = proj_ref[:, OFF_AV:OFF_AV + 128].T.astype(BF16)


def _mix_slot(chunks, proj_ref, mixed_ref, kwin_ref, vwin_ref,
              first_tab, first_keep, layer, gq_ref, sinks_ref, rg_ref,
              bias_ref, ret_ref, cdec_ref, s_ref):
    chunks = list(chunks)
    n_chunks = len(chunks)
    n_points = 5 + 5 * (MIX_ROWS // BLOCK)
    point = [0]

    def issue_chunk():
        point[0] += 1
        while chunks and (n_chunks - len(chunks)) * n_points < point[0] * n_chunks:
            chunks.pop(0)()

    issue_chunk()

    lo = lax.broadcasted_iota(jnp.int32, (BLOCK, 2 * HEAD_DIM), 1) < HEAD_DIM
    gq = _pair_gain(gq_ref, layer) * (HEAD_DIM ** -0.5 * LOG2E)
    n_blocks = MIX_ROWS // BLOCK
    rows = [pl.ds(j * BLOCK, BLOCK) for j in range(n_blocks)]
    wins = [slice(j * BLOCK, (j + 2) * BLOCK) for j in range(n_blocks)]
    tabs = [first_tab if j == 0 else 1 for j in range(n_blocks)]

    s_t = {}
    for j in range(n_blocks):
        for g in range(N_KV_HEADS):
            qs = []
            for pp in range(2):
                p = 2 * g + pp
                qn = _head_norm(proj_ref[rows[j], OFF_AQ + 128 * p:OFF_AQ + 128 * p + 128], gq)
                qs.append(jnp.where(lo, qn, 0.0).astype(BF16))
                qs.append(jnp.where(lo, 0.0, qn).astype(BF16))
            s_t[j, g] = lax.dot_general(kwin_ref[g, wins[j], :], jnp.concatenate(qs, axis=0),
                                        (((1,), (1,)), ((), ())),
                                        preferred_element_type=F32)
    issue_chunk()
    sinks = [sinks_ref[layer, head] * LOG2E for head in range(N_Q_HEADS)]
    es, ms = {}, {}
    for j in range(n_blocks):
        for head in range(N_Q_HEADS):
            g, hh = divmod(head, 4)
            sh = s_t[j, g][:, 128 * hh:128 * hh + 128] + bias_ref[tabs[j], head]
            ms[j, head] = jnp.maximum(jnp.max(sh, axis=0, keepdims=True), sinks[head])
            es[j, head] = jnp.exp2(sh - ms[j, head]).astype(BF16)
            if head % 4 == 3:
                issue_chunk()
    ones = jnp.ones((HEAD_DIM, 2 * BLOCK), BF16)
    o_t = {}
    for j in range(n_blocks):
        for g in range(N_KV_HEADS):
            lhs = jnp.concatenate([vwin_ref[64 * g:64 * g + 64, wins[j]], ones], axis=0)
            p_t = jnp.concatenate([es[j, 4 * g + hh] for hh in range(4)], axis=1)
            o_t[j, g] = jnp.dot(lhs, p_t, preferred_element_type=F32)
    issue_chunk()
    for j in range(n_blocks):
        for p in range(N_PAIRS):
            g, pp = divmod(p, 2)
            halves = []
            for head in (2 * p, 2 * p + 1):
                cols = slice(128 * (head % 4), 128 * (head % 4) + 128)
                denom = o_t[j, g][HEAD_DIM:, cols] + jnp.exp2(sinks[head] - ms[j, head])
                halves.append(o_t[j, g][:HEAD_DIM, cols] / denom)
            o = jnp.concatenate(halves, axis=0).T
            mixed_ref[rows[j], 128 * p:128 * p + 128] = o.astype(BF16)
        issue_chunk()

    heads = {}
    for j in range(n_blocks):
        for hd in range(RET_HEADS):
            c0 = 128 * hd
            q_f = proj_ref[rows[j], OFF_RQ + c0:OFF_RQ + c0 + 128]
            k = proj_ref[rows[j], OFF_RK + c0:OFF_RK + c0 + 128]
            v = proj_ref[rows[j], OFF_RV + c0:OFF_RV + c0 + 128].astype(BF16)
            inner = lax.dot_general(q_f.astype(BF16), k.astype(BF16), (((1,), (1,)), ((), ())),
                                    preferred_element_type=F32)
            kd = (k * ret_ref[2, hd]).astype(BF16)
            kv = lax.dot_general(kd, v, (((0,), (0,)), ((), ())), preferred_element_type=F32)
            heads[j, hd] = (q_f, v, inner, kv)
    issue_chunk()
    outs = {}
    for j in range(n_blocks):
        for hd in range(RET_HEADS):
            q_f, v, inner, kv = heads[j, hd]
            state = s_ref[hd] * first_keep if j == 0 else s_ref[hd]
            lhs = jnp.concatenate([(inner * ret_ref[0, hd]).astype(BF16), (q_f * ret_ref[1, hd]).astype(BF16)],
                                  axis=1)
            outs[j, hd] = jnp.dot(lhs, jnp.concatenate([v, state.astype(BF16)], axis=0),
                                  preferred_element_type=F32)
            s_ref[hd] = cdec_ref[hd] * state + kv
    issue_chunk()
    for j in range(n_blocks):
        for hd in range(RET_HEADS):
            c0 = 128 * hd
            o = outs[j, hd]
            gate = proj_ref[rows[j], OFF_RG + c0:OFF_RG + c0 + 128]
            mu = jnp.mean(o, axis=-1, keepdims=True)
            d = o - mu
            var = jnp.mean(d * d, axis=-1, keepdims=True)
            on = d * lax.rsqrt(var + EPS) * rg_ref[layer:layer + 1, c0:c0 + 128]
            y = gate * (1.0 / (1.0 + jnp.exp(-gate))) * on
            mixed_ref[rows[j], ATTN_WIDTH + c0:ATTN_WIDTH + c0 + 128] = y.astype(BF16)
            if hd % 2 == 1:
                issue_chunk()

    assert not chunks


def _mix_kernel(layer, tiles_per_seq,
                xn_ref, xr_ref, ada_n_ref, ada_r_ref, g1_ref, win_ref, gq_ref, gk_ref, sinks_ref,
                rg_ref, wout_ref, bias_ref, ret_ref, cdec_ref, w1_f32_ref, w2_f32_ref,
                o_ref, w1_bf16_ref, w2_bf16_ref,
                proj_a, proj_b, mixed_a, mixed_b, hn_ref, kwin_a, kwin_b, vwin_a, vwin_b, s_ref,
                *weight_copies):
    i = pl.program_id(0)
    if weight_copies:
        win_f32_ref, wout_f32_ref = win_ref, wout_ref
        win_ref, wout_ref = weight_copies

        @pl.when(i == 0)
        def _():
            win_ref[...] = win_f32_ref[...].astype(BF16)
            wout_ref[...] = wout_f32_ref[...].astype(BF16)

    w1_bf16_ref[...] = w1_f32_ref[...].astype(BF16)
    w2_bf16_ref[...] = w2_f32_ref[...].astype(BF16)

    @pl.when(i == 0)
    def _():
        proj_b[...] = jnp.zeros_like(proj_b)
        mixed_a[...] = jnp.zeros_like(mixed_a)
        kwin_a[...] = jnp.zeros_like(kwin_a)
        kwin_b[...] = jnp.zeros_like(kwin_b)
        vwin_a[...] = jnp.zeros_like(vwin_a)
        vwin_b[...] = jnp.zeros_like(vwin_b)
        s_ref[...] = jnp.zeros_like(s_ref)

    consts = (layer, gq_ref, sinks_ref, rg_ref, bias_ref, ret_ref, cdec_ref, s_ref)

    def slot(half, proj_cur, proj_next, mixed_cur, mixed_prev, kwin, kwin_next, vwin, vwin_next,
             first_tab, first_keep):
        def out_chunk(c):
            cols = slice(c * PROJ_CHUNK, min((c + 1) * PROJ_CHUNK, D_MODEL))
            y = jnp.dot(mixed_prev[...], wout_ref[:, cols], preferred_element_type=F32)
            o_ref[half, :, cols] = xr_ref[half, :, cols] + ada_r_ref[2:3, cols] * y

        def norm_next():
            x = xn_ref[half]
            ms = jnp.mean(x * x, axis=-1, keepdims=True)
            h = x * lax.rsqrt(ms + EPS) * g1_ref[layer:layer + 1, :]
            hn_ref[...] = (h * (1.0 + ada_n_ref[1:2, :]) + ada_n_ref[0:1, :]).astype(BF16)

        def in_chunk(c):
            cols = slice(c * PROJ_CHUNK, min((c + 1) * PROJ_CHUNK, IN_WIDTH))
            proj_next[:, cols] = jnp.dot(hn_ref[...], win_ref[:, cols], preferred_element_type=F32)
            if cols.start <= OFF_AK and OFF_AV + 2 * HEAD_DIM <= cols.stop:
                _build_windows(proj_next, _pair_gain(gk_ref, layer), kwin_next, kwin, vwin_next, vwin)

        def first_chunk():
            out_chunk(0)
            norm_next()
            out_chunk(1)

        chunks = [first_chunk]
        chunks += [functools.partial(out_chunk, c) for c in range(2, pl.cdiv(D_MODEL, PROJ_CHUNK))]
        chunks += [functools.partial(in_chunk, c) for c in range(pl.cdiv(IN_WIDTH, PROJ_CHUNK))]
        _mix_slot(chunks, proj_cur, mixed_cur, kwin, vwin, first_tab, first_keep, *consts)

    slot(0, proj_b, proj_a, mixed_b, mixed_a, kwin_b, kwin_a, vwin_b, vwin_a, 1, 1.0)
    seq_start = (2 * i) % tiles_per_seq == 0
    slot(1, proj_a, proj_b, mixed_a, mixed_b, kwin_a, kwin_b, vwin_a, vwin_b,
         jnp.where(seq_start, 0, 1), jnp.where(seq_start, 0.0, 1.0))


def _mix_call(layer, x, ada, g1, w_in, gq, gk, sinks, rg, w_out, tables, w_mlp1, w_mlp2):
    b, s, d = x.shape
    tiles_per_seq = s // MIX_ROWS
    pairs_per_seq = tiles_per_seq // 2
    n_pairs = b * pairs_per_seq
    bias_tab, ret_tab, cdec = tables
    x_pairs = x.reshape(n_pairs, 2, MIX_ROWS, d)

    nxt = lambda i: jnp.minimum(i, n_pairs - 1)
    res = lambda i: jnp.maximum(i - 1, 0)
    const2 = lambda i: (0, 0)
    const4 = lambda i: (0, 0, 0, 0)
    single = pl.Buffered(1)
    smem = pl.BlockSpec(memory_space=pltpu.SMEM)
    slab1, slab2 = w_mlp1.shape[1] // n_pairs, w_mlp2.shape[1] // n_pairs
    if w_in.dtype == F32:
        lay3 = lambda i: (layer, 0, 0)
        w_specs = [pl.BlockSpec((None, d, IN_WIDTH), lay3, pipeline_mode=single),
                   pl.BlockSpec((None, d, d), lay3, pipeline_mode=single)]
        w_copies = [pltpu.VMEM((d, IN_WIDTH), BF16), pltpu.VMEM((d, d), BF16)]
    else:
        w_specs = [pl.BlockSpec((d, IN_WIDTH), const2, pipeline_mode=single),
                   pl.BlockSpec((d, d), const2, pipeline_mode=single)]
        w_copies = []
    out, w1_b, w2_b = pl.pallas_call(
        functools.partial(_mix_kernel, layer, tiles_per_seq),
        grid=(n_pairs + 1,),
        in_specs=[
            pl.BlockSpec((None, 2, MIX_ROWS, d), lambda i: (nxt(i), 0, 0, 0)),
            pl.BlockSpec((None, 2, MIX_ROWS, d), lambda i: (res(i), 0, 0, 0)),
            pl.BlockSpec((None, None, 6, d), lambda i: (layer, nxt(i) // pairs_per_seq, 0, 0)),
            pl.BlockSpec((None, None, 6, d), lambda i: (layer, res(i) // pairs_per_seq, 0, 0)),
            pl.BlockSpec(g1.shape, const2),
            w_specs[0],
            pl.BlockSpec(gq.shape, const2),
            pl.BlockSpec(gk.shape, const2),
            smem,
            pl.BlockSpec(rg.shape, const2),
            w_specs[1],
            pl.BlockSpec(bias_tab.shape, const4, pipeline_mode=single),
            pl.BlockSpec(ret_tab.shape, const4, pipeline_mode=single),
            smem,
            pl.BlockSpec((None, slab1, D_FF), lambda i: (layer, nxt(i), 0)),
            pl.BlockSpec((None, slab2, d), lambda i: (layer, nxt(i), 0)),
        ],
        out_specs=[
            pl.BlockSpec((None, 2, MIX_ROWS, d), lambda i: (res(i), 0, 0, 0)),
            pl.BlockSpec((slab1, D_FF), lambda i: (nxt(i), 0)),
            pl.BlockSpec((slab2, d), lambda i: (nxt(i), 0)),
        ],
        out_shape=[
            jax.ShapeDtypeStruct((n_pairs, 2, MIX_ROWS, d), F32),
            jax.ShapeDtypeStruct(w_mlp1.shape[1:], BF16),
            jax.ShapeDtypeStruct(w_mlp2.shape[1:], BF16),
        ],
        scratch_shapes=[
            pltpu.VMEM((MIX_ROWS, IN_WIDTH), F32),
            pltpu.VMEM((MIX_ROWS, IN_WIDTH), F32),
            pltpu.VMEM((MIX_ROWS, d), BF16),
            pltpu.VMEM((MIX_ROWS, d), BF16),
            pltpu.VMEM((MIX_ROWS, d), BF16),
            pltpu.VMEM((N_KV_HEADS, BLOCK + MIX_ROWS, 2 * HEAD_DIM), BF16),
            pltpu.VMEM((N_KV_HEADS, BLOCK + MIX_ROWS, 2 * HEAD_DIM), BF16),
            pltpu.VMEM((2 * HEAD_DIM, BLOCK + MIX_ROWS), BF16),
            pltpu.VMEM((2 * HEAD_DIM, BLOCK + MIX_ROWS), BF16),
            pltpu.VMEM((RET_HEADS, 128, 128), F32),
        ] + w_copies,
        compiler_params=pltpu.CompilerParams(
            dimension_semantics=("arbitrary",), vmem_limit_bytes=VMEM_LIMIT),
        name="mix",
    )(x_pairs, x_pairs, ada, ada, g1, w_in, gq, gk, sinks, rg, w_out, bias_tab, ret_tab, cdec,
      w_mlp1, w_mlp2)
    return out.reshape(b, s, d), w1_b, w2_b


MLP_ROWS = 1024


def _mlp_kernel(layer, x_ref, ada_ref, g2_ref, w1_ref, w2_ref, *rest):
    if len(rest) == 1:
        (o_ref,) = rest
    else:
        win_f32_ref, wout_f32_ref, o_ref, win_bf16_ref, wout_bf16_ref = rest
        win_bf16_ref[...] = win_f32_ref[...].astype(BF16)
        wout_bf16_ref[...] = wout_f32_ref[...].astype(BF16)
    x = x_ref[...]
    ms = jnp.mean(x * x, axis=-1, keepdims=True)
    h = x * lax.rsqrt(ms + EPS) * g2_ref[layer:layer + 1, :]
    h = (h * (1.0 + ada_ref[4:5, :]) + ada_ref[3:4, :]).astype(BF16)
    acc = None
    for c in range(D_FF // FF_CHUNK):
        cols = slice(c * FF_CHUNK, (c + 1) * FF_CHUNK)
        a = jnp.dot(h, w1_ref[:, cols], preferred_element_type=F32)
        a = jnp.maximum(a, 0.0)
        part = jnp.dot((a * a).astype(BF16), w2_ref[cols, :], preferred_element_type=F32)
        acc = part if acc is None else acc + part
    o_ref[...] = x + ada_ref[5:6, :] * acc


def _mlp_call(layer, x, ada, g2, w1, w2, w_in=None, w_out=None):
    b, s, d = x.shape
    steps_per_seq = s // MLP_ROWS
    n_steps = b * steps_per_seq
    const2 = lambda i, j: (0, 0)
    single = pl.Buffered(1)
    in_specs = [
        pl.BlockSpec((None, MLP_ROWS, d), lambda i, j: (i, j, 0)),
        pl.BlockSpec((None, None, 6, d), lambda i, j: (layer, i, 0, 0)),
        pl.BlockSpec(g2.shape, const2),
        pl.BlockSpec((d, D_FF), const2, pipeline_mode=single),
        pl.BlockSpec((D_FF, d), const2, pipeline_mode=single),
    ]
    out_specs = [pl.BlockSpec((None, MLP_ROWS, d), lambda i, j: (i, j, 0))]
    out_shape = [jax.ShapeDtypeStruct((b, s, d), F32)]
    operands = [x, ada, g2, w1, w2]
    if w_in is not None:
        for w in (w_in, w_out):
            slab = w.shape[1] // n_steps
            step = lambda i, j: i * steps_per_seq + j
            in_specs.append(pl.BlockSpec((None, slab, w.shape[2]), lambda i, j: (layer + 1, step(i, j), 0)))
            out_specs.append(pl.BlockSpec((slab, w.shape[2]), lambda i, j: (step(i, j), 0)))
            out_shape.append(jax.ShapeDtypeStruct(w.shape[1:], BF16))
            operands.append(w)
    return pl.pallas_call(
        functools.partial(_mlp_kernel, layer),
        grid=(b, steps_per_seq),
        in_specs=in_specs,
        out_specs=out_specs,
        out_shape=out_shape,
        compiler_params=pltpu.CompilerParams(
            dimension_semantics=("arbitrary", "arbitrary"), vmem_limit_bytes=VMEM_LIMIT),
        name="mlp",
    )(*operands)


def kernel(x, c, norm1_g, norm2_g, w_ada, b_ada, w_in, q_norm_g, k_norm_g, sinks, ret_norm_g,
           w_out, w_mlp1, w_mlp2):
    b = x.shape[0]
    tables = tuple(jnp.asarray(t) for t in _constant_tables())
    c_pad = jnp.pad(c, ((0, 8 - b), (0, 0)))
    ada = _ada_call(c_pad, w_ada, b_ada)[:, :b].reshape(DEPTH, b, 6, D_MODEL)
    w_in_b, w_out_b = w_in, w_out
    for l in range(DEPTH):
        x, w1_b, w2_b = _mix_call(l, x, ada, norm1_g, w_in_b, q_norm_g, k_norm_g, sinks, ret_norm_g,
                                  w_out_b, tables, w_mlp1, w_mlp2)
        if l + 1 < DEPTH:
            x, w_in_b, w_out_b = _mlp_call(l, x, ada, norm2_g, w1_b, w2_b, w_in, w_out)
        else:
            (x,) = _mlp_call(l, x, ada, norm2_g, w1_b, w2_b)
    return x
```

```python
import functools

import numpy as np
import jax
import jax.numpy as jnp
from jax import lax
from jax.experimental import pallas as pl
from jax.experimental.pallas import tpu as pltpu

D_MODEL = 1024
DEPTH = 2
ATTN_WIDTH = 512
RET_WIDTH = 512
HEAD_DIM = 64
N_Q_HEADS = 8
N_KV_HEADS = 2
N_PAIRS = N_Q_HEADS // 2
BLOCK = 128
RET_HEADS = 4
RET_DK = 128
D_FF = 4 * D_MODEL
EPS = 1e-6
NEG_INF = -1e30
LOG2E = 1.4426950408889634
IN_WIDTH = 2816
OFF_AQ, OFF_AK, OFF_AV, OFF_RQ, OFF_RK, OFF_RV, OFF_RG = 0, 512, 640, 768, 1280, 1792, 2304

MIX_ROWS = 256
PROJ_CHUNK = 256
FF_CHUNK = 1024
ADA_COLS = 1536
VMEM_LIMIT = 60 * 1024 * 1024

F32 = jnp.float32
BF16 = jnp.bfloat16


@functools.lru_cache(maxsize=None)
def _constant_tables():
    q_pos = np.arange(BLOCK)[:, None]
    k_pos = np.arange(2 * BLOCK)[None, :]
    dist = q_pos + BLOCK - k_pos
    valid = (dist >= 0) & (dist < BLOCK)
    valid_first = valid & (k_pos >= BLOCK)
    slopes = np.exp2(-8.0 * np.arange(1, N_Q_HEADS + 1, dtype=np.float64) / N_Q_HEADS)
    bias = -slopes[:, None, None] * dist[None].astype(np.float64)
    bias_tab = np.stack([np.where(v[None], bias * LOG2E, NEG_INF) for v in (valid_first, valid)])
    bias_tab = np.ascontiguousarray(bias_tab.transpose(0, 1, 3, 2), np.float32)

    log_gamma = np.log1p(-np.exp2(-5.0 - np.arange(RET_HEADS, dtype=np.float64)))
    idx = np.arange(BLOCK, dtype=np.float64)
    rel = idx[:, None] - idx[None, :]
    k_scale = RET_DK ** -0.5
    decay_in = np.where(rel >= 0, np.exp(log_gamma[:, None, None] * np.maximum(rel, 0.0)), 0.0) * k_scale
    q_decay = np.exp(log_gamma[:, None] * (idx[None, :] + 1.0))
    k_decay = np.exp(log_gamma[:, None] * (BLOCK - 1.0 - idx[None, :])) * k_scale
    q_decay = np.broadcast_to(q_decay[:, :, None], (RET_HEADS, BLOCK, BLOCK))
    k_decay = np.broadcast_to(k_decay[:, :, None], (RET_HEADS, BLOCK, BLOCK))
    chunk_decay = np.exp(log_gamma * BLOCK)
    ret_tab = np.stack([decay_in, q_decay, k_decay]).astype(np.float32)
    return bias_tab, ret_tab, chunk_decay.astype(np.float32)


def _ada_kernel(c_ref, w_ref, b_ref, o_ref):
    c = c_ref[...]
    c_act = c * (1.0 / (1.0 + jnp.exp(-c)))
    o_ref[...] = jnp.dot(c_act.astype(BF16), w_ref[...].astype(BF16),
                         preferred_element_type=F32) + b_ref[...]


def _ada_call(c_pad, w_ada, b_ada):
    rows = c_pad.shape[0]
    n = w_ada.shape[-1]
    return pl.pallas_call(
        _ada_kernel,
        grid=(DEPTH, n // ADA_COLS),
        in_specs=[
            pl.BlockSpec((rows, D_MODEL), lambda l, j: (0, 0)),
            pl.BlockSpec((None, D_MODEL, ADA_COLS), lambda l, j: (l, 0, j)),
            pl.BlockSpec((None, 1, ADA_COLS), lambda l, j: (l, 0, j)),
        ],
        out_specs=pl.BlockSpec((None, rows, ADA_COLS), lambda l, j: (l, 0, j)),
        out_shape=jax.ShapeDtypeStruct((DEPTH, rows, n), F32),
        compiler_params=pltpu.CompilerParams(
            dimension_semantics=("arbitrary", "arbitrary"), vmem_limit_bytes=VMEM_LIMIT),
        name="ada",
    )(c_pad, w_ada, b_ada.reshape(DEPTH, 1, n))


def _head_norm(a, gain_row):
    lo = lax.broadcasted_iota(jnp.int32, a.shape, 1) < HEAD_DIM
    a2 = a * a
    s_lo = jnp.sum(jnp.where(lo, a2, 0.0), axis=-1, keepdims=True)
    s_hi = jnp.sum(jnp.where(lo, 0.0, a2), axis=-1, keepdims=True)
    mean_sq = jnp.where(lo, s_lo, s_hi) * (1.0 / HEAD_DIM)
    return a * lax.rsqrt(mean_sq + EPS) * gain_row


def _pair_gain(g_ref, layer):
    row = g_ref[layer:layer + 1, :]
    return jnp.concatenate([row, row], axis=1)


def _build_windows(proj_ref, gk_row, kwin_ref, kwin_prev, vwin_ref, vwin_prev):
    kn = _head_norm(proj_ref[:, OFF_AK:OFF_AK + 128], gk_row)
    kn_rot = pltpu.roll(kn, HEAD_DIM, axis=1)
    lo = lax.broadcasted_iota(jnp.int32, kn.shape, 1) < HEAD_DIM
    k_dup = (jnp.where(lo, kn, kn_rot), jnp.where(lo, kn_rot, kn))
    for g in range(N_KV_HEADS):
        kwin_ref[g, 0:BLOCK, :] = kwin_prev[g, MIX_ROWS:MIX_ROWS + BLOCK, :]
        kwin_ref[g, BLOCK:BLOCK + MIX_ROWS, :] = k_dup[g].astype(BF16)
    vwin_ref[:, 0:BLOCK] = vwin_prev[:, MIX_ROWS:MIX_ROWS + BLOCK]
    vwin_ref[:, BLOCK:BLOCK + MIX_ROWS] = proj_ref[:, OFF_AV:OFF_AV + 128].T.astype(BF16)


def _mix_slot(chunks, proj_ref, mixed_ref, kwin_ref, vwin_ref,
              first_tab, first_keep, layer, gq_ref, sinks_ref, rg_ref,
              bias_ref, ret_ref, cdec_ref, s_ref):
    chunks = list(chunks)
    n_chunks = len(chunks)
    n_points = 5 + 5 * (MIX_ROWS // BLOCK)
    point = [0]

    def issue_chunk():
        point[0] += 1
        while chunks and (n_chunks - len(chunks)) * n_points < point[0] * n_chunks:
            chunks.pop(0)()

    issue_chunk()

    lo = lax.broadcasted_iota(jnp.int32, (BLOCK, 2 * HEAD_DIM), 1) < HEAD_DIM
    gq = _pair_gain(gq_ref, layer) * (HEAD_DIM ** -0.5 * LOG2E)
    n_blocks = MIX_ROWS // BLOCK
    rows = [pl.ds(j * BLOCK, BLOCK) for j in range(n_blocks)]
    wins = [slice(j * BLOCK, (j + 2) * BLOCK) for j in range(n_blocks)]
    tabs = [first_tab if j == 0 else 1 for j in range(n_blocks)]

    s_t = {}
    for j in range(n_blocks):
        for g in range(N_KV_HEADS):
            qs = []
            for pp in range(2):
                p = 2 * g + pp
                qn = _head_norm(proj_ref[rows[j], OFF_AQ + 128 * p:OFF_AQ + 128 * p + 128], gq)
                qs.append(jnp.where(lo, qn, 0.0).astype(BF16))
                qs.append(jnp.where(lo, 0.0, qn).astype(BF16))
            s_t[j, g] = lax.dot_general(kwin_ref[g, wins[j], :], jnp.concatenate(qs, axis=0),
                                        (((1,), (1,)), ((), ())),
                                        preferred_element_type=F32)
    issue_chunk()
    sinks = [sinks_ref[layer, head] * LOG2E for head in range(N_Q_HEADS)]
    es, ms = {}, {}
    for j in range(n_blocks):
        for head in range(N_Q_HEADS):
            g, hh = divmod(head, 4)
            sh = s_t[j, g][:, 128 * hh:128 * hh + 128] + bias_ref[tabs[j], head]
            ms[j, head] = jnp.maximum(jnp.max(sh, axis=0, keepdims=True), sinks[head])
            es[j, head] = jnp.exp2(sh - ms[j, head]).astype(BF16)
            if head % 4 == 3:
                issue_chunk()
    ones = jnp.ones((HEAD_DIM, 2 * BLOCK), BF16)
    o_t = {}
    for j in range(n_blocks):
        for g in range(N_KV_HEADS):
            lhs = jnp.concatenate([vwin_ref[64 * g:64 * g + 64, wins[j]], ones], axis=0)
            p_t = jnp.concatenate([es[j, 4 * g + hh] for hh in range(4)], axis=1)
            o_t[j, g] = jnp.dot(lhs, p_t, preferred_element_type=F32)
    issue_chunk()
    for j in range(n_blocks):
        for p in range(N_PAIRS):
            g, pp = divmod(p, 2)
            halves = []
            for head in (2 * p, 2 * p + 1):
                cols = slice(128 * (head % 4), 128 * (head % 4) + 128)
                denom = o_t[j, g][HEAD_DIM:, cols] + jnp.exp2(sinks[head] - ms[j, head])
                halves.append(o_t[j, g][:HEAD_DIM, cols] / denom)
            o = jnp.concatenate(halves, axis=0).T
            mixed_ref[rows[j], 128 * p:128 * p + 128] = o.astype(BF16)
        issue_chunk()

    heads = {}
    for j in range(n_blocks):
        for hd in range(RET_HEADS):
            c0 = 128 * hd
            q_f = proj_ref[rows[j], OFF_RQ + c0:OFF_RQ + c0 + 128]
            k = proj_ref[rows[j], OFF_RK + c0:OFF_RK + c0 + 128]
            v = proj_ref[rows[j], OFF_RV + c0:OFF_RV + c0 + 128].astype(BF16)
            inner = lax.dot_general(q_f.astype(BF16), k.astype(BF16), (((1,), (1,)), ((), ())),
                                    preferred_element_type=F32)
            kd = (k * ret_ref[2, hd]).astype(BF16)
            kv = lax.dot_general(kd, v, (((0,), (0,)), ((), ())), preferred_element_type=F32)
            heads[j, hd] = (q_f, v, inner, kv)
    issue_chunk()
    outs = {}
    for j in range(n_blocks):
        for hd in range(RET_HEADS):
            q_f, v, inner, kv = heads[j, hd]
            state = s_ref[hd] * first_keep if j == 0 else s_ref[hd]
            lhs = jnp.concatenate([(inner * ret_ref[0, hd]).astype(BF16), (q_f * ret_ref[1, hd]).astype(BF16)],
                                  axis=1)
            outs[j, hd] = jnp.dot(lhs, jnp.concatenate([v, state.astype(BF16)], axis=0),
                                  preferred_element_type=F32)
            s_ref[hd] = cdec_ref[hd] * state + kv
    issue_chunk()
    for j in range(n_blocks):
        for hd in range(RET_HEADS):
            c0 = 128 * hd
            o = outs[j, hd]
            gate = proj_ref[rows[j], OFF_RG + c0:OFF_RG + c0 + 128]
            mu = jnp.mean(o, axis=-1, keepdims=True)
            d = o - mu
            var = jnp.mean(d * d, axis=-1, keepdims=True)
            on = d * lax.rsqrt(var + EPS) * rg_ref[layer:layer + 1, c0:c0 + 128]
            y = gate * (1.0 / (1.0 + jnp.exp(-gate))) * on
            mixed_ref[rows[j], ATTN_WIDTH + c0:ATTN_WIDTH + c0 + 128] = y.astype(BF16)
            if hd % 2 == 1:
                issue_chunk()

    assert not chunks


def _mix_kernel(layer, tiles_per_seq,
                xn_ref, xr_ref, ada_n_ref, ada_r_ref, g1_ref, win_f32_ref, gq_ref, gk_ref, sinks_ref,
                rg_ref, wout_f32_ref, bias_ref, ret_ref, cdec_ref, w1_f32_ref, w2_f32_ref,
                o_ref, w1_bf16_ref, w2_bf16_ref,
                proj_a, proj_b, mixed_a, mixed_b, hn_ref, kwin_a, kwin_b, vwin_a, vwin_b, s_ref,
                win_ref, wout_ref):
    i = pl.program_id(0)

    @pl.when(i == 0)
    def _():
        win_ref[...] = win_f32_ref[...].astype(BF16)
        wout_ref[...] = wout_f32_ref[...].astype(BF16)

    w1_bf16_ref[...] = w1_f32_ref[...].astype(BF16)
    w2_bf16_ref[...] = w2_f32_ref[...].astype(BF16)

    @pl.when(i == 0)
    def _():
        proj_b[...] = jnp.zeros_like(proj_b)
        mixed_a[...] = jnp.zeros_like(mixed_a)
        kwin_a[...] = jnp.zeros_like(kwin_a)
        kwin_b[...] = jnp.zeros_like(kwin_b)
        vwin_a[...] = jnp.zeros_like(vwin_a)
        vwin_b[...] = jnp.zeros_like(vwin_b)
        s_ref[...] = jnp.zeros_like(s_ref)

    consts = (layer, gq_ref, sinks_ref, rg_ref, bias_ref, ret_ref, cdec_ref, s_ref)

    def slot(half, proj_cur, proj_next, mixed_cur, mixed_prev, kwin, kwin_next, vwin, vwin_next,
             first_tab, first_keep):
        def out_chunk(c):
            cols = slice(c * PROJ_CHUNK, min((c + 1) * PROJ_CHUNK, D_MODEL))
            y = jnp.dot(mixed_prev[...], wout_ref[:, cols], preferred_element_type=F32)
            o_ref[half, :, cols] = xr_ref[half, :, cols] + ada_r_ref[2:3, cols] * y

        def norm_next():
            x = xn_ref[half]
            ms = jnp.mean(x * x, axis=-1, keepdims=True)
            h = x * lax.rsqrt(ms + EPS) * g1_ref[layer:layer + 1, :]
            hn_ref[...] = (h * (1.0 + ada_n_ref[1:2, :]) + ada_n_ref[0:1, :]).astype(BF16)

        def in_chunk(c):
            cols = slice(c * PROJ_CHUNK, min((c + 1) * PROJ_CHUNK, IN_WIDTH))
            proj_next[:, cols] = jnp.dot(hn_ref[...], win_ref[:, cols], preferred_element_type=F32)
            if cols.start <= OFF_AK and OFF_AV + 2 * HEAD_DIM <= cols.stop:
                _build_windows(proj_next, _pair_gain(gk_ref, layer), kwin_next, kwin, vwin_next, vwin)

        def first_chunk():
            out_chunk(0)
            norm_next()
            out_chunk(1)

        chunks = [first_chunk]
        chunks += [functools.partial(out_chunk, c) for c in range(2, pl.cdiv(D_MODEL, PROJ_CHUNK))]
        chunks += [functools.partial(in_chunk, c) for c in range(pl.cdiv(IN_WIDTH, PROJ_CHUNK))]
        _mix_slot(chunks, proj_cur, mixed_cur, kwin, vwin, first_tab, first_keep, *consts)

    slot(0, proj_b, proj_a, mixed_b, mixed_a, kwin_b, kwin_a, vwin_b, vwin_a, 1, 1.0)
    seq_start = (2 * i) % tiles_per_seq == 0
    slot(1, proj_a, proj_b, mixed_a, mixed_b, kwin_a, kwin_b, vwin_a, vwin_b,
         jnp.where(seq_start, 0, 1), jnp.where(seq_start, 0.0, 1.0))


def _mix_call(layer, x, ada, g1, w_in, gq, gk, sinks, rg, w_out, tables, w_mlp1, w_mlp2):
    b, s, d = x.shape
    tiles_per_seq = s // MIX_ROWS
    pairs_per_seq = tiles_per_seq // 2
    n_pairs = b * pairs_per_seq
    bias_tab, ret_tab, cdec = tables
    x_pairs = x.reshape(n_pairs, 2, MIX_ROWS, d)

    nxt = lambda i: jnp.minimum(i, n_pairs - 1)
    res = lambda i: jnp.maximum(i - 1, 0)
    const2 = lambda i: (0, 0)
    const4 = lambda i: (0, 0, 0, 0)
    single = pl.Buffered(1)
    smem = pl.BlockSpec(memory_space=pltpu.SMEM)
    slab1, slab2 = w_mlp1.shape[1] // n_pairs, w_mlp2.shape[1] // n_pairs
    lay3 = lambda i: (layer, 0, 0)
    out, w1_b, w2_b = pl.pallas_call(
        functools.partial(_mix_kernel, layer, tiles_per_seq),
        grid=(n_pairs + 1,),
        in_specs=[
            pl.BlockSpec((None, 2, MIX_ROWS, d), lambda i: (nxt(i), 0, 0, 0)),
            pl.BlockSpec((None, 2, MIX_ROWS, d), lambda i: (res(i), 0, 0, 0)),
            pl.BlockSpec((None, None, 6, d), lambda i: (layer, nxt(i) // pairs_per_seq, 0, 0)),
            pl.BlockSpec((None, None, 6, d), lambda i: (layer, res(i) // pairs_per_seq, 0, 0)),
            pl.BlockSpec(g1.shape, const2),
            pl.BlockSpec((None, d, IN_WIDTH), lay3, pipeline_mode=single),
            pl.BlockSpec(gq.shape, const2),
            pl.BlockSpec(gk.shape, const2),
            smem,
            pl.BlockSpec(rg.shape, const2),
            pl.BlockSpec((None, d, d), lay3, pipeline_mode=single),
            pl.BlockSpec(bias_tab.shape, const4, pipeline_mode=single),
            pl.BlockSpec(ret_tab.shape, const4, pipeline_mode=single),
            smem,
            pl.BlockSpec((None, slab1, D_FF), lambda i: (layer, nxt(i), 0)),
            pl.BlockSpec((None, slab2, d), lambda i: (layer, nxt(i), 0)),
        ],
        out_specs=[
            pl.BlockSpec((None, 2, MIX_ROWS, d), lambda i: (res(i), 0, 0, 0)),
            pl.BlockSpec((slab1, D_FF), lambda i: (nxt(i), 0)),
            pl.BlockSpec((slab2, d), lambda i: (nxt(i), 0)),
        ],
        out_shape=[
            jax.ShapeDtypeStruct((n_pairs, 2, MIX_ROWS, d), F32),
            jax.ShapeDtypeStruct(w_mlp1.shape[1:], BF16),
            jax.ShapeDtypeStruct(w_mlp2.shape[1:], BF16),
        ],
        scratch_shapes=[
            pltpu.VMEM((MIX_ROWS, IN_WIDTH), F32),
            pltpu.VMEM((MIX_ROWS, IN_WIDTH), F32),
            pltpu.VMEM((MIX_ROWS, d), BF16),
            pltpu.VMEM((MIX_ROWS, d), BF16),
            pltpu.VMEM((MIX_ROWS, d), BF16),
            pltpu.VMEM((N_KV_HEADS, BLOCK + MIX_ROWS, 2 * HEAD_DIM), BF16),
            pltpu.VMEM((N_KV_HEADS, BLOCK + MIX_ROWS, 2 * HEAD_DIM), BF16),
            pltpu.VMEM((2 * HEAD_DIM, BLOCK + MIX_ROWS), BF16),
            pltpu.VMEM((2 * HEAD_DIM, BLOCK + MIX_ROWS), BF16),
            pltpu.VMEM((RET_HEADS, 128, 128), F32),
            pltpu.VMEM((d, IN_WIDTH), BF16),
            pltpu.VMEM((d, d), BF16),
        ],
        compiler_params=pltpu.CompilerParams(
            dimension_semantics=("arbitrary",), vmem_limit_bytes=VMEM_LIMIT),
        name="mix",
    )(x_pairs, x_pairs, ada, ada, g1, w_in, gq, gk, sinks, rg, w_out, bias_tab, ret_tab, cdec,
      w_mlp1, w_mlp2)
    return out.reshape(b, s, d), w1_b, w2_b


MLP_ROWS = 1024


def _mlp_kernel(layer, x_ref, ada_ref, g2_ref, w1_ref, w2_ref, o_ref):
    x = x_ref[...]
    ms = jnp.mean(x * x, axis=-1, keepdims=True)
    h = x * lax.rsqrt(ms + EPS) * g2_ref[layer:layer + 1, :]
    h = (h * (1.0 + ada_ref[4:5, :]) + ada_ref[3:4, :]).astype(BF16)
    acc = None
    for c in range(D_FF // FF_CHUNK):
        cols = slice(c * FF_CHUNK, (c + 1) * FF_CHUNK)
        a = jnp.dot(h, w1_ref[:, cols], preferred_element_type=F32)
        a = jnp.maximum(a, 0.0)
        part = jnp.dot((a * a).astype(BF16), w2_ref[cols, :], preferred_element_type=F32)
        acc = part if acc is None else acc + part
    o_ref[...] = x + ada_ref[5:6, :] * acc


def _mlp_call(layer, x, ada, g2, w1, w2):
    b, s, d = x.shape
    const2 = lambda i, j: (0, 0)
    single = pl.Buffered(1)
    return pl.pallas_call(
        functools.partial(_mlp_kernel, layer),
        grid=(b, s // MLP_ROWS),
        in_specs=[
            pl.BlockSpec((None, MLP_ROWS, d), lambda i, j: (i, j, 0)),
            pl.BlockSpec((None, None, 6, d), lambda i, j: (layer, i, 0, 0)),
            pl.BlockSpec(g2.shape, const2),
            pl.BlockSpec((d, D_FF), const2, pipeline_mode=single),
            pl.BlockSpec((D_FF, d), const2, pipeline_mode=single),
        ],
        out_specs=pl.BlockSpec((None, MLP_ROWS, d), lambda i, j: (i, j, 0)),
        out_shape=jax.ShapeDtypeStruct((b, s, d), F32),
        compiler_params=pltpu.CompilerParams(
            dimension_semantics=("arbitrary", "arbitrary"), vmem_limit_bytes=VMEM_LIMIT),
        name="mlp",
    )(x, ada, g2, w1, w2)


def kernel(x, c, norm1_g, norm2_g, w_ada, b_ada, w_in, q_norm_g, k_norm_g, sinks, ret_norm_g,
           w_out, w_mlp1, w_mlp2):
    b = x.shape[0]
    tables = tuple(jnp.asarray(t) for t in _constant_tables())
    c_pad = jnp.pad(c, ((0, 8 - b), (0, 0)))
    ada = _ada_call(c_pad, w_ada, b_ada)[:, :b].reshape(DEPTH, b, 6, D_MODEL)
    for l in range(DEPTH):
        x, w1_b, w2_b = _mix_call(l, x, ada, norm1_g, w_in, q_norm_g, k_norm_g, sinks, ret_norm_g,
                                  w_out, tables, w_mlp1, w_mlp2)
        x = _mlp_call(l, x, ada, norm2_g, w1_b, w2_b)
    return x
```

```python
import functools

import numpy as np
import jax
import jax.numpy as jnp
from jax import lax
from jax.experimental import pallas as pl
from jax.experimental.pallas import tpu as pltpu

D_MODEL = 1024
DEPTH = 2
ATTN_WIDTH = 512
RET_WIDTH = 512
HEAD_DIM = 64
N_Q_HEADS = 8
N_KV_HEADS = 2
N_PAIRS = N_Q_HEADS // 2
BLOCK = 128
RET_HEADS = 4
RET_DK = 128
D_FF = 4 * D_MODEL
EPS = 1e-6
NEG_INF = -1e30
LOG2E = 1.4426950408889634
IN_WIDTH = 2816
OFF_AQ, OFF_AK, OFF_AV, OFF_RQ, OFF_RK, OFF_RV, OFF_RG = 0, 512, 640, 768, 1280, 1792, 2304

MIX_ROWS = 256
PROJ_CHUNK = 256
FF_CHUNK = 1024
ADA_COLS = 1536
VMEM_LIMIT = 60 * 1024 * 1024

F32 = jnp.float32
BF16 = jnp.bfloat16


@functools.lru_cache(maxsize=None)
def _constant_tables():
    q_pos = np.arange(BLOCK)[:, None]
    k_pos = np.arange(2 * BLOCK)[None, :]
    dist = q_pos + BLOCK - k_pos
    valid = (dist >= 0) & (dist < BLOCK)
    valid_first = valid & (k_pos >= BLOCK)
    slopes = np.exp2(-8.0 * np.arange(1, N_Q_HEADS + 1, dtype=np.float64) / N_Q_HEADS)
    bias = -slopes[:, None, None] * dist[None].astype(np.float64)
    bias_tab = np.stack([np.where(v[None], bias * LOG2E, NEG_INF) for v in (valid_first, valid)])
    bias_tab = np.ascontiguousarray(bias_tab.transpose(0, 1, 3, 2), np.float32)

    log_gamma = np.log1p(-np.exp2(-5.0 - np.arange(RET_HEADS, dtype=np.float64)))
    idx = np.arange(BLOCK, dtype=np.float64)
    rel = idx[:, None] - idx[None, :]
    k_scale = RET_DK ** -0.5
    decay_in = np.where(rel >= 0, np.exp(log_gamma[:, None, None] * np.maximum(rel, 0.0)), 0.0) * k_scale
    q_decay = np.exp(log_gamma[:, None] * (idx[None, :] + 1.0))
    k_decay = np.exp(log_gamma[:, None] * (BLOCK - 1.0 - idx[None, :])) * k_scale
    q_decay = np.broadcast_to(q_decay[:, :, None], (RET_HEADS, BLOCK, BLOCK))
    k_decay = np.broadcast_to(k_decay[:, :, None], (RET_HEADS, BLOCK, BLOCK))
    chunk_decay = np.exp(log_gamma * BLOCK)
    ret_tab = np.stack([decay_in, q_decay, k_decay]).astype(np.float32)
    return bias_tab, ret_tab, chunk_decay.astype(np.float32)


def _ada_kernel(c_ref, w_ref, b_ref, o_ref):
    c = c_ref[...]
    c_act = c * (1.0 / (1.0 + jnp.exp(-c)))
    o_ref[...] = jnp.dot(c_act.astype(BF16), w_ref[...].astype(BF16),
                         preferred_element_type=F32) + b_ref[...]


def _ada_call(c_pad, w_ada, b_ada):
    rows = c_pad.shape[0]
    n = w_ada.shape[-1]
    return pl.pallas_call(
        _ada_kernel,
        grid=(DEPTH, n // ADA_COLS),
        in_specs=[
            pl.BlockSpec((rows, D_MODEL), lambda l, j: (0, 0)),
            pl.BlockSpec((None, D_MODEL, ADA_COLS), lambda l, j: (l, 0, j)),
            pl.BlockSpec((None, 1, ADA_COLS), lambda l, j: (l, 0, j)),
        ],
        out_specs=pl.BlockSpec((None, rows, ADA_COLS), lambda l, j: (l, 0, j)),
        out_shape=jax.ShapeDtypeStruct((DEPTH, rows, n), F32),
        compiler_params=pltpu.CompilerParams(
            dimension_semantics=("arbitrary", "arbitrary"), vmem_limit_bytes=VMEM_LIMIT),
        name="ada",
    )(c_pad, w_ada, b_ada.reshape(DEPTH, 1, n))


def _head_norm(a, gain_row):
    lo = lax.broadcasted_iota(jnp.int32, a.shape, 1) < HEAD_DIM
    a2 = a * a
    s_lo = jnp.sum(jnp.where(lo, a2, 0.0), axis=-1, keepdims=True)
    s_hi = jnp.sum(jnp.where(lo, 0.0, a2), axis=-1, keepdims=True)
    mean_sq = jnp.where(lo, s_lo, s_hi) * (1.0 / HEAD_DIM)
    return a * lax.rsqrt(mean_sq + EPS) * gain_row


def _pair_gain(g_ref, layer):
    row = g_ref[layer:layer + 1, :]
    return jnp.concatenate([row, row], axis=1)


def _build_windows(proj_ref, gk_row, kwin_ref, kwin_prev, vwin_ref, vwin_prev):
    kn = _head_norm(proj_ref[:, OFF_AK:OFF_AK + 128], gk_row)
    kn_rot = pltpu.roll(kn, HEAD_DIM, axis=1)
    lo = lax.broadcasted_iota(jnp.int32, kn.shape, 1) < HEAD_DIM
    k_dup = (jnp.where(lo, kn, kn_rot), jnp.where(lo, kn_rot, kn))
    for g in range(N_KV_HEADS):
        kwin_ref[g, 0:BLOCK, :] = kwin_prev[g, MIX_ROWS:MIX_ROWS + BLOCK, :]
        kwin_ref[g, BLOCK:BLOCK + MIX_ROWS, :] = k_dup[g].astype(BF16)
    vwin_ref[:, 0:BLOCK] = vwin_prev[:, MIX_ROWS:MIX_ROWS + BLOCK]
    vwin_ref[:, BLOCK:BLOCK + MIX_ROWS] = proj_ref[:, OFF_AV:OFF_AV + 128].T.astype(BF16)


def _mix_slot(chunks, proj_ref, mixed_ref, kwin_ref, vwin_ref,
              seq_start, layer, gq_ref, sinks_ref, rg_ref,
              bias_ref, ret_ref, cdec_ref, s_ref):
    first_tab = 1 if seq_start is False else jnp.where(seq_start, 0, 1)
    chunks = list(chunks)
    n_chunks = len(chunks)
    n_points = 5 + 5 * (MIX_ROWS // BLOCK)
    point = [0]

    def issue_chunk():
        point[0] += 1
        while chunks and (n_chunks - len(chunks)) * n_points < point[0] * n_chunks:
            chunks.pop(0)()

    issue_chunk()

    lo = lax.broadcasted_iota(jnp.int32, (BLOCK, 2 * HEAD_DIM), 1) < HEAD_DIM
    gq = _pair_gain(gq_ref, layer) * (HEAD_DIM ** -0.5 * LOG2E)
    n_blocks = MIX_ROWS // BLOCK
    rows = [pl.ds(j * BLOCK, BLOCK) for j in range(n_blocks)]
    wins = [slice(j * BLOCK, (j + 2) * BLOCK) for j in range(n_blocks)]
    tabs = [first_tab if j == 0 else 1 for j in range(n_blocks)]

    s_t = {}
    for j in range(n_blocks):
        for g in range(N_KV_HEADS):
            qs = []
            for pp in range(2):
                p = 2 * g + pp
                qn = _head_norm(proj_ref[rows[j], OFF_AQ + 128 * p:OFF_AQ + 128 * p + 128], gq)
                qs.append(jnp.where(lo, qn, 0.0).astype(BF16))
                qs.append(jnp.where(lo, 0.0, qn).astype(BF16))
            s_t[j, g] = lax.dot_general(kwin_ref[g, wins[j], :], jnp.concatenate(qs, axis=0),
                                        (((1,), (1,)), ((), ())),
                                        preferred_element_type=F32)
    issue_chunk()
    sinks = [sinks_ref[layer, head] * LOG2E for head in range(N_Q_HEADS)]
    es, ms = {}, {}
    for j in range(n_blocks):
        for head in range(N_Q_HEADS):
            g, hh = divmod(head, 4)
            sh = s_t[j, g][:, 128 * hh:128 * hh + 128] + bias_ref[tabs[j], head]
            ms[j, head] = jnp.maximum(jnp.max(sh, axis=0, keepdims=True), sinks[head])
            es[j, head] = jnp.exp2(sh - ms[j, head]).astype(BF16)
            if head % 4 == 3:
                issue_chunk()
    ones = jnp.ones((HEAD_DIM, 2 * BLOCK), BF16)
    o_t = {}
    for j in range(n_blocks):
        for g in range(N_KV_HEADS):
            lhs = jnp.concatenate([vwin_ref[64 * g:64 * g + 64, wins[j]], ones], axis=0)
            p_t = jnp.concatenate([es[j, 4 * g + hh] for hh in range(4)], axis=1)
            o_t[j, g] = jnp.dot(lhs, p_t, preferred_element_type=F32)
    issue_chunk()
    for j in range(n_blocks):
        for p in range(N_PAIRS):
            g, pp = divmod(p, 2)
            halves = []
            for head in (2 * p, 2 * p + 1):
                cols = slice(128 * (head % 4), 128 * (head % 4) + 128)
                denom = o_t[j, g][HEAD_DIM:, cols] + jnp.exp2(sinks[head] - ms[j, head])
                halves.append(o_t[j, g][:HEAD_DIM, cols] / denom)
            o = jnp.concatenate(halves, axis=0).T
            mixed_ref[rows[j], 128 * p:128 * p + 128] = o.astype(BF16)
        issue_chunk()

    heads = {}
    for j in range(n_blocks):
        for hd in range(RET_HEADS):
            c0 = 128 * hd
            q_f = proj_ref[rows[j], OFF_RQ + c0:OFF_RQ + c0 + 128]
            k = proj_ref[rows[j], OFF_RK + c0:OFF_RK + c0 + 128]
            v = proj_ref[rows[j], OFF_RV + c0:OFF_RV + c0 + 128].astype(BF16)
            inner = lax.dot_general(q_f.astype(BF16), k.astype(BF16), (((1,), (1,)), ((), ())),
                                    preferred_element_type=F32)
            kd = (k * ret_ref[2, hd]).astype(BF16)
            kv = lax.dot_general(kd, v, (((0,), (0,)), ((), ())), preferred_element_type=F32)
            heads[j, hd] = (q_f, v, inner, kv)
    issue_chunk()
    outs = {}
    for j in range(n_blocks):
        for hd in range(RET_HEADS):
            q_f, v, inner, kv = heads[j, hd]
            state = s_ref[hd]
            if j == 0 and seq_start is not False:
                state = jnp.where(seq_start, 0.0, state)
            lhs = jnp.concatenate([(inner * ret_ref[0, hd]).astype(BF16), (q_f * ret_ref[1, hd]).astype(BF16)],
                                  axis=1)
            outs[j, hd] = jnp.dot(lhs, jnp.concatenate([v, state.astype(BF16)], axis=0),
                                  preferred_element_type=F32)
            s_ref[hd] = cdec_ref[hd] * state + kv
    issue_chunk()
    for j in range(n_blocks):
        for hd in range(RET_HEADS):
            c0 = 128 * hd
            o = outs[j, hd]
            gate = proj_ref[rows[j], OFF_RG + c0:OFF_RG + c0 + 128]
            mu = jnp.mean(o, axis=-1, keepdims=True)
            d = o - mu
            var = jnp.mean(d * d, axis=-1, keepdims=True)
            on = d * lax.rsqrt(var + EPS) * rg_ref[layer:layer + 1, c0:c0 + 128]
            y = gate * (1.0 / (1.0 + jnp.exp(-gate))) * on
            mixed_ref[rows[j], ATTN_WIDTH + c0:ATTN_WIDTH + c0 + 128] = y.astype(BF16)
            if hd % 2 == 1:
                issue_chunk()

    assert not chunks


def _mix_kernel(layer, tiles_per_seq,
                xn_ref, xr_ref, ada_n_ref, ada_r_ref, g1_ref, win_f32_ref, gq_ref, gk_ref, sinks_ref,
                rg_ref, wout_f32_ref, bias_ref, ret_ref, cdec_ref, w1_f32_ref, w2_f32_ref,
                o_ref, w1_bf16_ref, w2_bf16_ref,
                proj_a, proj_b, mixed_a, mixed_b, hn_ref, kwin_a, kwin_b, vwin_a, vwin_b, s_ref,
                win_ref, wout_ref):
    i = pl.program_id(0)

    @pl.when(i == 0)
    def _():
        win_ref[...] = win_f32_ref[...].astype(BF16)
        wout_ref[...] = wout_f32_ref[...].astype(BF16)

    w1_bf16_ref[...] = w1_f32_ref[...].astype(BF16)
    w2_bf16_ref[...] = w2_f32_ref[...].astype(BF16)

    @pl.when(i == 0)
    def _():
        proj_b[...] = jnp.zeros_like(proj_b)
        mixed_a[...] = jnp.zeros_like(mixed_a)
        kwin_a[...] = jnp.zeros_like(kwin_a)
        kwin_b[...] = jnp.zeros_like(kwin_b)
        vwin_a[...] = jnp.zeros_like(vwin_a)
        vwin_b[...] = jnp.zeros_like(vwin_b)
        s_ref[...] = jnp.zeros_like(s_ref)

    consts = (layer, gq_ref, sinks_ref, rg_ref, bias_ref, ret_ref, cdec_ref, s_ref)

    def slot(half, proj_cur, proj_next, mixed_cur, mixed_prev, kwin, kwin_next, vwin, vwin_next,
             seq_start):
        def out_chunk(c):
            cols = slice(c * PROJ_CHUNK, min((c + 1) * PROJ_CHUNK, D_MODEL))
            y = jnp.dot(mixed_prev[...], wout_ref[:, cols], preferred_element_type=F32)
            o_ref[half, :, cols] = xr_ref[half, :, cols] + ada_r_ref[2:3, cols] * y

        def norm_next():
            x = xn_ref[half]
            ms = jnp.mean(x * x, axis=-1, keepdims=True)
            h = x * lax.rsqrt(ms + EPS) * g1_ref[layer:layer + 1, :]
            hn_ref[...] = (h * (1.0 + ada_n_ref[1:2, :]) + ada_n_ref[0:1, :]).astype(BF16)

        def in_chunk(c):
            cols = slice(c * PROJ_CHUNK, min((c + 1) * PROJ_CHUNK, IN_WIDTH))
            proj_next[:, cols] = jnp.dot(hn_ref[...], win_ref[:, cols], preferred_element_type=F32)
            if cols.start <= OFF_AK and OFF_AV + 2 * HEAD_DIM <= cols.stop:
                _build_windows(proj_next, _pair_gain(gk_ref, layer), kwin_next, kwin, vwin_next, vwin)

        def first_chunk():
            out_chunk(0)
            norm_next()
            out_chunk(1)

        chunks = [first_chunk]
        chunks += [functools.partial(out_chunk, c) for c in range(2, pl.cdiv(D_MODEL, PROJ_CHUNK))]
        chunks += [functools.partial(in_chunk, c) for c in range(pl.cdiv(IN_WIDTH, PROJ_CHUNK))]
        _mix_slot(chunks, proj_cur, mixed_cur, kwin, vwin, seq_start, *consts)

    slot(0, proj_b, proj_a, mixed_b, mixed_a, kwin_b, kwin_a, vwin_b, vwin_a, False)
    slot(1, proj_a, proj_b, mixed_a, mixed_b, kwin_a, kwin_b, vwin_a, vwin_b,
         (2 * i) % tiles_per_seq == 0)


def _mix_call(layer, x, ada, g1, w_in, gq, gk, sinks, rg, w_out, tables, w_mlp1, w_mlp2):
    b, s, d = x.shape
    tiles_per_seq = s // MIX_ROWS
    pairs_per_seq = tiles_per_seq // 2
    n_pairs = b * pairs_per_seq
    bias_tab, ret_tab, cdec = tables
    x_pairs = x.reshape(n_pairs, 2, MIX_ROWS, d)

    nxt = lambda i: jnp.minimum(i, n_pairs - 1)
    res = lambda i: jnp.maximum(i - 1, 0)
    const2 = lambda i: (0, 0)
    const4 = lambda i: (0, 0, 0, 0)
    single = pl.Buffered(1)
    smem = pl.BlockSpec(memory_space=pltpu.SMEM)
    slab1, slab2 = w_mlp1.shape[1] // n_pairs, w_mlp2.shape[1] // n_pairs
    lay3 = lambda i: (layer, 0, 0)
    out, w1_b, w2_b = pl.pallas_call(
        functools.partial(_mix_kernel, layer, tiles_per_seq),
        grid=(n_pairs + 1,),
        in_specs=[
            pl.BlockSpec((None, 2, MIX_ROWS, d), lambda i: (nxt(i), 0, 0, 0)),
            pl.BlockSpec((None, 2, MIX_ROWS, d), lambda i: (res(i), 0, 0, 0)),
            pl.BlockSpec((None, None, 6, d), lambda i: (layer, nxt(i) // pairs_per_seq, 0, 0)),
            pl.BlockSpec((None, None, 6, d), lambda i: (layer, res(i) // pairs_per_seq, 0, 0)),
            pl.BlockSpec(g1.shape, const2),
            pl.BlockSpec((None, d, IN_WIDTH), lay3, pipeline_mode=single),
            pl.BlockSpec(gq.shape, const2),
            pl.BlockSpec(gk.shape, const2),
            smem,
            pl.BlockSpec(rg.shape, const2),
            pl.BlockSpec((None, d, d), lay3, pipeline_mode=single),
            pl.BlockSpec(bias_tab.shape, const4, pipeline_mode=single),
            pl.BlockSpec(ret_tab.shape, const4, pipeline_mode=single),
            smem,
            pl.BlockSpec((None, slab1, D_FF), lambda i: (layer, nxt(i), 0)),
            pl.BlockSpec((None, slab2, d), lambda i: (layer, nxt(i), 0)),
        ],
        out_specs=[
            pl.BlockSpec((None, 2, MIX_ROWS, d), lambda i: (res(i), 0, 0, 0)),
            pl.BlockSpec((slab1, D_FF), lambda i: (nxt(i), 0)),
            pl.BlockSpec((slab2, d), lambda i: (nxt(i), 0)),
        ],
        out_shape=[
            jax.ShapeDtypeStruct((n_pairs, 2, MIX_ROWS, d), F32),
            jax.ShapeDtypeStruct(w_mlp1.shape[1:], BF16),
            jax.ShapeDtypeStruct(w_mlp2.shape[1:], BF16),
        ],
        scratch_shapes=[
            pltpu.VMEM((MIX_ROWS, IN_WIDTH), F32),
            pltpu.VMEM((MIX_ROWS, IN_WIDTH), F32),
            pltpu.VMEM((MIX_ROWS, d), BF16),
            pltpu.VMEM((MIX_ROWS, d), BF16),
            pltpu.VMEM((MIX_ROWS, d), BF16),
            pltpu.VMEM((N_KV_HEADS, BLOCK + MIX_ROWS, 2 * HEAD_DIM), BF16),
            pltpu.VMEM((N_KV_HEADS, BLOCK + MIX_ROWS, 2 * HEAD_DIM), BF16),
            pltpu.VMEM((2 * HEAD_DIM, BLOCK + MIX_ROWS), BF16),
            pltpu.VMEM((2 * HEAD_DIM, BLOCK + MIX_ROWS), BF16),
            pltpu.VMEM((RET_HEADS, 128, 128), F32),
            pltpu.VMEM((d, IN_WIDTH), BF16),
            pltpu.VMEM((d, d), BF16),
        ],
        compiler_params=pltpu.CompilerParams(
            dimension_semantics=("arbitrary",), vmem_limit_bytes=VMEM_LIMIT),
        name="mix",
    )(x_pairs, x_pairs, ada, ada, g1, w_in, gq, gk, sinks, rg, w_out, bias_tab, ret_tab, cdec,
      w_mlp1, w_mlp2)
    return out.reshape(b, s, d), w1_b, w2_b


MLP_ROWS = 1024


def _mlp_kernel(layer, x_ref, ada_ref, g2_ref, w1_ref, w2_ref, o_ref):
    x = x_ref[...]
    ms = jnp.mean(x * x, axis=-1, keepdims=True)
    h = x * lax.rsqrt(ms + EPS) * g2_ref[layer:layer + 1, :]
    h = (h * (1.0 + ada_ref[4:5, :]) + ada_ref[3:4, :]).astype(BF16)
    acc = None
    for c in range(D_FF // FF_CHUNK):
        cols = slice(c * FF_CHUNK, (c + 1) * FF_CHUNK)
        a = jnp.dot(h, w1_ref[:, cols], preferred_element_type=F32)
        a = jnp.maximum(a, 0.0)
        part = jnp.dot((a * a).astype(BF16), w2_ref[cols, :], preferred_element_type=F32)
        acc = part if acc is None else acc + part
    o_ref[...] = x + ada_ref[5:6, :] * acc


def _mlp_call(layer, x, ada, g2, w1, w2):
    b, s, d = x.shape
    const2 = lambda i, j: (0, 0)
    single = pl.Buffered(1)
    return pl.pallas_call(
        functools.partial(_mlp_kernel, layer),
        grid=(b, s // MLP_ROWS),
        in_specs=[
            pl.BlockSpec((None, MLP_ROWS, d), lambda i, j: (i, j, 0)),
            pl.BlockSpec((None, None, 6, d), lambda i, j: (layer, i, 0, 0)),
            pl.BlockSpec(g2.shape, const2),
            pl.BlockSpec((d, D_FF), const2, pipeline_mode=single),
            pl.BlockSpec((D_FF, d), const2, pipeline_mode=single),
        ],
        out_specs=pl.BlockSpec((None, MLP_ROWS, d), lambda i, j: (i, j, 0)),
        out_shape=jax.ShapeDtypeStruct((b, s, d), F32),
        compiler_params=pltpu.CompilerParams(
            dimension_semantics=("arbitrary", "arbitrary"), vmem_limit_bytes=VMEM_LIMIT),
        name="mlp",
    )(x, ada, g2, w1, w2)


def kernel(x, c, norm1_g, norm2_g, w_ada, b_ada, w_in, q_norm_g, k_norm_g, sinks, ret_norm_g,
           w_out, w_mlp1, w_mlp2):
    b = x.shape[0]
    tables = tuple(jnp.asarray(t) for t in _constant_tables())
    c_pad = jnp.pad(c, ((0, 8 - b), (0, 0)))
    ada = _ada_call(c_pad, w_ada, b_ada)[:, :b].reshape(DEPTH, b, 6, D_MODEL)
    for l in range(DEPTH):
        x, w1_b, w2_b = _mix_call(l, x, ada, norm1_g, w_in, q_norm_g, k_norm_g, sinks, ret_norm_g,
                                  w_out, tables, w_mlp1, w_mlp2)
        x = _mlp_call(l, x, ada, norm2_g, w1_b, w2_b)
    return x
```

```python
import functools

import numpy as np
import jax
import jax.numpy as jnp
from jax import lax
from jax.experimental import pallas as pl
from jax.experimental.pallas import tpu as pltpu

D_MODEL = 1024
DEPTH = 2
ATTN_WIDTH = 512
RET_WIDTH = 512
HEAD_DIM = 64
N_Q_HEADS = 8
N_KV_HEADS = 2
N_PAIRS = N_Q_HEADS // 2
BLOCK = 128
RET_HEADS = 4
RET_DK = 128
D_FF = 4 * D_MODEL
EPS = 1e-6
NEG_INF = -1e30
LOG2E = 1.4426950408889634
IN_WIDTH = 2816
OFF_AQ, OFF_AK, OFF_AV, OFF_RQ, OFF_RK, OFF_RV, OFF_RG = 0, 512, 640, 768, 1280, 1792, 2304

MIX_ROWS = 256
PROJ_CHUNK = 256
FF_CHUNK = 1024
ADA_COLS = 1536
VMEM_LIMIT = 60 * 1024 * 1024

F32 = jnp.float32
BF16 = jnp.bfloat16


@functools.lru_cache(maxsize=None)
def _constant_tables():
    q_pos = np.arange(BLOCK)[:, None]
    k_pos = np.arange(2 * BLOCK)[None, :]
    dist = q_pos + BLOCK - k_pos
    valid = (dist >= 0) & (dist < BLOCK)
    valid_first = valid & (k_pos >= BLOCK)
    slopes = np.exp2(-8.0 * np.arange(1, N_Q_HEADS + 1, dtype=np.float64) / N_Q_HEADS)
    bias = -slopes[:, None, None] * dist[None].astype(np.float64)
    bias_tab = np.stack([np.where(v[None], bias * LOG2E, NEG_INF) for v in (valid_first, valid)])
    bias_tab = np.ascontiguousarray(bias_tab.transpose(0, 1, 3, 2), np.float32)

    log_gamma = np.log1p(-np.exp2(-5.0 - np.arange(RET_HEADS, dtype=np.float64)))
    idx = np.arange(BLOCK, dtype=np.float64)
    rel = idx[:, None] - idx[None, :]
    k_scale = RET_DK ** -0.5
    decay_in = np.where(rel >= 0, np.exp(log_gamma[:, None, None] * np.maximum(rel, 0.0)), 0.0) * k_scale
    q_decay = np.exp(log_gamma[:, None] * (idx[None, :] + 1.0))
    k_decay = np.exp(log_gamma[:, None] * (BLOCK - 1.0 - idx[None, :])) * k_scale
    q_decay = np.broadcast_to(q_decay[:, :, None], (RET_HEADS, BLOCK, BLOCK))
    k_decay = np.broadcast_to(k_decay[:, :, None], (RET_HEADS, BLOCK, BLOCK))
    chunk_decay = np.exp(log_gamma * BLOCK)
    ret_tab = np.stack([decay_in, q_decay, k_decay]).astype(np.float32)
    return bias_tab, ret_tab, chunk_decay.astype(np.float32)


def _ada_kernel(c_ref, w_ref, b_ref, o_ref):
    c = c_ref[...]
    c_act = c * (1.0 / (1.0 + jnp.exp(-c)))
    o_ref[...] = jnp.dot(c_act.astype(BF16), w_ref[...].astype(BF16),
                         preferred_element_type=F32) + b_ref[...]


def _ada_call(c, w_ada, b_ada):
    rows = c.shape[0]
    n = w_ada.shape[-1]
    return pl.pallas_call(
        _ada_kernel,
        grid=(DEPTH, n // ADA_COLS),
        in_specs=[
            pl.BlockSpec((rows, D_MODEL), lambda l, j: (0, 0)),
            pl.BlockSpec((None, D_MODEL, ADA_COLS), lambda l, j: (l, 0, j)),
            pl.BlockSpec((None, 1, ADA_COLS), lambda l, j: (l, 0, j)),
        ],
        out_specs=pl.BlockSpec((None, rows, ADA_COLS), lambda l, j: (l, 0, j)),
        out_shape=jax.ShapeDtypeStruct((DEPTH, rows, n), F32),
        compiler_params=pltpu.CompilerParams(
            dimension_semantics=("arbitrary", "arbitrary"), vmem_limit_bytes=VMEM_LIMIT),
        name="ada",
    )(c, w_ada, b_ada.reshape(DEPTH, 1, n))


def _head_norm(a, gain_row):
    lo = lax.broadcasted_iota(jnp.int32, a.shape, 1) < HEAD_DIM
    a2 = a * a
    s_lo = jnp.sum(jnp.where(lo, a2, 0.0), axis=-1, keepdims=True)
    s_hi = jnp.sum(jnp.where(lo, 0.0, a2), axis=-1, keepdims=True)
    mean_sq = jnp.where(lo, s_lo, s_hi) * (1.0 / HEAD_DIM)
    return a * lax.rsqrt(mean_sq + EPS) * gain_row


def _pair_gain(g_ref, layer):
    row = g_ref[layer:layer + 1, :]
    return jnp.concatenate([row, row], axis=1)


def _build_windows(proj_ref, gk_row, kwin_ref, kwin_prev, vwin_ref, vwin_prev):
    kn = _head_norm(proj_ref[:, OFF_AK:OFF_AK + 128], gk_row)
    kn_rot = pltpu.roll(kn, HEAD_DIM, axis=1)
    lo = lax.broadcasted_iota(jnp.int32, kn.shape, 1) < HEAD_DIM
    k_dup = (jnp.where(lo, kn, kn_rot), jnp.where(lo, kn_rot, kn))
    for g in range(N_KV_HEADS):
        kwin_ref[g, 0:BLOCK, :] = kwin_prev[g, MIX_ROWS:MIX_ROWS + BLOCK, :]
        kwin_ref[g, BLOCK:BLOCK + MIX_ROWS, :] = k_dup[g].astype(BF16)
    vwin_ref[:, 0:BLOCK] = vwin_prev[:, MIX_ROWS:MIX_ROWS + BLOCK]
    vwin_ref[:, BLOCK:BLOCK + MIX_ROWS] = proj_ref[:, OFF_AV:OFF_AV + 128].T.astype(BF16)


def _mix_slot(chunks, proj_ref, mixed_ref, kwin_ref, vwin_ref,
              seq_start, layer, gq_ref, sinks_ref, rg_ref,
              bias_ref, ret_ref, cdec_ref, s_ref):
    first_tab = 1 if seq_start is False else jnp.where(seq_start, 0, 1)
    chunks = list(chunks)
    n_chunks = len(chunks)
    n_points = 5 + 5 * (MIX_ROWS // BLOCK)
    point = [0]

    def issue_chunk():
        point[0] += 1
        while chunks and (n_chunks - len(chunks)) * n_points < point[0] * n_chunks:
            chunks.pop(0)()

    issue_chunk()

    lo = lax.broadcasted_iota(jnp.int32, (BLOCK, 2 * HEAD_DIM), 1) < HEAD_DIM
    gq = _pair_gain(gq_ref, layer) * (HEAD_DIM ** -0.5 * LOG2E)
    n_blocks = MIX_ROWS // BLOCK
    rows = [pl.ds(j * BLOCK, BLOCK) for j in range(n_blocks)]
    wins = [slice(j * BLOCK, (j + 2) * BLOCK) for j in range(n_blocks)]
    tabs = [first_tab if j == 0 else 1 for j in range(n_blocks)]

    s_t = {}
    for j in range(n_blocks):
        for g in range(N_KV_HEADS):
            qs = []
            for pp in range(2):
                p = 2 * g + pp
                qn = _head_norm(proj_ref[rows[j], OFF_AQ + 128 * p:OFF_AQ + 128 * p + 128], gq)
                qs.append(jnp.where(lo, qn, 0.0).astype(BF16))
                qs.append(jnp.where(lo, 0.0, qn).astype(BF16))
            s_t[j, g] = lax.dot_general(kwin_ref[g, wins[j], :], jnp.concatenate(qs, axis=0),
                                        (((1,), (1,)), ((), ())),
                                        preferred_element_type=F32)
    issue_chunk()
    sinks = [sinks_ref[layer, head] * LOG2E for head in range(N_Q_HEADS)]
    es, ms = {}, {}
    for j in range(n_blocks):
        for head in range(N_Q_HEADS):
            g, hh = divmod(head, 4)
            sh = s_t[j, g][:, 128 * hh:128 * hh + 128] + bias_ref[tabs[j], head]
            ms[j, head] = jnp.maximum(jnp.max(sh, axis=0, keepdims=True), sinks[head])
            es[j, head] = jnp.exp2(sh - ms[j, head]).astype(BF16)
            if head % 4 == 3:
                issue_chunk()
    ones = jnp.ones((HEAD_DIM, 2 * BLOCK), BF16)
    o_t = {}
    for j in range(n_blocks):
        for g in range(N_KV_HEADS):
            lhs = jnp.concatenate([vwin_ref[64 * g:64 * g + 64, wins[j]], ones], axis=0)
            p_t = jnp.concatenate([es[j, 4 * g + hh] for hh in range(4)], axis=1)
            o_t[j, g] = jnp.dot(lhs, p_t, preferred_element_type=F32)
    issue_chunk()
    for j in range(n_blocks):
        for p in range(N_PAIRS):
            g, pp = divmod(p, 2)
            halves = []
            for head in (2 * p, 2 * p + 1):
                cols = slice(128 * (head % 4), 128 * (head % 4) + 128)
                denom = o_t[j, g][HEAD_DIM:, cols] + jnp.exp2(sinks[head] - ms[j, head])
                halves.append(o_t[j, g][:HEAD_DIM, cols] / denom)
            o = jnp.concatenate(halves, axis=0).T
            mixed_ref[rows[j], 128 * p:128 * p + 128] = o.astype(BF16)
        issue_chunk()

    heads = {}
    for j in range(n_blocks):
        for hd in range(RET_HEADS):
            c0 = 128 * hd
            q_f = proj_ref[rows[j], OFF_RQ + c0:OFF_RQ + c0 + 128]
            k = proj_ref[rows[j], OFF_RK + c0:OFF_RK + c0 + 128]
            v = proj_ref[rows[j], OFF_RV + c0:OFF_RV + c0 + 128].astype(BF16)
            inner = lax.dot_general(q_f.astype(BF16), k.astype(BF16), (((1,), (1,)), ((), ())),
                                    preferred_element_type=F32)
            kd = (k * ret_ref[2, hd]).astype(BF16)
            kv = lax.dot_general(kd, v, (((0,), (0,)), ((), ())), preferred_element_type=F32)
            heads[j, hd] = (q_f, v, inner, kv)
    issue_chunk()
    outs = {}
    for j in range(n_blocks):
        for hd in range(RET_HEADS):
            q_f, v, inner, kv = heads[j, hd]
            state = s_ref[hd]
            if j == 0 and seq_start is not False:
                state = jnp.where(seq_start, 0.0, state)
            lhs = jnp.concatenate([(inner * ret_ref[0, hd]).astype(BF16), (q_f * ret_ref[1, hd]).astype(BF16)],
                                  axis=1)
            outs[j, hd] = jnp.dot(lhs, jnp.concatenate([v, state.astype(BF16)], axis=0),
                                  preferred_element_type=F32)
            s_ref[hd] = cdec_ref[hd] * state + kv
    issue_chunk()
    for j in range(n_blocks):
        for hd in range(RET_HEADS):
            c0 = 128 * hd
            o = outs[j, hd]
            gate = proj_ref[rows[j], OFF_RG + c0:OFF_RG + c0 + 128]
            mu = jnp.mean(o, axis=-1, keepdims=True)
            d = o - mu
            var = jnp.mean(d * d, axis=-1, keepdims=True)
            on = d * lax.rsqrt(var + EPS) * rg_ref[layer:layer + 1, c0:c0 + 128]
            y = gate * (1.0 / (1.0 + jnp.exp(-gate))) * on
            mixed_ref[rows[j], ATTN_WIDTH + c0:ATTN_WIDTH + c0 + 128] = y.astype(BF16)
            if hd % 2 == 1:
                issue_chunk()

    assert not chunks


def _mix_kernel(layer, tiles_per_seq,
                xn_ref, xr_ref, ada_n_ref, ada_r_ref, g1_ref, win_f32_ref, gq_ref, gk_ref, sinks_ref,
                rg_ref, wout_f32_ref, bias_ref, ret_ref, cdec_ref, w1_f32_ref, w2_f32_ref,
                o_ref, w1_bf16_ref, w2_bf16_ref,
                proj_a, proj_b, mixed_a, mixed_b, hn_ref, kwin_a, kwin_b, vwin_a, vwin_b, s_ref,
                win_ref, wout_ref):
    i = pl.program_id(0)

    @pl.when(i == 0)
    def _():
        win_ref[...] = win_f32_ref[...].astype(BF16)
        wout_ref[...] = wout_f32_ref[...].astype(BF16)

    w1_bf16_ref[...] = w1_f32_ref[...].astype(BF16)
    w2_bf16_ref[...] = w2_f32_ref[...].astype(BF16)

    @pl.when(i == 0)
    def _():
        proj_b[...] = jnp.zeros_like(proj_b)
        mixed_a[...] = jnp.zeros_like(mixed_a)
        kwin_a[...] = jnp.zeros_like(kwin_a)
        kwin_b[...] = jnp.zeros_like(kwin_b)
        vwin_a[...] = jnp.zeros_like(vwin_a)
        vwin_b[...] = jnp.zeros_like(vwin_b)
        s_ref[...] = jnp.zeros_like(s_ref)

    consts = (layer, gq_ref, sinks_ref, rg_ref, bias_ref, ret_ref, cdec_ref, s_ref)

    def slot(half, proj_cur, proj_next, mixed_cur, mixed_prev, kwin, kwin_next, vwin, vwin_next,
             seq_start):
        def out_chunk(c):
            cols = slice(c * PROJ_CHUNK, min((c + 1) * PROJ_CHUNK, D_MODEL))
            y = jnp.dot(mixed_prev[...], wout_ref[:, cols], preferred_element_type=F32)
            o_ref[half, :, cols] = xr_ref[half, :, cols] + ada_r_ref[2:3, cols] * y

        def norm_next():
            x = xn_ref[half]
            ms = jnp.mean(x * x, axis=-1, keepdims=True)
            h = x * lax.rsqrt(ms + EPS) * g1_ref[layer:layer + 1, :]
            hn_ref[...] = (h * (1.0 + ada_n_ref[1:2, :]) + ada_n_ref[0:1, :]).astype(BF16)

        def in_chunk(c):
            cols = slice(c * PROJ_CHUNK, min((c + 1) * PROJ_CHUNK, IN_WIDTH))
            proj_next[:, cols] = jnp.dot(hn_ref[...], win_ref[:, cols], preferred_element_type=F32)
            if cols.start <= OFF_AK and OFF_AV + 2 * HEAD_DIM <= cols.stop:
                _build_windows(proj_next, _pair_gain(gk_ref, layer), kwin_next, kwin, vwin_next, vwin)

        def first_chunk():
            out_chunk(0)
            norm_next()
            out_chunk(1)

        chunks = [first_chunk]
        chunks += [functools.partial(out_chunk, c) for c in range(2, pl.cdiv(D_MODEL, PROJ_CHUNK))]
        chunks += [functools.partial(in_chunk, c) for c in range(pl.cdiv(IN_WIDTH, PROJ_CHUNK))]
        _mix_slot(chunks, proj_cur, mixed_cur, kwin, vwin, seq_start, *consts)

    slot(0, proj_b, proj_a, mixed_b, mixed_a, kwin_b, kwin_a, vwin_b, vwin_a, False)
    slot(1, proj_a, proj_b, mixed_a, mixed_b, kwin_a, kwin_b, vwin_a, vwin_b,
         (2 * i) % tiles_per_seq == 0)


def _mix_call(layer, x, ada, g1, w_in, gq, gk, sinks, rg, w_out, tables, w_mlp1, w_mlp2):
    b, s, d = x.shape
    tiles_per_seq = s // MIX_ROWS
    pairs_per_seq = tiles_per_seq // 2
    n_pairs = b * pairs_per_seq
    bias_tab, ret_tab, cdec = tables
    x_pairs = x.reshape(n_pairs, 2, MIX_ROWS, d)

    nxt = lambda i: jnp.minimum(i, n_pairs - 1)
    res = lambda i: jnp.maximum(i - 1, 0)
    const2 = lambda i: (0, 0)
    const4 = lambda i: (0, 0, 0, 0)
    single = pl.Buffered(1)
    smem = pl.BlockSpec(memory_space=pltpu.SMEM)
    slab1, slab2 = w_mlp1.shape[1] // n_pairs, w_mlp2.shape[1] // n_pairs
    lay3 = lambda i: (layer, 0, 0)
    out, w1_b, w2_b = pl.pallas_call(
        functools.partial(_mix_kernel, layer, tiles_per_seq),
        grid=(n_pairs + 1,),
        in_specs=[
            pl.BlockSpec((None, 2, MIX_ROWS, d), lambda i: (nxt(i), 0, 0, 0)),
            pl.BlockSpec((None, 2, MIX_ROWS, d), lambda i: (res(i), 0, 0, 0)),
            pl.BlockSpec((None, None, 6, d), lambda i: (layer, nxt(i) // pairs_per_seq, 0, 0)),
            pl.BlockSpec((None, None, 6, d), lambda i: (layer, res(i) // pairs_per_seq, 0, 0)),
            pl.BlockSpec(g1.shape, const2),
            pl.BlockSpec((None, d, IN_WIDTH), lay3, pipeline_mode=single),
            pl.BlockSpec(gq.shape, const2),
            pl.BlockSpec(gk.shape, const2),
            smem,
            pl.BlockSpec(rg.shape, const2),
            pl.BlockSpec((None, d, d), lay3, pipeline_mode=single),
            pl.BlockSpec(bias_tab.shape, const4, pipeline_mode=single),
            pl.BlockSpec(ret_tab.shape, const4, pipeline_mode=single),
            smem,
            pl.BlockSpec((None, slab1, D_FF), lambda i: (layer, nxt(i), 0)),
            pl.BlockSpec((None, slab2, d), lambda i: (layer, nxt(i), 0)),
        ],
        out_specs=[
            pl.BlockSpec((None, 2, MIX_ROWS, d), lambda i: (res(i), 0, 0, 0)),
            pl.BlockSpec((slab1, D_FF), lambda i: (nxt(i), 0)),
            pl.BlockSpec((slab2, d), lambda i: (nxt(i), 0)),
        ],
        out_shape=[
            jax.ShapeDtypeStruct((n_pairs, 2, MIX_ROWS, d), F32),
            jax.ShapeDtypeStruct(w_mlp1.shape[1:], BF16),
            jax.ShapeDtypeStruct(w_mlp2.shape[1:], BF16),
        ],
        scratch_shapes=[
            pltpu.VMEM((MIX_ROWS, IN_WIDTH), F32),
            pltpu.VMEM((MIX_ROWS, IN_WIDTH), F32),
            pltpu.VMEM((MIX_ROWS, d), BF16),
            pltpu.VMEM((MIX_ROWS, d), BF16),
            pltpu.VMEM((MIX_ROWS, d), BF16),
            pltpu.VMEM((N_KV_HEADS, BLOCK + MIX_ROWS, 2 * HEAD_DIM), BF16),
            pltpu.VMEM((N_KV_HEADS, BLOCK + MIX_ROWS, 2 * HEAD_DIM), BF16),
            pltpu.VMEM((2 * HEAD_DIM, BLOCK + MIX_ROWS), BF16),
            pltpu.VMEM((2 * HEAD_DIM, BLOCK + MIX_ROWS), BF16),
            pltpu.VMEM((RET_HEADS, 128, 128), F32),
            pltpu.VMEM((d, IN_WIDTH), BF16),
            pltpu.VMEM((d, d), BF16),
        ],
        compiler_params=pltpu.CompilerParams(
            dimension_semantics=("arbitrary",), vmem_limit_bytes=VMEM_LIMIT),
        name="mix",
    )(x_pairs, x_pairs, ada, ada, g1, w_in, gq, gk, sinks, rg, w_out, bias_tab, ret_tab, cdec,
      w_mlp1, w_mlp2)
    return out.reshape(b, s, d), w1_b, w2_b


MLP_ROWS = 1024


def _mlp_kernel(layer, x_ref, ada_ref, g2_ref, w1_ref, w2_ref, o_ref):
    x = x_ref[...]
    ms = jnp.mean(x * x, axis=-1, keepdims=True)
    h = x * lax.rsqrt(ms + EPS) * g2_ref[layer:layer + 1, :]
    h = (h * (1.0 + ada_ref[4:5, :]) + ada_ref[3:4, :]).astype(BF16)
    acc = None
    for c in range(D_FF // FF_CHUNK):
        cols = slice(c * FF_CHUNK, (c + 1) * FF_CHUNK)
        a = jnp.dot(h, w1_ref[:, cols], preferred_element_type=F32)
        a = jnp.maximum(a, 0.0)
        part = jnp.dot((a * a).astype(BF16), w2_ref[cols, :], preferred_element_type=F32)
        acc = part if acc is None else acc + part
    o_ref[...] = x + ada_ref[5:6, :] * acc


def _mlp_call(layer, x, ada, g2, w1, w2):
    b, s, d = x.shape
    const2 = lambda i, j: (0, 0)
    single = pl.Buffered(1)
    return pl.pallas_call(
        functools.partial(_mlp_kernel, layer),
        grid=(b, s // MLP_ROWS),
        in_specs=[
            pl.BlockSpec((None, MLP_ROWS, d), lambda i, j: (i, j, 0)),
            pl.BlockSpec((None, None, 6, d), lambda i, j: (layer, i, 0, 0)),
            pl.BlockSpec(g2.shape, const2),
            pl.BlockSpec((d, D_FF), const2, pipeline_mode=single),
            pl.BlockSpec((D_FF, d), const2, pipeline_mode=single),
        ],
        out_specs=pl.BlockSpec((None, MLP_ROWS, d), lambda i, j: (i, j, 0)),
        out_shape=jax.ShapeDtypeStruct((b, s, d), F32),
        compiler_params=pltpu.CompilerParams(
            dimension_semantics=("arbitrary", "arbitrary"), vmem_limit_bytes=VMEM_LIMIT),
        name="mlp",
    )(x, ada, g2, w1, w2)


def kernel(x, c, norm1_g, norm2_g, w_ada, b_ada, w_in, q_norm_g, k_norm_g, sinks, ret_norm_g,
           w_out, w_mlp1, w_mlp2):
    b = x.shape[0]
    tables = tuple(jnp.asarray(t) for t in _constant_tables())
    ada = _ada_call(c, w_ada, b_ada).reshape(DEPTH, b, 6, D_MODEL)
    for l in range(DEPTH):
        x, w1_b, w2_b = _mix_call(l, x, ada, norm1_g, w_in, q_norm_g, k_norm_g, sinks, ret_norm_g,
                                  w_out, tables, w_mlp1, w_mlp2)
        x = _mlp_call(l, x, ada, norm2_g, w1_b, w2_b)
    return x
```

```python
import functools

import numpy as np
import jax
import jax.numpy as jnp
from jax import lax
from jax.experimental import pallas as pl
from jax.experimental.pallas import tpu as pltpu

D_MODEL = 1024
DEPTH = 2
ATTN_WIDTH = 512
RET_WIDTH = 512
HEAD_DIM = 64
N_Q_HEADS = 8
N_KV_HEADS = 2
N_PAIRS = N_Q_HEADS // 2
BLOCK = 128
RET_HEADS = 4
RET_DK = 128
D_FF = 4 * D_MODEL
EPS = 1e-6
NEG_INF = -1e30
LOG2E = 1.4426950408889634
IN_WIDTH = 2816
OFF_AQ, OFF_AK, OFF_AV, OFF_RQ, OFF_RK, OFF_RV, OFF_RG = 0, 512, 640, 768, 1280, 1792, 2304

MIX_ROWS = 256
PROJ_CHUNK = 256
FF_CHUNK = 1024
ADA_COLS = 1536
VMEM_LIMIT = 60 * 1024 * 1024

F32 = jnp.float32
BF16 = jnp.bfloat16


@functools.lru_cache(maxsize=None)
def _constant_tables():
    q_pos = np.arange(BLOCK)[:, None]
    k_pos = np.arange(2 * BLOCK)[None, :]
    dist = q_pos + BLOCK - k_pos
    valid = (dist >= 0) & (dist < BLOCK)
    valid_first = valid & (k_pos >= BLOCK)
    slopes = np.exp2(-8.0 * np.arange(1, N_Q_HEADS + 1, dtype=np.float64) / N_Q_HEADS)
    bias = -slopes[:, None, None] * dist[None].astype(np.float64)
    bias_tab = np.stack([np.where(v[None], bias * LOG2E, NEG_INF) for v in (valid_first, valid)])
    bias_tab = np.ascontiguousarray(bias_tab.transpose(0, 1, 3, 2), np.float32)

    log_gamma = np.log1p(-np.exp2(-5.0 - np.arange(RET_HEADS, dtype=np.float64)))
    idx = np.arange(BLOCK, dtype=np.float64)
    rel = idx[:, None] - idx[None, :]
    k_scale = RET_DK ** -0.5
    decay_in = np.where(rel >= 0, np.exp(log_gamma[:, None, None] * np.maximum(rel, 0.0)), 0.0) * k_scale
    q_decay = np.exp(log_gamma[:, None] * (idx[None, :] + 1.0))
    k_decay = np.exp(log_gamma[:, None] * (BLOCK - 1.0 - idx[None, :])) * k_scale
    q_decay = np.broadcast_to(q_decay[:, :, None], (RET_HEADS, BLOCK, BLOCK))
    k_decay = np.broadcast_to(k_decay[:, :, None], (RET_HEADS, BLOCK, BLOCK))
    chunk_decay = np.exp(log_gamma * BLOCK)
    ret_tab = np.stack([decay_in, q_decay, k_decay]).astype(np.float32)
    return bias_tab, ret_tab, chunk_decay.astype(np.float32)


def _ada_kernel(c_ref, w_ref, b_ref, o_ref):
    c = c_ref[...]
    c_act = c * (1.0 / (1.0 + jnp.exp(-c)))
    o_ref[...] = jnp.dot(c_act.astype(BF16), w_ref[...].astype(BF16),
                         preferred_element_type=F32) + b_ref[...]


def _ada_call(c, w_ada, b_ada):
    rows = c.shape[0]
    n = w_ada.shape[-1]
    return pl.pallas_call(
        _ada_kernel,
        grid=(DEPTH, n // ADA_COLS),
        in_specs=[
            pl.BlockSpec((rows, D_MODEL), lambda l, j: (0, 0)),
            pl.BlockSpec((None, D_MODEL, ADA_COLS), lambda l, j: (l, 0, j)),
            pl.BlockSpec((None, 1, ADA_COLS), lambda l, j: (l, 0, j)),
        ],
        out_specs=pl.BlockSpec((None, rows, ADA_COLS), lambda l, j: (l, 0, j)),
        out_shape=jax.ShapeDtypeStruct((DEPTH, rows, n), F32),
        compiler_params=pltpu.CompilerParams(
            dimension_semantics=("arbitrary", "arbitrary"), vmem_limit_bytes=VMEM_LIMIT),
        name="ada",
    )(c, w_ada, b_ada.reshape(DEPTH, 1, n))


def _head_norm(a, gain_row):
    lo = lax.broadcasted_iota(jnp.int32, a.shape, 1) < HEAD_DIM
    a2 = a * a
    s_lo = jnp.sum(jnp.where(lo, a2, 0.0), axis=-1, keepdims=True)
    s_hi = jnp.sum(jnp.where(lo, 0.0, a2), axis=-1, keepdims=True)
    mean_sq = jnp.where(lo, s_lo, s_hi) * (1.0 / HEAD_DIM)
    return a * lax.rsqrt(mean_sq + EPS) * gain_row


def _pair_gain(g_ref, layer):
    row = g_ref[layer:layer + 1, :]
    return jnp.concatenate([row, row], axis=1)


def _build_windows(proj_ref, gk_row, kwin_ref, kwin_prev, vwin_ref, vwin_prev):
    kn = _head_norm(proj_ref[:, OFF_AK:OFF_AK + 128], gk_row)
    kn_rot = pltpu.roll(kn, HEAD_DIM, axis=1)
    lo = lax.broadcasted_iota(jnp.int32, kn.shape, 1) < HEAD_DIM
    k_dup = (jnp.where(lo, kn, kn_rot), jnp.where(lo, kn_rot, kn))
    for g in range(N_KV_HEADS):
        kwin_ref[g, 0:BLOCK, :] = kwin_prev[g, MIX_ROWS:MIX_ROWS + BLOCK, :]
        kwin_ref[g, BLOCK:BLOCK + MIX_ROWS, :] = k_dup[g].astype(BF16)
    vwin_ref[:, 0:BLOCK] = vwin_prev[:, MIX_ROWS:MIX_ROWS + BLOCK]
    vwin_ref[:, BLOCK:BLOCK + MIX_ROWS] = proj_ref[:, OFF_AV:OFF_AV + 128].T.astype(BF16)


def _mix_slot(chunks, proj_ref, mixed_ref, kwin_ref, vwin_ref,
              seq_start, layer, gq_ref, sinks_ref, rg_ref,
              bias_ref, ret_ref, cdec_ref, s_ref):
    first_tab = 1 if seq_start is False else jnp.where(seq_start, 0, 1)
    chunks = list(chunks)
    n_chunks = len(chunks)
    n_points = 5 + 5 * (MIX_ROWS // BLOCK)
    point = [0]

    def issue_chunk():
        point[0] += 1
        while chunks and (n_chunks - len(chunks)) * n_points < point[0] * n_chunks:
            chunks.pop(0)()

    issue_chunk()

    lo = lax.broadcasted_iota(jnp.int32, (BLOCK, 2 * HEAD_DIM), 1) < HEAD_DIM
    gq = _pair_gain(gq_ref, layer) * (HEAD_DIM ** -0.5 * LOG2E)
    n_blocks = MIX_ROWS // BLOCK
    rows = [pl.ds(j * BLOCK, BLOCK) for j in range(n_blocks)]
    wins = [slice(j * BLOCK, (j + 2) * BLOCK) for j in range(n_blocks)]
    tabs = [first_tab if j == 0 else 1 for j in range(n_blocks)]

    s_t = {}
    for j in range(n_blocks):
        for g in range(N_KV_HEADS):
            qs = []
            for pp in range(2):
                p = 2 * g + pp
                qn = _head_norm(proj_ref[rows[j], OFF_AQ + 128 * p:OFF_AQ + 128 * p + 128], gq)
                qs.append(jnp.where(lo, qn, 0.0).astype(BF16))
                qs.append(jnp.where(lo, 0.0, qn).astype(BF16))
            s_t[j, g] = lax.dot_general(kwin_ref[g, wins[j], :], jnp.concatenate(qs, axis=0),
                                        (((1,), (1,)), ((), ())),
                                        preferred_element_type=F32)
    issue_chunk()
    sinks = [sinks_ref[layer, head] * LOG2E for head in range(N_Q_HEADS)]
    es, ms = {}, {}
    for j in range(n_blocks):
        for head in range(N_Q_HEADS):
            g, hh = divmod(head, 4)
            sh = s_t[j, g][:, 128 * hh:128 * hh + 128] + bias_ref[tabs[j], head]
            ms[j, head] = jnp.maximum(jnp.max(sh, axis=0, keepdims=True), sinks[head])
            es[j, head] = jnp.exp2(sh - ms[j, head]).astype(BF16)
            if head % 4 == 3:
                issue_chunk()
    ones = jnp.ones((HEAD_DIM, 2 * BLOCK), BF16)
    o_t = {}
    for j in range(n_blocks):
        for g in range(N_KV_HEADS):
            lhs = jnp.concatenate([vwin_ref[64 * g:64 * g + 64, wins[j]], ones], axis=0)
            p_t = jnp.concatenate([es[j, 4 * g + hh] for hh in range(4)], axis=1)
            o_t[j, g] = jnp.dot(lhs, p_t, preferred_element_type=F32)
    issue_chunk()
    for j in range(n_blocks):
        for p in range(N_PAIRS):
            g, pp = divmod(p, 2)
            halves = []
            for head in (2 * p, 2 * p + 1):
                cols = slice(128 * (head % 4), 128 * (head % 4) + 128)
                denom = o_t[j, g][HEAD_DIM:, cols] + jnp.exp2(sinks[head] - ms[j, head])
                halves.append(o_t[j, g][:HEAD_DIM, cols] / denom)
            o = jnp.concatenate(halves, axis=0).T
            mixed_ref[rows[j], 128 * p:128 * p + 128] = o.astype(BF16)
        issue_chunk()

    heads = {}
    for j in range(n_blocks):
        for hd in range(RET_HEADS):
            c0 = 128 * hd
            q_f = proj_ref[rows[j], OFF_RQ + c0:OFF_RQ + c0 + 128]
            k = proj_ref[rows[j], OFF_RK + c0:OFF_RK + c0 + 128]
            v = proj_ref[rows[j], OFF_RV + c0:OFF_RV + c0 + 128].astype(BF16)
            inner = lax.dot_general(q_f.astype(BF16), k.astype(BF16), (((1,), (1,)), ((), ())),
                                    preferred_element_type=F32)
            kd = (k * ret_ref[2, hd]).astype(BF16)
            kv = lax.dot_general(kd, v, (((0,), (0,)), ((), ())), preferred_element_type=F32)
            heads[j, hd] = (q_f, v, inner, kv)
    issue_chunk()
    outs = {}
    for j in range(n_blocks):
        for hd in range(RET_HEADS):
            q_f, v, inner, kv = heads[j, hd]
            state = s_ref[hd]
            if j == 0 and seq_start is not False:
                state = jnp.where(seq_start, 0.0, state)
            lhs = jnp.concatenate([(inner * ret_ref[0, hd]).astype(BF16), (q_f * ret_ref[1, hd]).astype(BF16)],
                                  axis=1)
            outs[j, hd] = jnp.dot(lhs, jnp.concatenate([v, state.astype(BF16)], axis=0),
                                  preferred_element_type=F32)
            s_ref[hd] = cdec_ref[hd] * state + kv
    issue_chunk()
    for j in range(n_blocks):
        for hd in range(RET_HEADS):
            c0 = 128 * hd
            o = outs[j, hd]
            gate = proj_ref[rows[j], OFF_RG + c0:OFF_RG + c0 + 128]
            mu = jnp.mean(o, axis=-1, keepdims=True)
            d = o - mu
            var = jnp.mean(d * d, axis=-1, keepdims=True)
            on = d * lax.rsqrt(var + EPS) * rg_ref[layer:layer + 1, c0:c0 + 128]
            y = gate * (1.0 / (1.0 + jnp.exp(-gate))) * on
            mixed_ref[rows[j], ATTN_WIDTH + c0:ATTN_WIDTH + c0 + 128] = y.astype(BF16)
            if hd % 2 == 1:
                issue_chunk()

    assert not chunks


def _mix_kernel(layer, tiles_per_seq,
                xn_ref, xr_ref, ada_n_ref, ada_r_ref, g1_ref, win_f32_ref, gq_ref, gk_ref, sinks_ref,
                rg_ref, wout_f32_ref, bias_ref, ret_ref, cdec_ref, w1_f32_ref, w2_f32_ref,
                o_ref, w1_bf16_ref, w2_bf16_ref,
                proj_a, proj_b, mixed_a, mixed_b, hn_ref, kwin_a, kwin_b, vwin_a, vwin_b, s_ref,
                win_ref, wout_ref):
    i = pl.program_id(0)

    @pl.when(i == 0)
    def _():
        win_ref[...] = win_f32_ref[...].astype(BF16)
        wout_ref[...] = wout_f32_ref[...].astype(BF16)

    @pl.when(i == 0)
    def _():
        proj_b[...] = jnp.zeros_like(proj_b)
        mixed_a[...] = jnp.zeros_like(mixed_a)
        kwin_a[...] = jnp.zeros_like(kwin_a)
        kwin_b[...] = jnp.zeros_like(kwin_b)
        vwin_a[...] = jnp.zeros_like(vwin_a)
        vwin_b[...] = jnp.zeros_like(vwin_b)
        s_ref[...] = jnp.zeros_like(s_ref)

    consts = (layer, gq_ref, sinks_ref, rg_ref, bias_ref, ret_ref, cdec_ref, s_ref)

    def slot(half, proj_cur, proj_next, mixed_cur, mixed_prev, kwin, kwin_next, vwin, vwin_next,
             seq_start):
        def out_chunk(c):
            cols = slice(c * PROJ_CHUNK, min((c + 1) * PROJ_CHUNK, D_MODEL))
            y = jnp.dot(mixed_prev[...], wout_ref[:, cols], preferred_element_type=F32)
            o_ref[half, :, cols] = xr_ref[half, :, cols] + ada_r_ref[2:3, cols] * y

        def norm_next():
            x = xn_ref[half]
            ms = jnp.mean(x * x, axis=-1, keepdims=True)
            h = x * lax.rsqrt(ms + EPS) * g1_ref[layer:layer + 1, :]
            hn_ref[...] = (h * (1.0 + ada_n_ref[1:2, :]) + ada_n_ref[0:1, :]).astype(BF16)

        def in_chunk(c):
            cols = slice(c * PROJ_CHUNK, min((c + 1) * PROJ_CHUNK, IN_WIDTH))
            proj_next[:, cols] = jnp.dot(hn_ref[...], win_ref[:, cols], preferred_element_type=F32)
            if cols.start <= OFF_AK and OFF_AV + 2 * HEAD_DIM <= cols.stop:
                _build_windows(proj_next, _pair_gain(gk_ref, layer), kwin_next, kwin, vwin_next, vwin)

        def first_chunk():
            out_chunk(0)
            norm_next()
            out_chunk(1)

        chunks = [first_chunk]
        chunks += [functools.partial(out_chunk, c) for c in range(2, pl.cdiv(D_MODEL, PROJ_CHUNK))]
        chunks += [functools.partial(in_chunk, c) for c in range(pl.cdiv(IN_WIDTH, PROJ_CHUNK))]
        _mix_slot(chunks, proj_cur, mixed_cur, kwin, vwin, seq_start, *consts)

    slot(0, proj_b, proj_a, mixed_b, mixed_a, kwin_b, kwin_a, vwin_b, vwin_a, False)
    slot(1, proj_a, proj_b, mixed_a, mixed_b, kwin_a, kwin_b, vwin_a, vwin_b,
         (2 * i) % tiles_per_seq == 0)

    w1_bf16_ref[...] = w1_f32_ref[...].astype(BF16)
    w2_bf16_ref[...] = w2_f32_ref[...].astype(BF16)


def _mix_call(layer, x, ada, g1, w_in, gq, gk, sinks, rg, w_out, tables, w_mlp1, w_mlp2):
    b, s, d = x.shape
    tiles_per_seq = s // MIX_ROWS
    pairs_per_seq = tiles_per_seq // 2
    n_pairs = b * pairs_per_seq
    bias_tab, ret_tab, cdec = tables
    x_pairs = x.reshape(n_pairs, 2, MIX_ROWS, d)

    nxt = lambda i: jnp.minimum(i, n_pairs - 1)
    res = lambda i: jnp.maximum(i - 1, 0)
    const2 = lambda i: (0, 0)
    const4 = lambda i: (0, 0, 0, 0)
    single = pl.Buffered(1)
    smem = pl.BlockSpec(memory_space=pltpu.SMEM)
    slab1, slab2 = w_mlp1.shape[1] // n_pairs, w_mlp2.shape[1] // n_pairs
    lay3 = lambda i: (layer, 0, 0)
    out, w1_b, w2_b = pl.pallas_call(
        functools.partial(_mix_kernel, layer, tiles_per_seq),
        grid=(n_pairs + 1,),
        in_specs=[
            pl.BlockSpec((None, 2, MIX_ROWS, d), lambda i: (nxt(i), 0, 0, 0)),
            pl.BlockSpec((None, 2, MIX_ROWS, d), lambda i: (res(i), 0, 0, 0)),
            pl.BlockSpec((None, None, 6, d), lambda i: (layer, nxt(i) // pairs_per_seq, 0, 0)),
            pl.BlockSpec((None, None, 6, d), lambda i: (layer, res(i) // pairs_per_seq, 0, 0)),
            pl.BlockSpec(g1.shape, const2),
            pl.BlockSpec((None, d, IN_WIDTH), lay3, pipeline_mode=single),
            pl.BlockSpec(gq.shape, const2),
            pl.BlockSpec(gk.shape, const2),
            smem,
            pl.BlockSpec(rg.shape, const2),
            pl.BlockSpec((None, d, d), lay3, pipeline_mode=single),
            pl.BlockSpec(bias_tab.shape, const4, pipeline_mode=single),
            pl.BlockSpec(ret_tab.shape, const4, pipeline_mode=single),
            smem,
            pl.BlockSpec((None, slab1, D_FF), lambda i: (layer, nxt(i), 0)),
            pl.BlockSpec((None, slab2, d), lambda i: (layer, nxt(i), 0)),
        ],
        out_specs=[
            pl.BlockSpec((None, 2, MIX_ROWS, d), lambda i: (res(i), 0, 0, 0)),
            pl.BlockSpec((slab1, D_FF), lambda i: (nxt(i), 0)),
            pl.BlockSpec((slab2, d), lambda i: (nxt(i), 0)),
        ],
        out_shape=[
            jax.ShapeDtypeStruct((n_pairs, 2, MIX_ROWS, d), F32),
            jax.ShapeDtypeStruct(w_mlp1.shape[1:], BF16),
            jax.ShapeDtypeStruct(w_mlp2.shape[1:], BF16),
        ],
        scratch_shapes=[
            pltpu.VMEM((MIX_ROWS, IN_WIDTH), F32),
            pltpu.VMEM((MIX_ROWS, IN_WIDTH), F32),
            pltpu.VMEM((MIX_ROWS, d), BF16),
            pltpu.VMEM((MIX_ROWS, d), BF16),
            pltpu.VMEM((MIX_ROWS, d), BF16),
            pltpu.VMEM((N_KV_HEADS, BLOCK + MIX_ROWS, 2 * HEAD_DIM), BF16),
            pltpu.VMEM((N_KV_HEADS, BLOCK + MIX_ROWS, 2 * HEAD_DIM), BF16),
            pltpu.VMEM((2 * HEAD_DIM, BLOCK + MIX_ROWS), BF16),
            pltpu.VMEM((2 * HEAD_DIM, BLOCK + MIX_ROWS), BF16),
            pltpu.VMEM((RET_HEADS, 128, 128), F32),
            pltpu.VMEM((d, IN_WIDTH), BF16),
            pltpu.VMEM((d, d), BF16),
        ],
        compiler_params=pltpu.CompilerParams(
            dimension_semantics=("arbitrary",), vmem_limit_bytes=VMEM_LIMIT),
        name="mix",
    )(x_pairs, x_pairs, ada, ada, g1, w_in, gq, gk, sinks, rg, w_out, bias_tab, ret_tab, cdec,
      w_mlp1, w_mlp2)
    return out.reshape(b, s, d), w1_b, w2_b


MLP_ROWS = 1024


def _mlp_kernel(layer, x_ref, ada_ref, g2_ref, w1_ref, w2_ref, o_ref):
    x = x_ref[...]
    ms = jnp.mean(x * x, axis=-1, keepdims=True)
    h = x * lax.rsqrt(ms + EPS) * g2_ref[layer:layer + 1, :]
    h = (h * (1.0 + ada_ref[4:5, :]) + ada_ref[3:4, :]).astype(BF16)
    acc = None
    for c in range(D_FF // FF_CHUNK):
        cols = slice(c * FF_CHUNK, (c + 1) * FF_CHUNK)
        a = jnp.dot(h, w1_ref[:, cols], preferred_element_type=F32)
        a = jnp.maximum(a, 0.0)
        part = jnp.dot((a * a).astype(BF16), w2_ref[cols, :], preferred_element_type=F32)
        acc = part if acc is None else acc + part
    o_ref[...] = x + ada_ref[5:6, :] * acc


def _mlp_call(layer, x, ada, g2, w1, w2):
    b, s, d = x.shape
    const2 = lambda i, j: (0, 0)
    single = pl.Buffered(1)
    return pl.pallas_call(
        functools.partial(_mlp_kernel, layer),
        grid=(b, s // MLP_ROWS),
        in_specs=[
            pl.BlockSpec((None, MLP_ROWS, d), lambda i, j: (i, j, 0)),
            pl.BlockSpec((None, None, 6, d), lambda i, j: (layer, i, 0, 0)),
            pl.BlockSpec(g2.shape, const2),
            pl.BlockSpec((d, D_FF), const2, pipeline_mode=single),
            pl.BlockSpec((D_FF, d), const2, pipeline_mode=single),
        ],
        out_specs=pl.BlockSpec((None, MLP_ROWS, d), lambda i, j: (i, j, 0)),
        out_shape=jax.ShapeDtypeStruct((b, s, d), F32),
        compiler_params=pltpu.CompilerParams(
            dimension_semantics=("arbitrary", "arbitrary"), vmem_limit_bytes=VMEM_LIMIT),
        name="mlp",
    )(x, ada, g2, w1, w2)


def kernel(x, c, norm1_g, norm2_g, w_ada, b_ada, w_in, q_norm_g, k_norm_g, sinks, ret_norm_g,
           w_out, w_mlp1, w_mlp2):
    b = x.shape[0]
    tables = tuple(jnp.asarray(t) for t in _constant_tables())
    ada = _ada_call(c, w_ada, b_ada).reshape(DEPTH, b, 6, D_MODEL)
    for l in range(DEPTH):
        x, w1_b, w2_b = _mix_call(l, x, ada, norm1_g, w_in, q_norm_g, k_norm_g, sinks, ret_norm_g,
                                  w_out, tables, w_mlp1, w_mlp2)
        x = _mlp_call(l, x, ada, norm2_g, w1_b, w2_b)
    return x
```

```python
import functools

import numpy as np
import jax
import jax.numpy as jnp
from jax import lax
from jax.experimental import pallas as pl
from jax.experimental.pallas import tpu as pltpu

D_MODEL = 1024
DEPTH = 2
ATTN_WIDTH = 512
RET_WIDTH = 512
HEAD_DIM = 64
N_Q_HEADS = 8
N_KV_HEADS = 2
N_PAIRS = N_Q_HEADS // 2
BLOCK = 128
RET_HEADS = 4
RET_DK = 128
D_FF = 4 * D_MODEL
EPS = 1e-6
NEG_INF = -1e30
LOG2E = 1.4426950408889634
IN_WIDTH = 2816
OFF_AQ, OFF_AK, OFF_AV, OFF_RQ, OFF_RK, OFF_RV, OFF_RG = 0, 512, 640, 768, 1280, 1792, 2304

MIX_ROWS = 256
PROJ_CHUNK = 256
FF_CHUNK = 1024
ADA_COLS = 1536
VMEM_LIMIT = 60 * 1024 * 1024

F32 = jnp.float32
BF16 = jnp.bfloat16


@functools.lru_cache(maxsize=None)
def _constant_tables():
    q_pos = np.arange(BLOCK)[:, None]
    k_pos = np.arange(2 * BLOCK)[None, :]
    dist = q_pos + BLOCK - k_pos
    valid = (dist >= 0) & (dist < BLOCK)
    valid_first = valid & (k_pos >= BLOCK)
    slopes = np.exp2(-8.0 * np.arange(1, N_Q_HEADS + 1, dtype=np.float64) / N_Q_HEADS)
    bias = -slopes[:, None, None] * dist[None].astype(np.float64)
    bias_tab = np.stack([np.where(v[None], bias * LOG2E, NEG_INF) for v in (valid_first, valid)])
    bias_tab = np.ascontiguousarray(bias_tab.transpose(0, 1, 3, 2), np.float32)

    log_gamma = np.log1p(-np.exp2(-5.0 - np.arange(RET_HEADS, dtype=np.float64)))
    idx = np.arange(BLOCK, dtype=np.float64)
    rel = idx[:, None] - idx[None, :]
    k_scale = RET_DK ** -0.5
    decay_in = np.where(rel >= 0, np.exp(log_gamma[:, None, None] * np.maximum(rel, 0.0)), 0.0) * k_scale
    q_decay = np.exp(log_gamma[:, None] * (idx[None, :] + 1.0))
    k_decay = np.exp(log_gamma[:, None] * (BLOCK - 1.0 - idx[None, :])) * k_scale
    q_decay = np.broadcast_to(q_decay[:, :, None], (RET_HEADS, BLOCK, BLOCK))
    k_decay = np.broadcast_to(k_decay[:, :, None], (RET_HEADS, BLOCK, BLOCK))
    chunk_decay = np.exp(log_gamma * BLOCK)
    ret_tab = np.stack([decay_in, q_decay, k_decay]).astype(np.float32)
    return bias_tab, ret_tab, chunk_decay.astype(np.float32)


def _ada_kernel(c_ref, w_ref, b_ref, o_ref):
    c = c_ref[...]
    c_act = c * (1.0 / (1.0 + jnp.exp(-c)))
    o_ref[...] = jnp.dot(c_act.astype(BF16), w_ref[...].astype(BF16),
                         preferred_element_type=F32) + b_ref[...]


def _ada_call(c, w_ada, b_ada):
    rows = c.shape[0]
    n = w_ada.shape[-1]
    return pl.pallas_call(
        _ada_kernel,
        grid=(DEPTH, n // ADA_COLS),
        in_specs=[
            pl.BlockSpec((rows, D_MODEL), lambda l, j: (0, 0)),
            pl.BlockSpec((None, D_MODEL, ADA_COLS), lambda l, j: (l, 0, j)),
            pl.BlockSpec((None, 1, ADA_COLS), lambda l, j: (l, 0, j)),
        ],
        out_specs=pl.BlockSpec((None, rows, ADA_COLS), lambda l, j: (l, 0, j)),
        out_shape=jax.ShapeDtypeStruct((DEPTH, rows, n), F32),
        compiler_params=pltpu.CompilerParams(
            dimension_semantics=("arbitrary", "arbitrary"), vmem_limit_bytes=VMEM_LIMIT),
        name="ada",
    )(c, w_ada, b_ada.reshape(DEPTH, 1, n))


def _head_norm(a, gain_row):
    lo = lax.broadcasted_iota(jnp.int32, a.shape, 1) < HEAD_DIM
    a2 = a * a
    s_lo = jnp.sum(jnp.where(lo, a2, 0.0), axis=-1, keepdims=True)
    s_hi = jnp.sum(jnp.where(lo, 0.0, a2), axis=-1, keepdims=True)
    mean_sq = jnp.where(lo, s_lo, s_hi) * (1.0 / HEAD_DIM)
    return a * lax.rsqrt(mean_sq + EPS) * gain_row


def _pair_gain(g_ref, layer):
    row = g_ref[layer:layer + 1, :]
    return jnp.concatenate([row, row], axis=1)


def _build_windows(proj_ref, gk_row, kwin_ref, kwin_prev, vwin_ref, vwin_prev):
    kn = _head_norm(proj_ref[:, OFF_AK:OFF_AK + 128], gk_row)
    kn_rot = pltpu.roll(kn, HEAD_DIM, axis=1)
    lo = lax.broadcasted_iota(jnp.int32, kn.shape, 1) < HEAD_DIM
    k_dup = (jnp.where(lo, kn, kn_rot), jnp.where(lo, kn_rot, kn))
    for g in range(N_KV_HEADS):
        kwin_ref[g, 0:BLOCK, :] = kwin_prev[g, MIX_ROWS:MIX_ROWS + BLOCK, :]
        kwin_ref[g, BLOCK:BLOCK + MIX_ROWS, :] = k_dup[g].astype(BF16)
    vwin_ref[:, 0:BLOCK] = vwin_prev[:, MIX_ROWS:MIX_ROWS + BLOCK]
    vwin_ref[:, BLOCK:BLOCK + MIX_ROWS] = proj_ref[:, OFF_AV:OFF_AV + 128].T.astype(BF16)


def _mix_slot(chunks, proj_ref, mixed_ref, kwin_ref, vwin_ref,
              seq_start, layer, gq_ref, sinks_ref, rg_ref,
              bias_ref, ret_ref, cdec_ref, s_ref):
    first_tab = 1 if seq_start is False else jnp.where(seq_start, 0, 1)
    chunks = list(chunks)
    n_chunks = len(chunks)
    n_points = 5 + 5 * (MIX_ROWS // BLOCK)
    point = [0]

    def issue_chunk():
        point[0] += 1
        while chunks and (n_chunks - len(chunks)) * n_points < point[0] * n_chunks:
            chunks.pop(0)()

    issue_chunk()

    lo = lax.broadcasted_iota(jnp.int32, (BLOCK, 2 * HEAD_DIM), 1) < HEAD_DIM
    gq = _pair_gain(gq_ref, layer) * (HEAD_DIM ** -0.5 * LOG2E)
    n_blocks = MIX_ROWS // BLOCK
    rows = [pl.ds(j * BLOCK, BLOCK) for j in range(n_blocks)]
    wins = [slice(j * BLOCK, (j + 2) * BLOCK) for j in range(n_blocks)]
    tabs = [first_tab if j == 0 else 1 for j in range(n_blocks)]

    s_t = {}
    for j in range(n_blocks):
        for g in range(N_KV_HEADS):
            qs = []
            for pp in range(2):
                p = 2 * g + pp
                qn = _head_norm(proj_ref[rows[j], OFF_AQ + 128 * p:OFF_AQ + 128 * p + 128], gq)
                qs.append(jnp.where(lo, qn, 0.0).astype(BF16))
                qs.append(jnp.where(lo, 0.0, qn).astype(BF16))
            s_t[j, g] = lax.dot_general(kwin_ref[g, wins[j], :], jnp.concatenate(qs, axis=0),
                                        (((1,), (1,)), ((), ())),
                                        preferred_element_type=F32)
    issue_chunk()
    sinks = [sinks_ref[layer, head] * LOG2E for head in range(N_Q_HEADS)]
    es, ms = {}, {}
    for j in range(n_blocks):
        for head in range(N_Q_HEADS):
            g, hh = divmod(head, 4)
            sh = s_t[j, g][:, 128 * hh:128 * hh + 128] + bias_ref[tabs[j], head]
            ms[j, head] = jnp.maximum(jnp.max(sh, axis=0, keepdims=True), sinks[head])
            es[j, head] = jnp.exp2(sh - ms[j, head]).astype(BF16)
            if head % 4 == 3:
                issue_chunk()
    ones = jnp.ones((HEAD_DIM, 2 * BLOCK), BF16)
    o_t = {}
    for j in range(n_blocks):
        for g in range(N_KV_HEADS):
            lhs = jnp.concatenate([vwin_ref[64 * g:64 * g + 64, wins[j]], ones], axis=0)
            p_t = jnp.concatenate([es[j, 4 * g + hh] for hh in range(4)], axis=1)
            o_t[j, g] = jnp.dot(lhs, p_t, preferred_element_type=F32)
    issue_chunk()
    for j in range(n_blocks):
        for p in range(N_PAIRS):
            g, pp = divmod(p, 2)
            halves = []
            for head in (2 * p, 2 * p + 1):
                cols = slice(128 * (head % 4), 128 * (head % 4) + 128)
                denom = o_t[j, g][HEAD_DIM:, cols] + jnp.exp2(sinks[head] - ms[j, head])
                halves.append(o_t[j, g][:HEAD_DIM, cols] / denom)
            o = jnp.concatenate(halves, axis=0).T
            mixed_ref[rows[j], 128 * p:128 * p + 128] = o.astype(BF16)
        issue_chunk()

    heads = {}
    for j in range(n_blocks):
        for hd in range(RET_HEADS):
            c0 = 128 * hd
            q_f = proj_ref[rows[j], OFF_RQ + c0:OFF_RQ + c0 + 128]
            k = proj_ref[rows[j], OFF_RK + c0:OFF_RK + c0 + 128]
            v = proj_ref[rows[j], OFF_RV + c0:OFF_RV + c0 + 128].astype(BF16)
            inner = lax.dot_general(q_f.astype(BF16), k.astype(BF16), (((1,), (1,)), ((), ())),
                                    preferred_element_type=F32)
            kd = (k * ret_ref[2, hd]).astype(BF16)
            kv = lax.dot_general(kd, v, (((0,), (0,)), ((), ())), preferred_element_type=F32)
            heads[j, hd] = (q_f, v, inner, kv)
    issue_chunk()
    outs = {}
    for j in range(n_blocks):
        for hd in range(RET_HEADS):
            q_f, v, inner, kv = heads[j, hd]
            state = s_ref[hd]
            if j == 0 and seq_start is not False:
                state = jnp.where(seq_start, 0.0, state)
            lhs = jnp.concatenate([(inner * ret_ref[0, hd]).astype(BF16), (q_f * ret_ref[1, hd]).astype(BF16)],
                                  axis=1)
            outs[j, hd] = jnp.dot(lhs, jnp.concatenate([v, state.astype(BF16)], axis=0),
                                  preferred_element_type=F32)
            s_ref[hd] = cdec_ref[hd] * state + kv
    issue_chunk()
    for j in range(n_blocks):
        for hd in range(RET_HEADS):
            c0 = 128 * hd
            o = outs[j, hd]
            gate = proj_ref[rows[j], OFF_RG + c0:OFF_RG + c0 + 128]
            mu = jnp.mean(o, axis=-1, keepdims=True)
            d = o - mu
            var = jnp.mean(d * d, axis=-1, keepdims=True)
            on = d * lax.rsqrt(var + EPS) * rg_ref[layer:layer + 1, c0:c0 + 128]
            y = gate * (1.0 / (1.0 + jnp.exp(-gate))) * on
            mixed_ref[rows[j], ATTN_WIDTH + c0:ATTN_WIDTH + c0 + 128] = y.astype(BF16)
            if hd % 2 == 1:
                issue_chunk()

    assert not chunks


def _mix_kernel(layer, tiles_per_seq,
                xn_ref, ada_n_ref, g1_ref, win_f32_ref, gq_ref, gk_ref, sinks_ref,
                rg_ref, bias_ref, ret_ref, cdec_ref, wout_f32_ref, w1_f32_ref, w2_f32_ref,
                even_ref, odd_ref, wout_bf16_ref, w1_bf16_ref, w2_bf16_ref,
                proj_a, proj_b, hn_ref, kwin_a, kwin_b, vwin_a, vwin_b, s_ref, win_ref):
    i = pl.program_id(0)

    wout_bf16_ref[...] = wout_f32_ref[...].astype(BF16)
    w1_bf16_ref[...] = w1_f32_ref[...].astype(BF16)
    w2_bf16_ref[...] = w2_f32_ref[...].astype(BF16)

    @pl.when(i == 0)
    def _():
        win_ref[...] = win_f32_ref[...].astype(BF16)
        proj_b[...] = jnp.zeros_like(proj_b)
        kwin_a[...] = jnp.zeros_like(kwin_a)
        kwin_b[...] = jnp.zeros_like(kwin_b)
        vwin_a[...] = jnp.zeros_like(vwin_a)
        vwin_b[...] = jnp.zeros_like(vwin_b)
        s_ref[...] = jnp.zeros_like(s_ref)

    consts = (layer, gq_ref, sinks_ref, rg_ref, bias_ref, ret_ref, cdec_ref, s_ref)

    def slot(half, proj_cur, proj_next, mixed_out, kwin, kwin_next, vwin, vwin_next, seq_start):
        def norm_next():
            x = xn_ref[half]
            ms = jnp.mean(x * x, axis=-1, keepdims=True)
            h = x * lax.rsqrt(ms + EPS) * g1_ref[layer:layer + 1, :]
            hn_ref[...] = (h * (1.0 + ada_n_ref[1:2, :]) + ada_n_ref[0:1, :]).astype(BF16)

        def in_chunk(c):
            cols = slice(c * PROJ_CHUNK, min((c + 1) * PROJ_CHUNK, IN_WIDTH))
            proj_next[:, cols] = jnp.dot(hn_ref[...], win_ref[:, cols], preferred_element_type=F32)
            if cols.start <= OFF_AK and OFF_AV + 2 * HEAD_DIM <= cols.stop:
                _build_windows(proj_next, _pair_gain(gk_ref, layer), kwin_next, kwin, vwin_next, vwin)

        def first_chunk():
            norm_next()
            in_chunk(0)

        chunks = [first_chunk]
        chunks += [functools.partial(in_chunk, c) for c in range(1, pl.cdiv(IN_WIDTH, PROJ_CHUNK))]
        _mix_slot(chunks, proj_cur, mixed_out, kwin, vwin, seq_start, *consts)

    slot(0, proj_b, proj_a, odd_ref, kwin_b, kwin_a, vwin_b, vwin_a, False)
    slot(1, proj_a, proj_b, even_ref, kwin_a, kwin_b, vwin_a, vwin_b, (2 * i) % tiles_per_seq == 0)


def _mix_call(layer, x, ada, g1, w_in, gq, gk, sinks, rg, w_out, tables, w_mlp1, w_mlp2):
    b, s, d = x.shape
    tiles_per_seq = s // MIX_ROWS
    pairs_per_seq = tiles_per_seq // 2
    n_pairs = b * pairs_per_seq
    bias_tab, ret_tab, cdec = tables
    x_pairs = x.reshape(n_pairs, 2, MIX_ROWS, d)

    nxt = lambda i: jnp.minimum(i, n_pairs - 1)
    const2 = lambda i: (0, 0)
    const4 = lambda i: (0, 0, 0, 0)
    lay3 = lambda i: (layer, 0, 0)
    single = pl.Buffered(1)
    smem = pl.BlockSpec(memory_space=pltpu.SMEM)
    slabs = [w.shape[1] // n_pairs for w in (w_out, w_mlp1, w_mlp2)]
    slab_in = [pl.BlockSpec((None, sl, w.shape[2]), lambda i: (layer, nxt(i), 0))
               for sl, w in zip(slabs, (w_out, w_mlp1, w_mlp2))]
    slab_out = [pl.BlockSpec((sl, w.shape[2]), lambda i: (nxt(i), 0))
                for sl, w in zip(slabs, (w_out, w_mlp1, w_mlp2))]
    return pl.pallas_call(
        functools.partial(_mix_kernel, layer, tiles_per_seq),
        grid=(n_pairs + 1,),
        in_specs=[
            pl.BlockSpec((None, 2, MIX_ROWS, d), lambda i: (nxt(i), 0, 0, 0)),
            pl.BlockSpec((None, None, 6, d), lambda i: (layer, nxt(i) // pairs_per_seq, 0, 0)),
            pl.BlockSpec(g1.shape, const2),
            pl.BlockSpec((None, d, IN_WIDTH), lay3, pipeline_mode=single),
            pl.BlockSpec(gq.shape, const2),
            pl.BlockSpec(gk.shape, const2),
            smem,
            pl.BlockSpec(rg.shape, const2),
            pl.BlockSpec(bias_tab.shape, const4, pipeline_mode=single),
            pl.BlockSpec(ret_tab.shape, const4, pipeline_mode=single),
            smem,
        ] + slab_in,
        out_specs=[
            pl.BlockSpec((None, MIX_ROWS, d), lambda i: (i, 0, 0)),
            pl.BlockSpec((None, MIX_ROWS, d), lambda i: (jnp.maximum(i - 1, 0), 0, 0)),
        ] + slab_out,
        out_shape=[
            jax.ShapeDtypeStruct((n_pairs + 1, MIX_ROWS, d), BF16),
            jax.ShapeDtypeStruct((n_pairs, MIX_ROWS, d), BF16),
        ] + [jax.ShapeDtypeStruct(w.shape[1:], BF16) for w in (w_out, w_mlp1, w_mlp2)],
        scratch_shapes=[
            pltpu.VMEM((MIX_ROWS, IN_WIDTH), F32),
            pltpu.VMEM((MIX_ROWS, IN_WIDTH), F32),
            pltpu.VMEM((MIX_ROWS, d), BF16),
            pltpu.VMEM((N_KV_HEADS, BLOCK + MIX_ROWS, 2 * HEAD_DIM), BF16),
            pltpu.VMEM((N_KV_HEADS, BLOCK + MIX_ROWS, 2 * HEAD_DIM), BF16),
            pltpu.VMEM((2 * HEAD_DIM, BLOCK + MIX_ROWS), BF16),
            pltpu.VMEM((2 * HEAD_DIM, BLOCK + MIX_ROWS), BF16),
            pltpu.VMEM((RET_HEADS, 128, 128), F32),
            pltpu.VMEM((d, IN_WIDTH), BF16),
        ],
        compiler_params=pltpu.CompilerParams(
            dimension_semantics=("arbitrary",), vmem_limit_bytes=VMEM_LIMIT),
        name="mix",
    )(x_pairs, ada, g1, w_in, gq, gk, sinks, rg, bias_tab, ret_tab, cdec, w_out, w_mlp1, w_mlp2)


MLP_ROWS = 1024


def _mlp_kernel(layer, x_ref, even_ref, odd_ref, ada_ref, g2_ref, wout_ref, w1_ref, w2_ref, o_ref):
    tiles = []
    for t in range(even_ref.shape[0]):
        tiles += [even_ref[t], odd_ref[t]]
    mixed = jnp.concatenate(tiles, axis=0)
    x = x_ref[...] + ada_ref[2:3, :] * jnp.dot(mixed, wout_ref[...], preferred_element_type=F32)
    ms = jnp.mean(x * x, axis=-1, keepdims=True)
    h = x * lax.rsqrt(ms + EPS) * g2_ref[layer:layer + 1, :]
    h = (h * (1.0 + ada_ref[4:5, :]) + ada_ref[3:4, :]).astype(BF16)
    acc = None
    for c in range(D_FF // FF_CHUNK):
        cols = slice(c * FF_CHUNK, (c + 1) * FF_CHUNK)
        a = jnp.dot(h, w1_ref[:, cols], preferred_element_type=F32)
        a = jnp.maximum(a, 0.0)
        part = jnp.dot((a * a).astype(BF16), w2_ref[cols, :], preferred_element_type=F32)
        acc = part if acc is None else acc + part
    o_ref[...] = x + ada_ref[5:6, :] * acc


def _mlp_call(layer, x, even, odd, ada, g2, w_out, w1, w2):
    b, s, d = x.shape
    steps_per_seq = s // MLP_ROWS
    pairs = MLP_ROWS // (2 * MIX_ROWS)
    const2 = lambda i, j: (0, 0)
    single = pl.Buffered(1)
    pair_block = pl.BlockSpec((pairs, MIX_ROWS, d), lambda i, j: (i * steps_per_seq + j, 0, 0))
    return pl.pallas_call(
        functools.partial(_mlp_kernel, layer),
        grid=(b, steps_per_seq),
        in_specs=[
            pl.BlockSpec((None, MLP_ROWS, d), lambda i, j: (i, j, 0)),
            pair_block,
            pair_block,
            pl.BlockSpec((None, None, 6, d), lambda i, j: (layer, i, 0, 0)),
            pl.BlockSpec(g2.shape, const2),
            pl.BlockSpec((d, d), const2, pipeline_mode=single),
            pl.BlockSpec((d, D_FF), const2, pipeline_mode=single),
            pl.BlockSpec((D_FF, d), const2, pipeline_mode=single),
        ],
        out_specs=pl.BlockSpec((None, MLP_ROWS, d), lambda i, j: (i, j, 0)),
        out_shape=jax.ShapeDtypeStruct((b, s, d), F32),
        compiler_params=pltpu.CompilerParams(
            dimension_semantics=("arbitrary", "arbitrary"), vmem_limit_bytes=VMEM_LIMIT),
        name="mlp",
    )(x, even, odd, ada, g2, w_out, w1, w2)


def kernel(x, c, norm1_g, norm2_g, w_ada, b_ada, w_in, q_norm_g, k_norm_g, sinks, ret_norm_g,
           w_out, w_mlp1, w_mlp2):
    b = x.shape[0]
    tables = tuple(jnp.asarray(t) for t in _constant_tables())
    ada = _ada_call(c, w_ada, b_ada).reshape(DEPTH, b, 6, D_MODEL)
    for l in range(DEPTH):
        even, odd, w_out_b, w1_b, w2_b = _mix_call(l, x, ada, norm1_g, w_in, q_norm_g, k_norm_g, sinks,
                                                   ret_norm_g, w_out, tables, w_mlp1, w_mlp2)
        x = _mlp_call(l, x, even, odd, ada, norm2_g, w_out_b, w1_b, w2_b)
    return x
```

```python
import functools

import numpy as np
import jax
import jax.numpy as jnp
from jax import lax
from jax.experimental import pallas as pl
from jax.experimental.pallas import tpu as pltpu

D_MODEL = 1024
DEPTH = 2
ATTN_WIDTH = 512
RET_WIDTH = 512
HEAD_DIM = 64
N_Q_HEADS = 8
N_KV_HEADS = 2
N_PAIRS = N_Q_HEADS // 2
BLOCK = 128
RET_HEADS = 4
RET_DK = 128
D_FF = 4 * D_MODEL
EPS = 1e-6
NEG_INF = -1e30
LOG2E = 1.4426950408889634
IN_WIDTH = 2816
OFF_AQ, OFF_AK, OFF_AV, OFF_RQ, OFF_RK, OFF_RV, OFF_RG = 0, 512, 640, 768, 1280, 1792, 2304

MIX_ROWS = 256
PROJ_CHUNK = 256
FF_CHUNK = 1024
ADA_COLS = 1536
VMEM_LIMIT = 60 * 1024 * 1024

F32 = jnp.float32
BF16 = jnp.bfloat16


@functools.lru_cache(maxsize=None)
def _constant_tables():
    q_pos = np.arange(BLOCK)[:, None]
    k_pos = np.arange(2 * BLOCK)[None, :]
    dist = q_pos + BLOCK - k_pos
    valid = (dist >= 0) & (dist < BLOCK)
    valid_first = valid & (k_pos >= BLOCK)
    slopes = np.exp2(-8.0 * np.arange(1, N_Q_HEADS + 1, dtype=np.float64) / N_Q_HEADS)
    bias = -slopes[:, None, None] * dist[None].astype(np.float64)
    bias_tab = np.stack([np.where(v[None], bias * LOG2E, NEG_INF) for v in (valid_first, valid)])
    bias_tab = np.ascontiguousarray(bias_tab.transpose(0, 1, 3, 2), np.float32)

    log_gamma = np.log1p(-np.exp2(-5.0 - np.arange(RET_HEADS, dtype=np.float64)))
    idx = np.arange(BLOCK, dtype=np.float64)
    rel = idx[:, None] - idx[None, :]
    k_scale = RET_DK ** -0.5
    decay_in = np.where(rel >= 0, np.exp(log_gamma[:, None, None] * np.maximum(rel, 0.0)), 0.0) * k_scale
    q_decay = np.exp(log_gamma[:, None] * (idx[None, :] + 1.0))
    k_decay = np.exp(log_gamma[:, None] * (BLOCK - 1.0 - idx[None, :])) * k_scale
    q_decay = np.broadcast_to(q_decay[:, :, None], (RET_HEADS, BLOCK, BLOCK))
    k_decay = np.broadcast_to(k_decay[:, :, None], (RET_HEADS, BLOCK, BLOCK))
    chunk_decay = np.exp(log_gamma * BLOCK)
    ret_tab = np.stack([decay_in, q_decay, k_decay]).astype(np.float32)
    return bias_tab, ret_tab, chunk_decay.astype(np.float32)


def _ada_kernel(c_ref, w_ref, b_ref, o_ref):
    c = c_ref[...]
    c_act = c * (1.0 / (1.0 + jnp.exp(-c)))
    o_ref[...] = jnp.dot(c_act.astype(BF16), w_ref[...].astype(BF16),
                         preferred_element_type=F32) + b_ref[pl.ds(pl.program_id(0), 1), :]


def _ada_call(c, w_ada, b_ada):
    rows = c.shape[0]
    n = w_ada.shape[-1]
    return pl.pallas_call(
        _ada_kernel,
        grid=(DEPTH, n // ADA_COLS),
        in_specs=[
            pl.BlockSpec((rows, D_MODEL), lambda l, j: (0, 0)),
            pl.BlockSpec((None, D_MODEL, ADA_COLS), lambda l, j: (l, 0, j)),
            pl.BlockSpec((DEPTH, ADA_COLS), lambda l, j: (0, j)),
        ],
        out_specs=pl.BlockSpec((None, rows, ADA_COLS), lambda l, j: (l, 0, j)),
        out_shape=jax.ShapeDtypeStruct((DEPTH, rows, n), F32),
        compiler_params=pltpu.CompilerParams(
            dimension_semantics=("arbitrary", "arbitrary"), vmem_limit_bytes=VMEM_LIMIT),
        name="ada",
    )(c, w_ada, b_ada)


def _head_norm(a, gain_row):
    lo = lax.broadcasted_iota(jnp.int32, a.shape, 1) < HEAD_DIM
    a2 = a * a
    s_lo = jnp.sum(jnp.where(lo, a2, 0.0), axis=-1, keepdims=True)
    s_hi = jnp.sum(jnp.where(lo, 0.0, a2), axis=-1, keepdims=True)
    mean_sq = jnp.where(lo, s_lo, s_hi) * (1.0 / HEAD_DIM)
    return a * lax.rsqrt(mean_sq + EPS) * gain_row


def _pair_gain(g_ref, layer):
    row = g_ref[layer:layer + 1, :]
    return jnp.concatenate([row, row], axis=1)


def _build_windows(proj_ref, gk_row, kwin_ref, kwin_prev, vwin_ref, vwin_prev):
    kn = _head_norm(proj_ref[:, OFF_AK:OFF_AK + 128], gk_row)
    kn_rot = pltpu.roll(kn, HEAD_DIM, axis=1)
    lo = lax.broadcasted_iota(jnp.int32, kn.shape, 1) < HEAD_DIM
    k_dup = (jnp.where(lo, kn, kn_rot), jnp.where(lo, kn_rot, kn))
    for g in range(N_KV_HEADS):
        kwin_ref[g, 0:BLOCK, :] = kwin_prev[g, MIX_ROWS:MIX_ROWS + BLOCK, :]
        kwin_ref[g, BLOCK:BLOCK + MIX_ROWS, :] = k_dup[g].astype(BF16)
    vwin_ref[:, 0:BLOCK] = vwin_prev[:, MIX_ROWS:MIX_ROWS + BLOCK]
    vwin_ref[:, BLOCK:BLOCK + MIX_ROWS] = proj_ref[:, OFF_AV:OFF_AV + 128].T.astype(BF16)


def _mix_slot(chunks, proj_ref, mixed_ref, kwin_ref, vwin_ref,
              seq_start, layer, gq_ref, sinks_ref, rg_ref,
              bias_ref, ret_ref, cdec_ref, s_ref):
    first_tab = 1 if seq_start is False else jnp.where(seq_start, 0, 1)
    chunks = list(chunks)
    n_chunks = len(chunks)
    n_points = 5 + 5 * (MIX_ROWS // BLOCK)
    point = [0]

    def issue_chunk():
        point[0] += 1
        while chunks and (n_chunks - len(chunks)) * n_points < point[0] * n_chunks:
            chunks.pop(0)()

    issue_chunk()

    lo = lax.broadcasted_iota(jnp.int32, (BLOCK, 2 * HEAD_DIM), 1) < HEAD_DIM
    gq = _pair_gain(gq_ref, layer) * (HEAD_DIM ** -0.5 * LOG2E)
    n_blocks = MIX_ROWS // BLOCK
    rows = [pl.ds(j * BLOCK, BLOCK) for j in range(n_blocks)]
    wins = [slice(j * BLOCK, (j + 2) * BLOCK) for j in range(n_blocks)]
    tabs = [first_tab if j == 0 else 1 for j in range(n_blocks)]

    s_t = {}
    for j in range(n_blocks):
        for g in range(N_KV_HEADS):
            qs = []
            for pp in range(2):
                p = 2 * g + pp
                qn = _head_norm(proj_ref[rows[j], OFF_AQ + 128 * p:OFF_AQ + 128 * p + 128], gq)
                qs.append(jnp.where(lo, qn, 0.0).astype(BF16))
                qs.append(jnp.where(lo, 0.0, qn).astype(BF16))
            s_t[j, g] = lax.dot_general(kwin_ref[g, wins[j], :], jnp.concatenate(qs, axis=0),
                                        (((1,), (1,)), ((), ())),
                                        preferred_element_type=F32)
    issue_chunk()
    sinks = [sinks_ref[layer, head] * LOG2E for head in range(N_Q_HEADS)]
    es, ms = {}, {}
    for j in range(n_blocks):
        for head in range(N_Q_HEADS):
            g, hh = divmod(head, 4)
            sh = s_t[j, g][:, 128 * hh:128 * hh + 128] + bias_ref[tabs[j], head]
            ms[j, head] = jnp.maximum(jnp.max(sh, axis=0, keepdims=True), sinks[head])
            es[j, head] = jnp.exp2(sh - ms[j, head]).astype(BF16)
            if head % 4 == 3:
                issue_chunk()
    ones = jnp.ones((HEAD_DIM, 2 * BLOCK), BF16)
    o_t = {}
    for j in range(n_blocks):
        for g in range(N_KV_HEADS):
            lhs = jnp.concatenate([vwin_ref[64 * g:64 * g + 64, wins[j]], ones], axis=0)
            p_t = jnp.concatenate([es[j, 4 * g + hh] for hh in range(4)], axis=1)
            o_t[j, g] = jnp.dot(lhs, p_t, preferred_element_type=F32)
    issue_chunk()
    for j in range(n_blocks):
        for p in range(N_PAIRS):
            g, pp = divmod(p, 2)
            halves = []
            for head in (2 * p, 2 * p + 1):
                cols = slice(128 * (head % 4), 128 * (head % 4) + 128)
                denom = o_t[j, g][HEAD_DIM:, cols] + jnp.exp2(sinks[head] - ms[j, head])
                halves.append(o_t[j, g][:HEAD_DIM, cols] / denom)
            o = jnp.concatenate(halves, axis=0).T
            mixed_ref[rows[j], 128 * p:128 * p + 128] = o.astype(BF16)
        issue_chunk()

    heads = {}
    for j in range(n_blocks):
        for hd in range(RET_HEADS):
            c0 = 128 * hd
            q_f = proj_ref[rows[j], OFF_RQ + c0:OFF_RQ + c0 + 128]
            k = proj_ref[rows[j], OFF_RK + c0:OFF_RK + c0 + 128]
            v = proj_ref[rows[j], OFF_RV + c0:OFF_RV + c0 + 128].astype(BF16)
            inner = lax.dot_general(q_f.astype(BF16), k.astype(BF16), (((1,), (1,)), ((), ())),
                                    preferred_element_type=F32)
            kd = (k * ret_ref[2, hd]).astype(BF16)
            kv = lax.dot_general(kd, v, (((0,), (0,)), ((), ())), preferred_element_type=F32)
            heads[j, hd] = (q_f, v, inner, kv)
    issue_chunk()
    outs = {}
    for j in range(n_blocks):
        for hd in range(RET_HEADS):
            q_f, v, inner, kv = heads[j, hd]
            state = s_ref[hd]
            if j == 0 and seq_start is not False:
                state = jnp.where(seq_start, 0.0, state)
            lhs = jnp.concatenate([(inner * ret_ref[0, hd]).astype(BF16), (q_f * ret_ref[1, hd]).astype(BF16)],
                                  axis=1)
            outs[j, hd] = jnp.dot(lhs, jnp.concatenate([v, state.astype(BF16)], axis=0),
                                  preferred_element_type=F32)
            s_ref[hd] = cdec_ref[hd] * state + kv
    issue_chunk()
    for j in range(n_blocks):
        for hd in range(RET_HEADS):
            c0 = 128 * hd
            o = outs[j, hd]
            gate = proj_ref[rows[j], OFF_RG + c0:OFF_RG + c0 + 128]
            mu = jnp.mean(o, axis=-1, keepdims=True)
            d = o - mu
            var = jnp.mean(d * d, axis=-1, keepdims=True)
            on = d * lax.rsqrt(var + EPS) * rg_ref[layer:layer + 1, c0:c0 + 128]
            y = gate * (1.0 / (1.0 + jnp.exp(-gate))) * on
            mixed_ref[rows[j], ATTN_WIDTH + c0:ATTN_WIDTH + c0 + 128] = y.astype(BF16)
            if hd % 2 == 1:
                issue_chunk()

    assert not chunks


def _mix_kernel(layer, tiles_per_seq, n_pairs,
                xn_ref, xr_ref, ada_ref, g1_ref, win_f32_ref, gq_ref, gk_ref, sinks_ref,
                rg_ref, wout_f32_ref, bias_ref, ret_ref, cdec_ref, w1_f32_ref, w2_f32_ref,
                o_ref, w1_bf16_ref, w2_bf16_ref,
                proj_a, proj_b, mixed_a, mixed_b, hn_ref, kwin_a, kwin_b, vwin_a, vwin_b, s_ref,
                win_ref, wout_ref):
    i = pl.program_id(0)
    pairs_per_seq = tiles_per_seq // 2
    ada_n = pl.ds(jnp.minimum(i, n_pairs - 1) // pairs_per_seq, 1)
    ada_r = pl.ds(jnp.maximum(i - 1, 0) // pairs_per_seq, 1)

    @pl.when(i == 0)
    def _():
        win_ref[...] = win_f32_ref[...].astype(BF16)
        wout_ref[...] = wout_f32_ref[...].astype(BF16)

    w1_bf16_ref[...] = w1_f32_ref[...].astype(BF16)
    w2_bf16_ref[...] = w2_f32_ref[...].astype(BF16)

    @pl.when(i == 0)
    def _():
        proj_b[...] = jnp.zeros_like(proj_b)
        mixed_a[...] = jnp.zeros_like(mixed_a)
        kwin_a[...] = jnp.zeros_like(kwin_a)
        kwin_b[...] = jnp.zeros_like(kwin_b)
        vwin_a[...] = jnp.zeros_like(vwin_a)
        vwin_b[...] = jnp.zeros_like(vwin_b)
        s_ref[...] = jnp.zeros_like(s_ref)

    consts = (layer, gq_ref, sinks_ref, rg_ref, bias_ref, ret_ref, cdec_ref, s_ref)

    def slot(half, proj_cur, proj_next, mixed_cur, mixed_prev, kwin, kwin_next, vwin, vwin_next,
             seq_start):
        def out_chunk(c):
            cols = slice(c * PROJ_CHUNK, min((c + 1) * PROJ_CHUNK, D_MODEL))
            y = jnp.dot(mixed_prev[...], wout_ref[:, cols], preferred_element_type=F32)
            gate = ada_ref[ada_r, 2 * D_MODEL + cols.start:2 * D_MODEL + cols.stop]
            o_ref[half, :, cols] = xr_ref[half, :, cols] + gate * y

        def norm_next():
            x = xn_ref[half]
            ms = jnp.mean(x * x, axis=-1, keepdims=True)
            h = x * lax.rsqrt(ms + EPS) * g1_ref[layer:layer + 1, :]
            shift, scale = ada_ref[ada_n, 0:D_MODEL], ada_ref[ada_n, D_MODEL:2 * D_MODEL]
            hn_ref[...] = (h * (1.0 + scale) + shift).astype(BF16)

        def in_chunk(c):
            cols = slice(c * PROJ_CHUNK, min((c + 1) * PROJ_CHUNK, IN_WIDTH))
            proj_next[:, cols] = jnp.dot(hn_ref[...], win_ref[:, cols], preferred_element_type=F32)
            if cols.start <= OFF_AK and OFF_AV + 2 * HEAD_DIM <= cols.stop:
                _build_windows(proj_next, _pair_gain(gk_ref, layer), kwin_next, kwin, vwin_next, vwin)

        def first_chunk():
            out_chunk(0)
            norm_next()
            out_chunk(1)

        chunks = [first_chunk]
        chunks += [functools.partial(out_chunk, c) for c in range(2, pl.cdiv(D_MODEL, PROJ_CHUNK))]
        chunks += [functools.partial(in_chunk, c) for c in range(pl.cdiv(IN_WIDTH, PROJ_CHUNK))]
        _mix_slot(chunks, proj_cur, mixed_cur, kwin, vwin, seq_start, *consts)

    slot(0, proj_b, proj_a, mixed_b, mixed_a, kwin_b, kwin_a, vwin_b, vwin_a, False)
    slot(1, proj_a, proj_b, mixed_a, mixed_b, kwin_a, kwin_b, vwin_a, vwin_b,
         (2 * i) % tiles_per_seq == 0)


def _mix_call(layer, x, ada, g1, w_in, gq, gk, sinks, rg, w_out, tables, w_mlp1, w_mlp2):
    b, s, d = x.shape
    tiles_per_seq = s // MIX_ROWS
    pairs_per_seq = tiles_per_seq // 2
    n_pairs = b * pairs_per_seq
    bias_tab, ret_tab, cdec = tables
    x_pairs = x.reshape(n_pairs, 2, MIX_ROWS, d)

    nxt = lambda i: jnp.minimum(i, n_pairs - 1)
    res = lambda i: jnp.maximum(i - 1, 0)
    const2 = lambda i: (0, 0)
    const4 = lambda i: (0, 0, 0, 0)
    single = pl.Buffered(1)
    smem = pl.BlockSpec(memory_space=pltpu.SMEM)
    slab1, slab2 = w_mlp1.shape[1] // n_pairs, w_mlp2.shape[1] // n_pairs
    lay3 = lambda i: (layer, 0, 0)
    out, w1_b, w2_b = pl.pallas_call(
        functools.partial(_mix_kernel, layer, tiles_per_seq, n_pairs),
        grid=(n_pairs + 1,),
        in_specs=[
            pl.BlockSpec((None, 2, MIX_ROWS, d), lambda i: (nxt(i), 0, 0, 0)),
            pl.BlockSpec((None, 2, MIX_ROWS, d), lambda i: (res(i), 0, 0, 0)),
            pl.BlockSpec((None,) + ada.shape[1:], lambda i: (layer, 0, 0)),
            pl.BlockSpec(g1.shape, const2),
            pl.BlockSpec((None, d, IN_WIDTH), lay3, pipeline_mode=single),
            pl.BlockSpec(gq.shape, const2),
            pl.BlockSpec(gk.shape, const2),
            smem,
            pl.BlockSpec(rg.shape, const2),
            pl.BlockSpec((None, d, d), lay3, pipeline_mode=single),
            pl.BlockSpec(bias_tab.shape, const4, pipeline_mode=single),
            pl.BlockSpec(ret_tab.shape, const4, pipeline_mode=single),
            smem,
            pl.BlockSpec((None, slab1, D_FF), lambda i: (layer, nxt(i), 0)),
            pl.BlockSpec((None, slab2, d), lambda i: (layer, nxt(i), 0)),
        ],
        out_specs=[
            pl.BlockSpec((None, 2, MIX_ROWS, d), lambda i: (res(i), 0, 0, 0)),
            pl.BlockSpec((slab1, D_FF), lambda i: (nxt(i), 0)),
            pl.BlockSpec((slab2, d), lambda i: (nxt(i), 0)),
        ],
        out_shape=[
            jax.ShapeDtypeStruct((n_pairs, 2, MIX_ROWS, d), F32),
            jax.ShapeDtypeStruct(w_mlp1.shape[1:], BF16),
            jax.ShapeDtypeStruct(w_mlp2.shape[1:], BF16),
        ],
        scratch_shapes=[
            pltpu.VMEM((MIX_ROWS, IN_WIDTH), F32),
            pltpu.VMEM((MIX_ROWS, IN_WIDTH), F32),
            pltpu.VMEM((MIX_ROWS, d), BF16),
            pltpu.VMEM((MIX_ROWS, d), BF16),
            pltpu.VMEM((MIX_ROWS, d), BF16),
            pltpu.VMEM((N_KV_HEADS, BLOCK + MIX_ROWS, 2 * HEAD_DIM), BF16),
            pltpu.VMEM((N_KV_HEADS, BLOCK + MIX_ROWS, 2 * HEAD_DIM), BF16),
            pltpu.VMEM((2 * HEAD_DIM, BLOCK + MIX_ROWS), BF16),
            pltpu.VMEM((2 * HEAD_DIM, BLOCK + MIX_ROWS), BF16),
            pltpu.VMEM((RET_HEADS, 128, 128), F32),
            pltpu.VMEM((d, IN_WIDTH), BF16),
            pltpu.VMEM((d, d), BF16),
        ],
        compiler_params=pltpu.CompilerParams(
            dimension_semantics=("arbitrary",), vmem_limit_bytes=VMEM_LIMIT),
        name="mix",
    )(x_pairs, x_pairs, ada, g1, w_in, gq, gk, sinks, rg, w_out, bias_tab, ret_tab, cdec,
      w_mlp1, w_mlp2)
    return out.reshape(b, s, d), w1_b, w2_b


MLP_ROWS = 1024


def _mlp_kernel(layer, x_ref, ada_ref, g2_ref, w1_ref, w2_ref, o_ref):
    x = x_ref[...]
    ms = jnp.mean(x * x, axis=-1, keepdims=True)
    h = x * lax.rsqrt(ms + EPS) * g2_ref[layer:layer + 1, :]
    row = pl.ds(pl.program_id(0), 1)
    shift, scale, gate = (ada_ref[row, k * D_MODEL:(k + 1) * D_MODEL] for k in (3, 4, 5))
    h = (h * (1.0 + scale) + shift).astype(BF16)
    acc = None
    for c in range(D_FF // FF_CHUNK):
        cols = slice(c * FF_CHUNK, (c + 1) * FF_CHUNK)
        a = jnp.dot(h, w1_ref[:, cols], preferred_element_type=F32)
        a = jnp.maximum(a, 0.0)
        part = jnp.dot((a * a).astype(BF16), w2_ref[cols, :], preferred_element_type=F32)
        acc = part if acc is None else acc + part
    o_ref[...] = x + gate * acc


def _mlp_call(layer, x, ada, g2, w1, w2):
    b, s, d = x.shape
    const2 = lambda i, j: (0, 0)
    single = pl.Buffered(1)
    return pl.pallas_call(
        functools.partial(_mlp_kernel, layer),
        grid=(b, s // MLP_ROWS),
        in_specs=[
            pl.BlockSpec((None, MLP_ROWS, d), lambda i, j: (i, j, 0)),
            pl.BlockSpec((None,) + ada.shape[1:], lambda i, j: (layer, 0, 0)),
            pl.BlockSpec(g2.shape, const2),
            pl.BlockSpec((d, D_FF), const2, pipeline_mode=single),
            pl.BlockSpec((D_FF, d), const2, pipeline_mode=single),
        ],
        out_specs=pl.BlockSpec((None, MLP_ROWS, d), lambda i, j: (i, j, 0)),
        out_shape=jax.ShapeDtypeStruct((b, s, d), F32),
        compiler_params=pltpu.CompilerParams(
            dimension_semantics=("arbitrary", "arbitrary"), vmem_limit_bytes=VMEM_LIMIT),
        name="mlp",
    )(x, ada, g2, w1, w2)


def kernel(x, c, norm1_g, norm2_g, w_ada, b_ada, w_in, q_norm_g, k_norm_g, sinks, ret_norm_g,
           w_out, w_mlp1, w_mlp2):
    tables = tuple(jnp.asarray(t) for t in _constant_tables())
    ada = _ada_call(c, w_ada, b_ada)
    for l in range(DEPTH):
        x, w1_b, w2_b = _mix_call(l, x, ada, norm1_g, w_in, q_norm_g, k_norm_g, sinks, ret_norm_g,
                                  w_out, tables, w_mlp1, w_mlp2)
        x = _mlp_call(l, x, ada, norm2_g, w1_b, w2_b)
    return x
```

```python
import functools

import numpy as np
import jax
import jax.numpy as jnp
from jax import lax
from jax.experimental import pallas as pl
from jax.experimental.pallas import tpu as pltpu

D_MODEL = 1024
DEPTH = 2
ATTN_WIDTH = 512
RET_WIDTH = 512
HEAD_DIM = 64
N_Q_HEADS = 8
N_KV_HEADS = 2
N_PAIRS = N_Q_HEADS // 2
BLOCK = 128
RET_HEADS = 4
RET_DK = 128
D_FF = 4 * D_MODEL
EPS = 1e-6
NEG_INF = -1e30
LOG2E = 1.4426950408889634
IN_WIDTH = 2816
OFF_AQ, OFF_AK, OFF_AV, OFF_RQ, OFF_RK, OFF_RV, OFF_RG = 0, 512, 640, 768, 1280, 1792, 2304

MIX_ROWS = 256
PROJ_CHUNK = 256
FF_CHUNK = 1024
ADA_COLS = 1536
VMEM_LIMIT = 60 * 1024 * 1024

F32 = jnp.float32
BF16 = jnp.bfloat16


@functools.lru_cache(maxsize=None)
def _constant_tables():
    q_pos = np.arange(BLOCK)[:, None]
    k_pos = np.arange(2 * BLOCK)[None, :]
    dist = q_pos + BLOCK - k_pos
    valid = (dist >= 0) & (dist < BLOCK)
    valid_first = valid & (k_pos >= BLOCK)
    slopes = np.exp2(-8.0 * np.arange(1, N_Q_HEADS + 1, dtype=np.float64) / N_Q_HEADS)
    bias = -slopes[:, None, None] * dist[None].astype(np.float64)
    bias_tab = np.stack([np.where(v[None], bias * LOG2E, NEG_INF) for v in (valid_first, valid)])
    bias_tab = np.ascontiguousarray(bias_tab.transpose(0, 1, 3, 2), np.float32)

    log_gamma = np.log1p(-np.exp2(-5.0 - np.arange(RET_HEADS, dtype=np.float64)))
    idx = np.arange(BLOCK, dtype=np.float64)
    rel = idx[:, None] - idx[None, :]
    k_scale = RET_DK ** -0.5
    decay_in = np.where(rel >= 0, np.exp(log_gamma[:, None, None] * np.maximum(rel, 0.0)), 0.0) * k_scale
    q_decay = np.exp(log_gamma[:, None] * (idx[None, :] + 1.0))
    k_decay = np.exp(log_gamma[:, None] * (BLOCK - 1.0 - idx[None, :])) * k_scale
    q_decay = np.broadcast_to(q_decay[:, :, None], (RET_HEADS, BLOCK, BLOCK))
    k_decay = np.broadcast_to(k_decay[:, :, None], (RET_HEADS, BLOCK, BLOCK))
    chunk_decay = np.exp(log_gamma * BLOCK)
    ret_tab = np.stack([decay_in, q_decay, k_decay]).astype(np.float32)
    return bias_tab, ret_tab, chunk_decay.astype(np.float32)


ADA_SLOTS = 4


def _ada_kernel(c_ref, w_hbm, b_ref, o_ref, stage_ref, sem_ref):
    blocks_per_layer = w_hbm.shape[2] // ADA_COLS
    n_blocks = DEPTH * blocks_per_layer

    def block_copy(k):
        layer, j = divmod(k, blocks_per_layer)
        return pltpu.make_async_copy(w_hbm.at[layer, :, pl.ds(j * ADA_COLS, ADA_COLS)],
                                     stage_ref.at[k % ADA_SLOTS], sem_ref.at[k % ADA_SLOTS])

    for k in range(ADA_SLOTS):
        block_copy(k).start()
    c = c_ref[...]
    c_act = (c * (1.0 / (1.0 + jnp.exp(-c)))).astype(BF16)
    for k in range(n_blocks):
        layer, j = divmod(k, blocks_per_layer)
        cols = slice(j * ADA_COLS, (j + 1) * ADA_COLS)
        block_copy(k).wait()
        o_ref[layer, :, cols] = (jnp.dot(c_act, stage_ref[k % ADA_SLOTS].astype(BF16),
                                         preferred_element_type=F32) + b_ref[layer:layer + 1, cols])
        if k + ADA_SLOTS < n_blocks:
            block_copy(k + ADA_SLOTS).start()


def _ada_call(c, w_ada, b_ada):
    rows = c.shape[0]
    n = w_ada.shape[-1]
    return pl.pallas_call(
        _ada_kernel,
        grid=(1,),
        in_specs=[
            pl.BlockSpec(c.shape, lambda i: (0, 0)),
            pl.BlockSpec(memory_space=pl.ANY),
            pl.BlockSpec(b_ada.shape, lambda i: (0, 0)),
        ],
        out_specs=pl.BlockSpec((DEPTH, rows, n), lambda i: (0, 0, 0)),
        out_shape=jax.ShapeDtypeStruct((DEPTH, rows, n), F32),
        scratch_shapes=[
            pltpu.VMEM((ADA_SLOTS, D_MODEL, ADA_COLS), F32),
            pltpu.SemaphoreType.DMA((ADA_SLOTS,)),
        ],
        compiler_params=pltpu.CompilerParams(
            dimension_semantics=("arbitrary",), vmem_limit_bytes=VMEM_LIMIT),
        name="ada",
    )(c, w_ada, b_ada)


def _head_norm(a, gain_row):
    lo = lax.broadcasted_iota(jnp.int32, a.shape, 1) < HEAD_DIM
    a2 = a * a
    s_lo = jnp.sum(jnp.where(lo, a2, 0.0), axis=-1, keepdims=True)
    s_hi = jnp.sum(jnp.where(lo, 0.0, a2), axis=-1, keepdims=True)
    mean_sq = jnp.where(lo, s_lo, s_hi) * (1.0 / HEAD_DIM)
    return a * lax.rsqrt(mean_sq + EPS) * gain_row


def _pair_gain(g_ref, layer):
    row = g_ref[layer:layer + 1, :]
    return jnp.concatenate([row, row], axis=1)


def _build_windows(proj_ref, gk_row, kwin_ref, kwin_prev, vwin_ref, vwin_prev):
    kn = _head_norm(proj_ref[:, OFF_AK:OFF_AK + 128], gk_row)
    kn_rot = pltpu.roll(kn, HEAD_DIM, axis=1)
    lo = lax.broadcasted_iota(jnp.int32, kn.shape, 1) < HEAD_DIM
    k_dup = (jnp.where(lo, kn, kn_rot), jnp.where(lo, kn_rot, kn))
    for g in range(N_KV_HEADS):
        kwin_ref[g, 0:BLOCK, :] = kwin_prev[g, MIX_ROWS:MIX_ROWS + BLOCK, :]
        kwin_ref[g, BLOCK:BLOCK + MIX_ROWS, :] = k_dup[g].astype(BF16)
    vwin_ref[:, 0:BLOCK] = vwin_prev[:, MIX_ROWS:MIX_ROWS + BLOCK]
    vwin_ref[:, BLOCK:BLOCK + MIX_ROWS] = proj_ref[:, OFF_AV:OFF_AV + 128].T.astype(BF16)


def _mix_slot(chunks, proj_ref, mixed_ref, kwin_ref, vwin_ref,
              seq_start, layer, gq_ref, sinks_ref, rg_ref,
              bias_ref, ret_ref, cdec_ref, s_ref):
    first_tab = 1 if seq_start is False else jnp.where(seq_start, 0, 1)
    chunks = list(chunks)
    n_chunks = len(chunks)
    n_points = 5 + 5 * (MIX_ROWS // BLOCK)
    point = [0]

    def issue_chunk():
        point[0] += 1
        while chunks and (n_chunks - len(chunks)) * n_points < point[0] * n_chunks:
            chunks.pop(0)()

    issue_chunk()

    lo = lax.broadcasted_iota(jnp.int32, (BLOCK, 2 * HEAD_DIM), 1) < HEAD_DIM
    gq = _pair_gain(gq_ref, layer) * (HEAD_DIM ** -0.5 * LOG2E)
    n_blocks = MIX_ROWS // BLOCK
    rows = [pl.ds(j * BLOCK, BLOCK) for j in range(n_blocks)]
    wins = [slice(j * BLOCK, (j + 2) * BLOCK) for j in range(n_blocks)]
    tabs = [first_tab if j == 0 else 1 for j in range(n_blocks)]

    s_t = {}
    for j in range(n_blocks):
        for g in range(N_KV_HEADS):
            qs = []
            for pp in range(2):
                p = 2 * g + pp
                qn = _head_norm(proj_ref[rows[j], OFF_AQ + 128 * p:OFF_AQ + 128 * p + 128], gq)
                qs.append(jnp.where(lo, qn, 0.0).astype(BF16))
                qs.append(jnp.where(lo, 0.0, qn).astype(BF16))
            s_t[j, g] = lax.dot_general(kwin_ref[g, wins[j], :], jnp.concatenate(qs, axis=0),
                                        (((1,), (1,)), ((), ())),
                                        preferred_element_type=F32)
    issue_chunk()
    sinks = [sinks_ref[layer, head] * LOG2E for head in range(N_Q_HEADS)]
    es, ms = {}, {}
    for j in range(n_blocks):
        for head in range(N_Q_HEADS):
            g, hh = divmod(head, 4)
            sh = s_t[j, g][:, 128 * hh:128 * hh + 128] + bias_ref[tabs[j], head]
            ms[j, head] = jnp.maximum(jnp.max(sh, axis=0, keepdims=True), sinks[head])
            es[j, head] = jnp.exp2(sh - ms[j, head]).astype(BF16)
            if head % 4 == 3:
                issue_chunk()
    ones = jnp.ones((HEAD_DIM, 2 * BLOCK), BF16)
    o_t = {}
    for j in range(n_blocks):
        for g in range(N_KV_HEADS):
            lhs = jnp.concatenate([vwin_ref[64 * g:64 * g + 64, wins[j]], ones], axis=0)
            p_t = jnp.concatenate([es[j, 4 * g + hh] for hh in range(4)], axis=1)
            o_t[j, g] = jnp.dot(lhs, p_t, preferred_element_type=F32)
    issue_chunk()
    for j in range(n_blocks):
        for p in range(N_PAIRS):
            g, pp = divmod(p, 2)
            halves = []
            for head in (2 * p, 2 * p + 1):
                cols = slice(128 * (head % 4), 128 * (head % 4) + 128)
                denom = o_t[j, g][HEAD_DIM:, cols] + jnp.exp2(sinks[head] - ms[j, head])
                halves.append(o_t[j, g][:HEAD_DIM, cols] / denom)
            o = jnp.concatenate(halves, axis=0).T
            mixed_ref[rows[j], 128 * p:128 * p + 128] = o.astype(BF16)
        issue_chunk()

    heads = {}
    for j in range(n_blocks):
        for hd in range(RET_HEADS):
            c0 = 128 * hd
            q_f = proj_ref[rows[j], OFF_RQ + c0:OFF_RQ + c0 + 128]
            k = proj_ref[rows[j], OFF_RK + c0:OFF_RK + c0 + 128]
            v = proj_ref[rows[j], OFF_RV + c0:OFF_RV + c0 + 128].astype(BF16)
            inner = lax.dot_general(q_f.astype(BF16), k.astype(BF16), (((1,), (1,)), ((), ())),
                                    preferred_element_type=F32)
            kd = (k * ret_ref[2, hd]).astype(BF16)
            kv = lax.dot_general(kd, v, (((0,), (0,)), ((), ())), preferred_element_type=F32)
            heads[j, hd] = (q_f, v, inner, kv)
    issue_chunk()
    outs = {}
    for j in range(n_blocks):
        for hd in range(RET_HEADS):
            q_f, v, inner, kv = heads[j, hd]
            state = s_ref[hd]
            if j == 0 and seq_start is not False:
                state = jnp.where(seq_start, 0.0, state)
            lhs = jnp.concatenate([(inner * ret_ref[0, hd]).astype(BF16), (q_f * ret_ref[1, hd]).astype(BF16)],
                                  axis=1)
            outs[j, hd] = jnp.dot(lhs, jnp.concatenate([v, state.astype(BF16)], axis=0),
                                  preferred_element_type=F32)
            s_ref[hd] = cdec_ref[hd] * state + kv
    issue_chunk()
    for j in range(n_blocks):
        for hd in range(RET_HEADS):
            c0 = 128 * hd
            o = outs[j, hd]
            gate = proj_ref[rows[j], OFF_RG + c0:OFF_RG + c0 + 128]
            mu = jnp.mean(o, axis=-1, keepdims=True)
            d = o - mu
            var = jnp.mean(d * d, axis=-1, keepdims=True)
            on = d * lax.rsqrt(var + EPS) * rg_ref[layer:layer + 1, c0:c0 + 128]
            y = gate * (1.0 / (1.0 + jnp.exp(-gate))) * on
            mixed_ref[rows[j], ATTN_WIDTH + c0:ATTN_WIDTH + c0 + 128] = y.astype(BF16)
            if hd % 2 == 1:
                issue_chunk()

    assert not chunks


def _mix_kernel(layer, tiles_per_seq, n_pairs,
                xn_ref, xr_ref, ada_ref, g1_ref, win_f32_ref, gq_ref, gk_ref, sinks_ref,
                rg_ref, wout_f32_ref, bias_ref, ret_ref, cdec_ref, w1_f32_ref, w2_f32_ref,
                o_ref, w1_bf16_ref, w2_bf16_ref,
                proj_a, proj_b, mixed_a, mixed_b, hn_ref, kwin_a, kwin_b, vwin_a, vwin_b, s_ref,
                win_ref, wout_ref):
    i = pl.program_id(0)
    pairs_per_seq = tiles_per_seq // 2
    ada_n = pl.ds(jnp.minimum(i, n_pairs - 1) // pairs_per_seq, 1)
    ada_r = pl.ds(jnp.maximum(i - 1, 0) // pairs_per_seq, 1)

    @pl.when(i == 0)
    def _():
        win_ref[...] = win_f32_ref[...].astype(BF16)
        wout_ref[...] = wout_f32_ref[...].astype(BF16)

    w1_bf16_ref[...] = w1_f32_ref[...].astype(BF16)
    w2_bf16_ref[...] = w2_f32_ref[...].astype(BF16)

    @pl.when(i == 0)
    def _():
        proj_b[...] = jnp.zeros_like(proj_b)
        mixed_a[...] = jnp.zeros_like(mixed_a)
        kwin_a[...] = jnp.zeros_like(kwin_a)
        kwin_b[...] = jnp.zeros_like(kwin_b)
        vwin_a[...] = jnp.zeros_like(vwin_a)
        vwin_b[...] = jnp.zeros_like(vwin_b)
        s_ref[...] = jnp.zeros_like(s_ref)

    consts = (layer, gq_ref, sinks_ref, rg_ref, bias_ref, ret_ref, cdec_ref, s_ref)

    def slot(half, proj_cur, proj_next, mixed_cur, mixed_prev, kwin, kwin_next, vwin, vwin_next,
             seq_start):
        def out_chunk(c):
            cols = slice(c * PROJ_CHUNK, min((c + 1) * PROJ_CHUNK, D_MODEL))
            y = jnp.dot(mixed_prev[...], wout_ref[:, cols], preferred_element_type=F32)
            gate = ada_ref[ada_r, 2 * D_MODEL + cols.start:2 * D_MODEL + cols.stop]
            o_ref[half, :, cols] = xr_ref[half, :, cols] + gate * y

        def norm_next():
            x = xn_ref[half]
            ms = jnp.mean(x * x, axis=-1, keepdims=True)
            h = x * lax.rsqrt(ms + EPS) * g1_ref[layer:layer + 1, :]
            shift, scale = ada_ref[ada_n, 0:D_MODEL], ada_ref[ada_n, D_MODEL:2 * D_MODEL]
            hn_ref[...] = (h * (1.0 + scale) + shift).astype(BF16)

        def in_chunk(c):
            cols = slice(c * PROJ_CHUNK, min((c + 1) * PROJ_CHUNK, IN_WIDTH))
            proj_next[:, cols] = jnp.dot(hn_ref[...], win_ref[:, cols], preferred_element_type=F32)
            if cols.start <= OFF_AK and OFF_AV + 2 * HEAD_DIM <= cols.stop:
                _build_windows(proj_next, _pair_gain(gk_ref, layer), kwin_next, kwin, vwin_next, vwin)

        def first_chunk():
            out_chunk(0)
            norm_next()
            out_chunk(1)

        chunks = [first_chunk]
        chunks += [functools.partial(out_chunk, c) for c in range(2, pl.cdiv(D_MODEL, PROJ_CHUNK))]
        chunks += [functools.partial(in_chunk, c) for c in range(pl.cdiv(IN_WIDTH, PROJ_CHUNK))]
        _mix_slot(chunks, proj_cur, mixed_cur, kwin, vwin, seq_start, *consts)

    slot(0, proj_b, proj_a, mixed_b, mixed_a, kwin_b, kwin_a, vwin_b, vwin_a, False)
    slot(1, proj_a, proj_b, mixed_a, mixed_b, kwin_a, kwin_b, vwin_a, vwin_b,
         (2 * i) % tiles_per_seq == 0)


def _mix_call(layer, x, ada, g1, w_in, gq, gk, sinks, rg, w_out, tables, w_mlp1, w_mlp2):
    b, s, d = x.shape
    tiles_per_seq = s // MIX_ROWS
    pairs_per_seq = tiles_per_seq // 2
    n_pairs = b * pairs_per_seq
    bias_tab, ret_tab, cdec = tables
    x_pairs = x.reshape(n_pairs, 2, MIX_ROWS, d)

    nxt = lambda i: jnp.minimum(i, n_pairs - 1)
    res = lambda i: jnp.maximum(i - 1, 0)
    const2 = lambda i: (0, 0)
    const4 = lambda i: (0, 0, 0, 0)
    single = pl.Buffered(1)
    smem = pl.BlockSpec(memory_space=pltpu.SMEM)
    slab1, slab2 = w_mlp1.shape[1] // n_pairs, w_mlp2.shape[1] // n_pairs
    lay3 = lambda i: (layer, 0, 0)
    out, w1_b, w2_b = pl.pallas_call(
        functools.partial(_mix_kernel, layer, tiles_per_seq, n_pairs),
        grid=(n_pairs + 1,),
        in_specs=[
            pl.BlockSpec((None, 2, MIX_ROWS, d), lambda i: (nxt(i), 0, 0, 0)),
            pl.BlockSpec((None, 2, MIX_ROWS, d), lambda i: (res(i), 0, 0, 0)),
            pl.BlockSpec((None,) + ada.shape[1:], lambda i: (layer, 0, 0)),
            pl.BlockSpec(g1.shape, const2),
            pl.BlockSpec((None, d, IN_WIDTH), lay3, pipeline_mode=single),
            pl.BlockSpec(gq.shape, const2),
            pl.BlockSpec(gk.shape, const2),
            smem,
            pl.BlockSpec(rg.shape, const2),
            pl.BlockSpec((None, d, d), lay3, pipeline_mode=single),
            pl.BlockSpec(bias_tab.shape, const4, pipeline_mode=single),
            pl.BlockSpec(ret_tab.shape, const4, pipeline_mode=single),
            smem,
            pl.BlockSpec((None, slab1, D_FF), lambda i: (layer, nxt(i), 0)),
            pl.BlockSpec((None, slab2, d), lambda i: (layer, nxt(i), 0)),
        ],
        out_specs=[
            pl.BlockSpec((None, 2, MIX_ROWS, d), lambda i: (res(i), 0, 0, 0)),
            pl.BlockSpec((slab1, D_FF), lambda i: (nxt(i), 0)),
            pl.BlockSpec((slab2, d), lambda i: (nxt(i), 0)),
        ],
        out_shape=[
            jax.ShapeDtypeStruct((n_pairs, 2, MIX_ROWS, d), F32),
            jax.ShapeDtypeStruct(w_mlp1.shape[1:], BF16),
            jax.ShapeDtypeStruct(w_mlp2.shape[1:], BF16),
        ],
        scratch_shapes=[
            pltpu.VMEM((MIX_ROWS, IN_WIDTH), F32),
            pltpu.VMEM((MIX_ROWS, IN_WIDTH), F32),
            pltpu.VMEM((MIX_ROWS, d), BF16),
            pltpu.VMEM((MIX_ROWS, d), BF16),
            pltpu.VMEM((MIX_ROWS, d), BF16),
            pltpu.VMEM((N_KV_HEADS, BLOCK + MIX_ROWS, 2 * HEAD_DIM), BF16),
            pltpu.VMEM((N_KV_HEADS, BLOCK + MIX_ROWS, 2 * HEAD_DIM), BF16),
            pltpu.VMEM((2 * HEAD_DIM, BLOCK + MIX_ROWS), BF16),
            pltpu.VMEM((2 * HEAD_DIM, BLOCK + MIX_ROWS), BF16),
            pltpu.VMEM((RET_HEADS, 128, 128), F32),
            pltpu.VMEM((d, IN_WIDTH), BF16),
            pltpu.VMEM((d, d), BF16),
        ],
        compiler_params=pltpu.CompilerParams(
            dimension_semantics=("arbitrary",), vmem_limit_bytes=VMEM_LIMIT),
        name="mix",
    )(x_pairs, x_pairs, ada, g1, w_in, gq, gk, sinks, rg, w_out, bias_tab, ret_tab, cdec,
      w_mlp1, w_mlp2)
    return out.reshape(b, s, d), w1_b, w2_b


MLP_ROWS = 1024


def _mlp_kernel(layer, x_ref, ada_ref, g2_ref, w1_ref, w2_ref, o_ref):
    x = x_ref[...]
    ms = jnp.mean(x * x, axis=-1, keepdims=True)
    h = x * lax.rsqrt(ms + EPS) * g2_ref[layer:layer + 1, :]
    row = pl.ds(pl.program_id(0), 1)
    shift, scale, gate = (ada_ref[row, k * D_MODEL:(k + 1) * D_MODEL] for k in (3, 4, 5))
    h = (h * (1.0 + scale) + shift).astype(BF16)
    acc = None
    for c in range(D_FF // FF_CHUNK):
        cols = slice(c * FF_CHUNK, (c + 1) * FF_CHUNK)
        a = jnp.dot(h, w1_ref[:, cols], preferred_element_type=F32)
        a = jnp.maximum(a, 0.0)
        part = jnp.dot((a * a).astype(BF16), w2_ref[cols, :], preferred_element_type=F32)
        acc = part if acc is None else acc + part
    o_ref[...] = x + gate * acc


def _mlp_call(layer, x, ada, g2, w1, w2):
    b, s, d = x.shape
    const2 = lambda i, j: (0, 0)
    single = pl.Buffered(1)
    return pl.pallas_call(
        functools.partial(_mlp_kernel, layer),
        grid=(b, s // MLP_ROWS),
        in_specs=[
            pl.BlockSpec((None, MLP_ROWS, d), lambda i, j: (i, j, 0)),
            pl.BlockSpec((None,) + ada.shape[1:], lambda i, j: (layer, 0, 0)),
            pl.BlockSpec(g2.shape, const2),
            pl.BlockSpec((d, D_FF), const2, pipeline_mode=single),
            pl.BlockSpec((D_FF, d), const2, pipeline_mode=single),
        ],
        out_specs=pl.BlockSpec((None, MLP_ROWS, d), lambda i, j: (i, j, 0)),
        out_shape=jax.ShapeDtypeStruct((b, s, d), F32),
        compiler_params=pltpu.CompilerParams(
            dimension_semantics=("arbitrary", "arbitrary"), vmem_limit_bytes=VMEM_LIMIT),
        name="mlp",
    )(x, ada, g2, w1, w2)


def kernel(x, c, norm1_g, norm2_g, w_ada, b_ada, w_in, q_norm_g, k_norm_g, sinks, ret_norm_g,
           w_out, w_mlp1, w_mlp2):
    tables = tuple(jnp.asarray(t) for t in _constant_tables())
    ada = _ada_call(c, w_ada, b_ada)
    for l in range(DEPTH):
        x, w1_b, w2_b = _mix_call(l, x, ada, norm1_g, w_in, q_norm_g, k_norm_g, sinks, ret_norm_g,
                                  w_out, tables, w_mlp1, w_mlp2)
        x = _mlp_call(l, x, ada, norm2_g, w1_b, w2_b)
    return x
```

```python
import functools

import numpy as np
import jax
import jax.numpy as jnp
from jax import lax
from jax.experimental import pallas as pl
from jax.experimental.pallas import tpu as pltpu

D_MODEL = 1024
DEPTH = 2
ATTN_WIDTH = 512
RET_WIDTH = 512
HEAD_DIM = 64
N_Q_HEADS = 8
N_KV_HEADS = 2
N_PAIRS = N_Q_HEADS // 2
BLOCK = 128
RET_HEADS = 4
RET_DK = 128
D_FF = 4 * D_MODEL
EPS = 1e-6
NEG_INF = -1e30
LOG2E = 1.4426950408889634
IN_WIDTH = 2816
OFF_AQ, OFF_AK, OFF_AV, OFF_RQ, OFF_RK, OFF_RV, OFF_RG = 0, 512, 640, 768, 1280, 1792, 2304

MIX_ROWS = 256
PROJ_CHUNK = 256
FF_CHUNK = 1024
ADA_COLS = 1536
VMEM_LIMIT = 60 * 1024 * 1024

F32 = jnp.float32
BF16 = jnp.bfloat16


@functools.lru_cache(maxsize=None)
def _constant_tables():
    q_pos = np.arange(BLOCK)[:, None]
    k_pos = np.arange(2 * BLOCK)[None, :]
    dist = q_pos + BLOCK - k_pos
    valid = (dist >= 0) & (dist < BLOCK)
    valid_first = valid & (k_pos >= BLOCK)
    slopes = np.exp2(-8.0 * np.arange(1, N_Q_HEADS + 1, dtype=np.float64) / N_Q_HEADS)
    bias = -slopes[:, None, None] * dist[None].astype(np.float64)
    bias_tab = np.stack([np.where(v[None], bias * LOG2E, NEG_INF) for v in (valid_first, valid)])
    bias_tab = np.ascontiguousarray(bias_tab.transpose(0, 1, 3, 2), np.float32)

    log_gamma = np.log1p(-np.exp2(-5.0 - np.arange(RET_HEADS, dtype=np.float64)))
    idx = np.arange(BLOCK, dtype=np.float64)
    rel = idx[:, None] - idx[None, :]
    k_scale = RET_DK ** -0.5
    decay_in = np.where(rel >= 0, np.exp(log_gamma[:, None, None] * np.maximum(rel, 0.0)), 0.0) * k_scale
    q_decay = np.exp(log_gamma[:, None] * (idx[None, :] + 1.0))
    k_decay = np.exp(log_gamma[:, None] * (BLOCK - 1.0 - idx[None, :])) * k_scale
    q_decay = np.broadcast_to(q_decay[:, :, None], (RET_HEADS, BLOCK, BLOCK))
    k_decay = np.broadcast_to(k_decay[:, :, None], (RET_HEADS, BLOCK, BLOCK))
    chunk_decay = np.exp(log_gamma * BLOCK)
    ret_tab = np.stack([decay_in, q_decay, k_decay]).astype(np.float32)
    return bias_tab, ret_tab, chunk_decay.astype(np.float32)


def _ada_kernel(c_ref, w_ref, b_ref, o_ref):
    c = c_ref[...]
    c_act = c * (1.0 / (1.0 + jnp.exp(-c)))
    o_ref[...] = jnp.dot(c_act.astype(BF16), w_ref[...].astype(BF16),
                         preferred_element_type=F32) + b_ref[pl.ds(pl.program_id(0), 1), :]


def _ada_call(c, w_ada, b_ada):
    rows = c.shape[0]
    n = w_ada.shape[-1]
    return pl.pallas_call(
        _ada_kernel,
        grid=(DEPTH, n // ADA_COLS),
        in_specs=[
            pl.BlockSpec((rows, D_MODEL), lambda l, j: (0, 0)),
            pl.BlockSpec((None, D_MODEL, ADA_COLS), lambda l, j: (l, 0, j)),
            pl.BlockSpec((DEPTH, ADA_COLS), lambda l, j: (0, j)),
        ],
        out_specs=pl.BlockSpec((None, rows, ADA_COLS), lambda l, j: (l, 0, j)),
        out_shape=jax.ShapeDtypeStruct((DEPTH, rows, n), F32),
        compiler_params=pltpu.CompilerParams(
            dimension_semantics=("arbitrary", "arbitrary"), vmem_limit_bytes=VMEM_LIMIT),
        name="ada",
    )(c, w_ada, b_ada)


def _head_norm(a, gain_row):
    lo = lax.broadcasted_iota(jnp.int32, a.shape, 1) < HEAD_DIM
    a2 = a * a
    s_lo = jnp.sum(jnp.where(lo, a2, 0.0), axis=-1, keepdims=True)
    s_hi = jnp.sum(jnp.where(lo, 0.0, a2), axis=-1, keepdims=True)
    mean_sq = jnp.where(lo, s_lo, s_hi) * (1.0 / HEAD_DIM)
    return a * lax.rsqrt(mean_sq + EPS) * gain_row


def _pair_gain(g_ref, layer):
    row = g_ref[layer:layer + 1, :]
    return jnp.concatenate([row, row], axis=1)


def _build_windows(proj_ref, gk_row, kwin_ref, kwin_prev, vwin_ref, vwin_prev):
    kn = _head_norm(proj_ref[:, OFF_AK:OFF_AK + 128], gk_row)
    kn_rot = pltpu.roll(kn, HEAD_DIM, axis=1)
    lo = lax.broadcasted_iota(jnp.int32, kn.shape, 1) < HEAD_DIM
    k_dup = (jnp.where(lo, kn, kn_rot), jnp.where(lo, kn_rot, kn))
    for g in range(N_KV_HEADS):
        kwin_ref[g, 0:BLOCK, :] = kwin_prev[g, MIX_ROWS:MIX_ROWS + BLOCK, :]
        kwin_ref[g, BLOCK:BLOCK + MIX_ROWS, :] = k_dup[g].astype(BF16)
    vwin_ref[:, 0:BLOCK] = vwin_prev[:, MIX_ROWS:MIX_ROWS + BLOCK]
    vwin_ref[:, BLOCK:BLOCK + MIX_ROWS] = proj_ref[:, OFF_AV:OFF_AV + 128].T.astype(BF16)


def _mix_slot(chunks, proj_ref, mixed_ref, kwin_ref, vwin_ref,
              seq_start, layer, gq_ref, sinks_ref, rg_ref,
              bias_ref, ret_ref, cdec_ref, s_ref):
    first_tab = 1 if seq_start is False else jnp.where(seq_start, 0, 1)
    chunks = list(chunks)
    n_chunks = len(chunks)
    n_points = 5 + 5 * (MIX_ROWS // BLOCK)
    point = [0]

    def issue_chunk():
        point[0] += 1
        while chunks and (n_chunks - len(chunks)) * n_points < point[0] * n_chunks:
            chunks.pop(0)()

    issue_chunk()

    lo = lax.broadcasted_iota(jnp.int32, (BLOCK, 2 * HEAD_DIM), 1) < HEAD_DIM
    gq = _pair_gain(gq_ref, layer) * (HEAD_DIM ** -0.5 * LOG2E)
    n_blocks = MIX_ROWS // BLOCK
    rows = [pl.ds(j * BLOCK, BLOCK) for j in range(n_blocks)]
    wins = [slice(j * BLOCK, (j + 2) * BLOCK) for j in range(n_blocks)]
    tabs = [first_tab if j == 0 else 1 for j in range(n_blocks)]

    s_t = {}
    for j in range(n_blocks):
        for g in range(N_KV_HEADS):
            qs = []
            for pp in range(2):
                p = 2 * g + pp
                qn = _head_norm(proj_ref[rows[j], OFF_AQ + 128 * p:OFF_AQ + 128 * p + 128], gq)
                qs.append(jnp.where(lo, qn, 0.0).astype(BF16))
                qs.append(jnp.where(lo, 0.0, qn).astype(BF16))
            s_t[j, g] = lax.dot_general(kwin_ref[g, wins[j], :], jnp.concatenate(qs, axis=0),
                                        (((1,), (1,)), ((), ())),
                                        preferred_element_type=F32)
    issue_chunk()
    sinks = [sinks_ref[layer, head] * LOG2E for head in range(N_Q_HEADS)]
    es, ms = {}, {}
    for j in range(n_blocks):
        for head in range(N_Q_HEADS):
            g, hh = divmod(head, 4)
            sh = s_t[j, g][:, 128 * hh:128 * hh + 128] + bias_ref[tabs[j], head]
            ms[j, head] = jnp.maximum(jnp.max(sh, axis=0, keepdims=True), sinks[head])
            es[j, head] = jnp.exp2(sh - ms[j, head]).astype(BF16)
            if head % 4 == 3:
                issue_chunk()
    ones = jnp.ones((HEAD_DIM, 2 * BLOCK), BF16)
    o_t = {}
    for j in range(n_blocks):
        for g in range(N_KV_HEADS):
            lhs = jnp.concatenate([vwin_ref[64 * g:64 * g + 64, wins[j]], ones], axis=0)
            p_t = jnp.concatenate([es[j, 4 * g + hh] for hh in range(4)], axis=1)
            o_t[j, g] = jnp.dot(lhs, p_t, preferred_element_type=F32)
    issue_chunk()
    for j in range(n_blocks):
        for p in range(N_PAIRS):
            g, pp = divmod(p, 2)
            halves = []
            for head in (2 * p, 2 * p + 1):
                cols = slice(128 * (head % 4), 128 * (head % 4) + 128)
                denom = o_t[j, g][HEAD_DIM:, cols] + jnp.exp2(sinks[head] - ms[j, head])
                halves.append(o_t[j, g][:HEAD_DIM, cols] / denom)
            o = jnp.concatenate(halves, axis=0).T
            mixed_ref[rows[j], 128 * p:128 * p + 128] = o.astype(BF16)
        issue_chunk()

    heads = {}
    for j in range(n_blocks):
        for hd in range(RET_HEADS):
            c0 = 128 * hd
            q_f = proj_ref[rows[j], OFF_RQ + c0:OFF_RQ + c0 + 128]
            k = proj_ref[rows[j], OFF_RK + c0:OFF_RK + c0 + 128]
            v = proj_ref[rows[j], OFF_RV + c0:OFF_RV + c0 + 128].astype(BF16)
            inner = lax.dot_general(q_f.astype(BF16), k.astype(BF16), (((1,), (1,)), ((), ())),
                                    preferred_element_type=F32)
            kd = (k * ret_ref[2, hd]).astype(BF16)
            kv = lax.dot_general(kd, v, (((0,), (0,)), ((), ())), preferred_element_type=F32)
            heads[j, hd] = (q_f, v, inner, kv)
    issue_chunk()
    outs = {}
    for j in range(n_blocks):
        for hd in range(RET_HEADS):
            q_f, v, inner, kv = heads[j, hd]
            state = s_ref[hd]
            if j == 0 and seq_start is not False:
                state = jnp.where(seq_start, 0.0, state)
            lhs = jnp.concatenate([(inner * ret_ref[0, hd]).astype(BF16), (q_f * ret_ref[1, hd]).astype(BF16)],
                                  axis=1)
            outs[j, hd] = jnp.dot(lhs, jnp.concatenate([v, state.astype(BF16)], axis=0),
                                  preferred_element_type=F32)
            s_ref[hd] = cdec_ref[hd] * state + kv
    issue_chunk()
    for j in range(n_blocks):
        for hd in range(RET_HEADS):
            c0 = 128 * hd
            o = outs[j, hd]
            gate = proj_ref[rows[j], OFF_RG + c0:OFF_RG + c0 + 128]
            mu = jnp.mean(o, axis=-1, keepdims=True)
            d = o - mu
            var = jnp.mean(d * d, axis=-1, keepdims=True)
            on = d * lax.rsqrt(var + EPS) * rg_ref[layer:layer + 1, c0:c0 + 128]
            y = gate * (1.0 / (1.0 + jnp.exp(-gate))) * on
            mixed_ref[rows[j], ATTN_WIDTH + c0:ATTN_WIDTH + c0 + 128] = y.astype(BF16)
            if hd % 2 == 1:
                issue_chunk()

    assert not chunks


def _mix_kernel(layer, tiles_per_seq, n_pairs,
                xn_ref, ada_ref, g1_ref, win_f32_ref, gq_ref, gk_ref, sinks_ref,
                rg_ref, wout_f32_ref, bias_ref, ret_ref, cdec_ref, w1_f32_ref, w2_f32_ref,
                o_ref, w1_bf16_ref, w2_bf16_ref,
                proj_a, proj_b, mixed_a, mixed_b, hn_ref, xres_ref, kwin_a, kwin_b, vwin_a, vwin_b, s_ref,
                win_ref, wout_ref):
    i = pl.program_id(0)
    pairs_per_seq = tiles_per_seq // 2
    ada_n = pl.ds(jnp.minimum(i, n_pairs - 1) // pairs_per_seq, 1)
    ada_r = pl.ds(jnp.maximum(i - 1, 0) // pairs_per_seq, 1)

    @pl.when(i == 0)
    def _():
        win_ref[...] = win_f32_ref[...].astype(BF16)
        wout_ref[...] = wout_f32_ref[...].astype(BF16)

    w1_bf16_ref[...] = w1_f32_ref[...].astype(BF16)
    w2_bf16_ref[...] = w2_f32_ref[...].astype(BF16)

    @pl.when(i == 0)
    def _():
        proj_b[...] = jnp.zeros_like(proj_b)
        mixed_a[...] = jnp.zeros_like(mixed_a)
        xres_ref[...] = jnp.zeros_like(xres_ref)
        kwin_a[...] = jnp.zeros_like(kwin_a)
        kwin_b[...] = jnp.zeros_like(kwin_b)
        vwin_a[...] = jnp.zeros_like(vwin_a)
        vwin_b[...] = jnp.zeros_like(vwin_b)
        s_ref[...] = jnp.zeros_like(s_ref)

    consts = (layer, gq_ref, sinks_ref, rg_ref, bias_ref, ret_ref, cdec_ref, s_ref)

    def slot(half, proj_cur, proj_next, mixed_cur, mixed_prev, kwin, kwin_next, vwin, vwin_next,
             seq_start):
        def out_chunk(c):
            cols = slice(c * PROJ_CHUNK, min((c + 1) * PROJ_CHUNK, D_MODEL))
            y = jnp.dot(mixed_prev[...], wout_ref[:, cols], preferred_element_type=F32)
            gate = ada_ref[ada_r, 2 * D_MODEL + cols.start:2 * D_MODEL + cols.stop]
            o_ref[half, :, cols] = xres_ref[half, :, cols] + gate * y
            xres_ref[half, :, cols] = xn_ref[half, :, cols]

        def norm_next():
            x = xn_ref[half]
            ms = jnp.mean(x * x, axis=-1, keepdims=True)
            h = x * lax.rsqrt(ms + EPS) * g1_ref[layer:layer + 1, :]
            shift, scale = ada_ref[ada_n, 0:D_MODEL], ada_ref[ada_n, D_MODEL:2 * D_MODEL]
            hn_ref[...] = (h * (1.0 + scale) + shift).astype(BF16)

        def in_chunk(c):
            cols = slice(c * PROJ_CHUNK, min((c + 1) * PROJ_CHUNK, IN_WIDTH))
            proj_next[:, cols] = jnp.dot(hn_ref[...], win_ref[:, cols], preferred_element_type=F32)
            if cols.start <= OFF_AK and OFF_AV + 2 * HEAD_DIM <= cols.stop:
                _build_windows(proj_next, _pair_gain(gk_ref, layer), kwin_next, kwin, vwin_next, vwin)

        def first_chunk():
            out_chunk(0)
            norm_next()
            out_chunk(1)

        chunks = [first_chunk]
        chunks += [functools.partial(out_chunk, c) for c in range(2, pl.cdiv(D_MODEL, PROJ_CHUNK))]
        chunks += [functools.partial(in_chunk, c) for c in range(pl.cdiv(IN_WIDTH, PROJ_CHUNK))]
        _mix_slot(chunks, proj_cur, mixed_cur, kwin, vwin, seq_start, *consts)

    slot(0, proj_b, proj_a, mixed_b, mixed_a, kwin_b, kwin_a, vwin_b, vwin_a, False)
    slot(1, proj_a, proj_b, mixed_a, mixed_b, kwin_a, kwin_b, vwin_a, vwin_b,
         (2 * i) % tiles_per_seq == 0)


def _mix_call(layer, x, ada, g1, w_in, gq, gk, sinks, rg, w_out, tables, w_mlp1, w_mlp2):
    b, s, d = x.shape
    tiles_per_seq = s // MIX_ROWS
    pairs_per_seq = tiles_per_seq // 2
    n_pairs = b * pairs_per_seq
    bias_tab, ret_tab, cdec = tables
    x_pairs = x.reshape(n_pairs, 2, MIX_ROWS, d)

    nxt = lambda i: jnp.minimum(i, n_pairs - 1)
    res = lambda i: jnp.maximum(i - 1, 0)
    const2 = lambda i: (0, 0)
    const4 = lambda i: (0, 0, 0, 0)
    single = pl.Buffered(1)
    smem = pl.BlockSpec(memory_space=pltpu.SMEM)
    slab1, slab2 = w_mlp1.shape[1] // n_pairs, w_mlp2.shape[1] // n_pairs
    lay3 = lambda i: (layer, 0, 0)
    out, w1_b, w2_b = pl.pallas_call(
        functools.partial(_mix_kernel, layer, tiles_per_seq, n_pairs),
        grid=(n_pairs + 1,),
        in_specs=[
            pl.BlockSpec((None, 2, MIX_ROWS, d), lambda i: (nxt(i), 0, 0, 0)),
            pl.BlockSpec((None,) + ada.shape[1:], lambda i: (layer, 0, 0)),
            pl.BlockSpec(g1.shape, const2),
            pl.BlockSpec((None, d, IN_WIDTH), lay3, pipeline_mode=single),
            pl.BlockSpec(gq.shape, const2),
            pl.BlockSpec(gk.shape, const2),
            smem,
            pl.BlockSpec(rg.shape, const2),
            pl.BlockSpec((None, d, d), lay3, pipeline_mode=single),
            pl.BlockSpec(bias_tab.shape, const4, pipeline_mode=single),
            pl.BlockSpec(ret_tab.shape, const4, pipeline_mode=single),
            smem,
            pl.BlockSpec((None, slab1, D_FF), lambda i: (layer, nxt(i), 0)),
            pl.BlockSpec((None, slab2, d), lambda i: (layer, nxt(i), 0)),
        ],
        out_specs=[
            pl.BlockSpec((None, 2, MIX_ROWS, d), lambda i: (res(i), 0, 0, 0)),
            pl.BlockSpec((slab1, D_FF), lambda i: (nxt(i), 0)),
            pl.BlockSpec((slab2, d), lambda i: (nxt(i), 0)),
        ],
        out_shape=[
            jax.ShapeDtypeStruct((n_pairs, 2, MIX_ROWS, d), F32),
            jax.ShapeDtypeStruct(w_mlp1.shape[1:], BF16),
            jax.ShapeDtypeStruct(w_mlp2.shape[1:], BF16),
        ],
        scratch_shapes=[
            pltpu.VMEM((MIX_ROWS, IN_WIDTH), F32),
            pltpu.VMEM((MIX_ROWS, IN_WIDTH), F32),
            pltpu.VMEM((MIX_ROWS, d), BF16),
            pltpu.VMEM((MIX_ROWS, d), BF16),
            pltpu.VMEM((MIX_ROWS, d), BF16),
            pltpu.VMEM((2, MIX_ROWS, d), F32),
            pltpu.VMEM((N_KV_HEADS, BLOCK + MIX_ROWS, 2 * HEAD_DIM), BF16),
            pltpu.VMEM((N_KV_HEADS, BLOCK + MIX_ROWS, 2 * HEAD_DIM), BF16),
            pltpu.VMEM((2 * HEAD_DIM, BLOCK + MIX_ROWS), BF16),
            pltpu.VMEM((2 * HEAD_DIM, BLOCK + MIX_ROWS), BF16),
            pltpu.VMEM((RET_HEADS, 128, 128), F32),
            pltpu.VMEM((d, IN_WIDTH), BF16),
            pltpu.VMEM((d, d), BF16),
        ],
        compiler_params=pltpu.CompilerParams(
            dimension_semantics=("arbitrary",), vmem_limit_bytes=VMEM_LIMIT),
        name="mix",
    )(x_pairs, ada, g1, w_in, gq, gk, sinks, rg, w_out, bias_tab, ret_tab, cdec,
      w_mlp1, w_mlp2)
    return out.reshape(b, s, d), w1_b, w2_b


MLP_ROWS = 1024


def _mlp_kernel(layer, x_ref, ada_ref, g2_ref, w1_ref, w2_ref, o_ref):
    x = x_ref[...]
    ms = jnp.mean(x * x, axis=-1, keepdims=True)
    h = x * lax.rsqrt(ms + EPS) * g2_ref[layer:layer + 1, :]
    row = pl.ds(pl.program_id(0), 1)
    shift, scale, gate = (ada_ref[row, k * D_MODEL:(k + 1) * D_MODEL] for k in (3, 4, 5))
    h = (h * (1.0 + scale) + shift).astype(BF16)
    acc = None
    for c in range(D_FF // FF_CHUNK):
        cols = slice(c * FF_CHUNK, (c + 1) * FF_CHUNK)
        a = jnp.dot(h, w1_ref[:, cols], preferred_element_type=F32)
        a = jnp.maximum(a, 0.0)
        part = jnp.dot((a * a).astype(BF16), w2_ref[cols, :], preferred_element_type=F32)
        acc = part if acc is None else acc + part
    o_ref[...] = x + gate * acc


def _mlp_call(layer, x, ada, g2, w1, w2):
    b, s, d = x.shape
    const2 = lambda i, j: (0, 0)
    single = pl.Buffered(1)
    return pl.pallas_call(
        functools.partial(_mlp_kernel, layer),
        grid=(b, s // MLP_ROWS),
        in_specs=[
            pl.BlockSpec((None, MLP_ROWS, d), lambda i, j: (i, j, 0)),
            pl.BlockSpec((None,) + ada.shape[1:], lambda i, j: (layer, 0, 0)),
            pl.BlockSpec(g2.shape, const2),
            pl.BlockSpec((d, D_FF), const2, pipeline_mode=single),
            pl.BlockSpec((D_FF, d), const2, pipeline_mode=single),
        ],
        out_specs=pl.BlockSpec((None, MLP_ROWS, d), lambda i, j: (i, j, 0)),
        out_shape=jax.ShapeDtypeStruct((b, s, d), F32),
        compiler_params=pltpu.CompilerParams(
            dimension_semantics=("arbitrary", "arbitrary"), vmem_limit_bytes=VMEM_LIMIT),
        name="mlp",
    )(x, ada, g2, w1, w2)


def kernel(x, c, norm1_g, norm2_g, w_ada, b_ada, w_in, q_norm_g, k_norm_g, sinks, ret_norm_g,
           w_out, w_mlp1, w_mlp2):
    tables = tuple(jnp.asarray(t) for t in _constant_tables())
    ada = _ada_call(c, w_ada, b_ada)
    for l in range(DEPTH):
        x, w1_b, w2_b = _mix_call(l, x, ada, norm1_g, w_in, q_norm_g, k_norm_g, sinks, ret_norm_g,
                                  w_out, tables, w_mlp1, w_mlp2)
        x = _mlp_call(l, x, ada, norm2_g, w1_b, w2_b)
    return x
```

```python
import functools

import numpy as np
import jax
import jax.numpy as jnp
from jax import lax
from jax.experimental import pallas as pl
from jax.experimental.pallas import tpu as pltpu

D_MODEL = 1024
DEPTH = 2
ATTN_WIDTH = 512
RET_WIDTH = 512
HEAD_DIM = 64
N_Q_HEADS = 8
N_KV_HEADS = 2
N_PAIRS = N_Q_HEADS // 2
BLOCK = 128
RET_HEADS = 4
RET_DK = 128
D_FF = 4 * D_MODEL
EPS = 1e-6
NEG_INF = -1e30
LOG2E = 1.4426950408889634
IN_WIDTH = 2816
OFF_AQ, OFF_AK, OFF_AV, OFF_RQ, OFF_RK, OFF_RV, OFF_RG = 0, 512, 640, 768, 1280, 1792, 2304

MIX_ROWS = 256
PROJ_CHUNK = 256
FF_CHUNK = 1024
ADA_COLS = 1536
VMEM_LIMIT = 60 * 1024 * 1024

F32 = jnp.float32
BF16 = jnp.bfloat16


@functools.lru_cache(maxsize=None)
def _constant_tables():
    q_pos = np.arange(BLOCK)[:, None]
    k_pos = np.arange(2 * BLOCK)[None, :]
    dist = q_pos + BLOCK - k_pos
    valid = (dist >= 0) & (dist < BLOCK)
    valid_first = valid & (k_pos >= BLOCK)
    slopes = np.exp2(-8.0 * np.arange(1, N_Q_HEADS + 1, dtype=np.float64) / N_Q_HEADS)
    bias = -slopes[:, None, None] * dist[None].astype(np.float64)
    bias_tab = np.stack([np.where(v[None], bias * LOG2E, NEG_INF) for v in (valid_first, valid)])
    bias_tab = np.ascontiguousarray(bias_tab.transpose(0, 1, 3, 2), np.float32)

    log_gamma = np.log1p(-np.exp2(-5.0 - np.arange(RET_HEADS, dtype=np.float64)))
    idx = np.arange(BLOCK, dtype=np.float64)
    rel = idx[:, None] - idx[None, :]
    k_scale = RET_DK ** -0.5
    decay_in = np.where(rel >= 0, np.exp(log_gamma[:, None, None] * np.maximum(rel, 0.0)), 0.0) * k_scale
    q_decay = np.exp(log_gamma[:, None] * (idx[None, :] + 1.0))
    k_decay = np.exp(log_gamma[:, None] * (BLOCK - 1.0 - idx[None, :])) * k_scale
    q_decay = np.broadcast_to(q_decay[:, :, None], (RET_HEADS, BLOCK, BLOCK))
    k_decay = np.broadcast_to(k_decay[:, :, None], (RET_HEADS, BLOCK, BLOCK))
    chunk_decay = np.exp(log_gamma * BLOCK)
    ret_tab = np.stack([decay_in, q_decay, k_decay]).astype(np.float32)
    return bias_tab, ret_tab, chunk_decay.astype(np.float32)


def _ada_kernel(c_ref, w_ref, b_ref, o_ref):
    c = c_ref[...]
    c_act = c * (1.0 / (1.0 + jnp.exp(-c)))
    o_ref[...] = jnp.dot(c_act.astype(BF16), w_ref[...].astype(BF16),
                         preferred_element_type=F32) + b_ref[pl.ds(pl.program_id(0), 1), :]


def _ada_call(c, w_ada, b_ada):
    rows = c.shape[0]
    n = w_ada.shape[-1]
    return pl.pallas_call(
        _ada_kernel,
        grid=(DEPTH, n // ADA_COLS),
        in_specs=[
            pl.BlockSpec((rows, D_MODEL), lambda l, j: (0, 0)),
            pl.BlockSpec((None, D_MODEL, ADA_COLS), lambda l, j: (l, 0, j)),
            pl.BlockSpec((DEPTH, ADA_COLS), lambda l, j: (0, j)),
        ],
        out_specs=pl.BlockSpec((None, rows, ADA_COLS), lambda l, j: (l, 0, j)),
        out_shape=jax.ShapeDtypeStruct((DEPTH, rows, n), F32),
        compiler_params=pltpu.CompilerParams(
            dimension_semantics=("arbitrary", "arbitrary"), vmem_limit_bytes=VMEM_LIMIT),
        name="ada",
    )(c, w_ada, b_ada)


def _head_norm(a, gain_row):
    lo = lax.broadcasted_iota(jnp.int32, a.shape, 1) < HEAD_DIM
    a2 = a * a
    s_lo = jnp.sum(jnp.where(lo, a2, 0.0), axis=-1, keepdims=True)
    s_hi = jnp.sum(jnp.where(lo, 0.0, a2), axis=-1, keepdims=True)
    mean_sq = jnp.where(lo, s_lo, s_hi) * (1.0 / HEAD_DIM)
    return a * lax.rsqrt(mean_sq + EPS) * gain_row


def _pair_gain(g_ref, layer):
    row = g_ref[layer:layer + 1, :]
    return jnp.concatenate([row, row], axis=1)


def _build_windows(proj_ref, gk_row, kwin_ref, kwin_prev, vwin_ref, vwin_prev):
    kn = _head_norm(proj_ref[:, OFF_AK:OFF_AK + 128], gk_row)
    kn_rot = pltpu.roll(kn, HEAD_DIM, axis=1)
    lo = lax.broadcasted_iota(jnp.int32, kn.shape, 1) < HEAD_DIM
    k_dup = (jnp.where(lo, kn, kn_rot), jnp.where(lo, kn_rot, kn))
    for g in range(N_KV_HEADS):
        kwin_ref[g, 0:BLOCK, :] = kwin_prev[g, MIX_ROWS:MIX_ROWS + BLOCK, :]
        kwin_ref[g, BLOCK:BLOCK + MIX_ROWS, :] = k_dup[g].astype(BF16)
    vwin_ref[:, 0:BLOCK] = vwin_prev[:, MIX_ROWS:MIX_ROWS + BLOCK]
    vwin_ref[:, BLOCK:BLOCK + MIX_ROWS] = proj_ref[:, OFF_AV:OFF_AV + 128].T.astype(BF16)


def _mix_slot(chunks, proj_ref, mixed_ref, kwin_ref, vwin_ref,
              seq_start, layer, gq_ref, sinks_ref, rg_ref,
              bias_ref, ret_ref, cdec_ref, s_ref):
    first_tab = 1 if seq_start is False else jnp.where(seq_start, 0, 1)
    chunks = list(chunks)
    n_chunks = len(chunks)
    n_points = 5 + 5 * (MIX_ROWS // BLOCK)
    point = [0]

    def issue_chunk():
        point[0] += 1
        while chunks and (n_chunks - len(chunks)) * n_points < point[0] * n_chunks:
            chunks.pop(0)()

    issue_chunk()

    lo = lax.broadcasted_iota(jnp.int32, (BLOCK, 2 * HEAD_DIM), 1) < HEAD_DIM
    gq = _pair_gain(gq_ref, layer) * (HEAD_DIM ** -0.5 * LOG2E)
    n_blocks = MIX_ROWS // BLOCK
    rows = [pl.ds(j * BLOCK, BLOCK) for j in range(n_blocks)]
    wins = [slice(j * BLOCK, (j + 2) * BLOCK) for j in range(n_blocks)]
    tabs = [first_tab if j == 0 else 1 for j in range(n_blocks)]

    s_t = {}
    for j in range(n_blocks):
        for g in range(N_KV_HEADS):
            qs = []
            for pp in range(2):
                p = 2 * g + pp
                qn = _head_norm(proj_ref[rows[j], OFF_AQ + 128 * p:OFF_AQ + 128 * p + 128], gq)
                qs.append(jnp.where(lo, qn, 0.0).astype(BF16))
                qs.append(jnp.where(lo, 0.0, qn).astype(BF16))
            s_t[j, g] = lax.dot_general(kwin_ref[g, wins[j], :], jnp.concatenate(qs, axis=0),
                                        (((1,), (1,)), ((), ())),
                                        preferred_element_type=F32)
    issue_chunk()
    sinks = [sinks_ref[layer, head] * LOG2E for head in range(N_Q_HEADS)]
    es, ms = {}, {}
    for j in range(n_blocks):
        for head in range(N_Q_HEADS):
            g, hh = divmod(head, 4)
            sh = s_t[j, g][:, 128 * hh:128 * hh + 128] + bias_ref[tabs[j], head]
            ms[j, head] = jnp.maximum(jnp.max(sh, axis=0, keepdims=True), sinks[head])
            es[j, head] = jnp.exp2(sh - ms[j, head]).astype(BF16)
            if head % 4 == 3:
                issue_chunk()
    ones = jnp.ones((HEAD_DIM, 2 * BLOCK), BF16)
    o_t = {}
    for j in range(n_blocks):
        for g in range(N_KV_HEADS):
            lhs = jnp.concatenate([vwin_ref[64 * g:64 * g + 64, wins[j]], ones], axis=0)
            p_t = jnp.concatenate([es[j, 4 * g + hh] for hh in range(4)], axis=1)
            o_t[j, g] = jnp.dot(lhs, p_t, preferred_element_type=F32)
    issue_chunk()
    for j in range(n_blocks):
        for p in range(N_PAIRS):
            g, pp = divmod(p, 2)
            halves = []
            for head in (2 * p, 2 * p + 1):
                cols = slice(128 * (head % 4), 128 * (head % 4) + 128)
                denom = o_t[j, g][HEAD_DIM:, cols] + jnp.exp2(sinks[head] - ms[j, head])
                halves.append(o_t[j, g][:HEAD_DIM, cols] / denom)
            o = jnp.concatenate(halves, axis=0).T
            mixed_ref[rows[j], 128 * p:128 * p + 128] = o.astype(BF16)
        issue_chunk()

    heads = {}
    for j in range(n_blocks):
        for hd in range(RET_HEADS):
            c0 = 128 * hd
            q_f = proj_ref[rows[j], OFF_RQ + c0:OFF_RQ + c0 + 128]
            k = proj_ref[rows[j], OFF_RK + c0:OFF_RK + c0 + 128]
            v = proj_ref[rows[j], OFF_RV + c0:OFF_RV + c0 + 128].astype(BF16)
            inner = lax.dot_general(q_f.astype(BF16), k.astype(BF16), (((1,), (1,)), ((), ())),
                                    preferred_element_type=F32)
            kd = (k * ret_ref[2, hd]).astype(BF16)
            kv = lax.dot_general(kd, v, (((0,), (0,)), ((), ())), preferred_element_type=F32)
            heads[j, hd] = (q_f, v, inner, kv)
    issue_chunk()
    outs = {}
    for j in range(n_blocks):
        for hd in range(RET_HEADS):
            q_f, v, inner, kv = heads[j, hd]
            state = s_ref[hd]
            if j == 0 and seq_start is not False:
                state = jnp.where(seq_start, 0.0, state)
            lhs = jnp.concatenate([(inner * ret_ref[0, hd]).astype(BF16), (q_f * ret_ref[1, hd]).astype(BF16)],
                                  axis=1)
            outs[j, hd] = jnp.dot(lhs, jnp.concatenate([v, state.astype(BF16)], axis=0),
                                  preferred_element_type=F32)
            s_ref[hd] = cdec_ref[hd] * state + kv
    issue_chunk()
    for j in range(n_blocks):
        for hd in range(RET_HEADS):
            c0 = 128 * hd
            o = outs[j, hd]
            gate = proj_ref[rows[j], OFF_RG + c0:OFF_RG + c0 + 128]
            mu = jnp.mean(o, axis=-1, keepdims=True)
            d = o - mu
            var = jnp.mean(d * d, axis=-1, keepdims=True)
            on = d * lax.rsqrt(var + EPS) * rg_ref[layer:layer + 1, c0:c0 + 128]
            y = gate * (1.0 / (1.0 + jnp.exp(-gate))) * on
            mixed_ref[rows[j], ATTN_WIDTH + c0:ATTN_WIDTH + c0 + 128] = y.astype(BF16)
            if hd % 2 == 1:
                issue_chunk()

    assert not chunks


def _mix_kernel(layer, tiles_per_seq, n_pairs,
                xn_ref, ada_ref, g1_ref, win_f32_ref, gq_ref, gk_ref, sinks_ref,
                rg_ref, wout_f32_ref, bias_ref, ret_ref, cdec_ref, w1_f32_ref, w2_f32_ref,
                o_ref, w1_bf16_ref, w2_bf16_ref,
                proj_a, proj_b, mixed_a, mixed_b, hn_ref, xres_ref, kwin_a, kwin_b, vwin_a, vwin_b, s_ref,
                win_ref, wout_ref):
    i = pl.program_id(0)
    pairs_per_seq = tiles_per_seq // 2
    ada_n = pl.ds(jnp.minimum(i, n_pairs - 1) // pairs_per_seq, 1)
    ada_r = pl.ds(jnp.maximum(i - 1, 0) // pairs_per_seq, 1)

    @pl.when(i == 0)
    def _():
        win_ref[...] = win_f32_ref[...].astype(BF16)
        wout_ref[...] = wout_f32_ref[...].astype(BF16)

    w1_bf16_ref[...] = w1_f32_ref[...].astype(BF16)
    w2_bf16_ref[...] = w2_f32_ref[...].astype(BF16)

    @pl.when(i == 0)
    def _():
        proj_b[...] = jnp.zeros_like(proj_b)
        mixed_a[...] = jnp.zeros_like(mixed_a)
        xres_ref[...] = jnp.zeros_like(xres_ref)
        kwin_a[...] = jnp.zeros_like(kwin_a)
        kwin_b[...] = jnp.zeros_like(kwin_b)
        vwin_a[...] = jnp.zeros_like(vwin_a)
        vwin_b[...] = jnp.zeros_like(vwin_b)
        s_ref[...] = jnp.zeros_like(s_ref)

    consts = (layer, gq_ref, sinks_ref, rg_ref, bias_ref, ret_ref, cdec_ref, s_ref)

    def slot(half, proj_cur, proj_next, mixed_cur, mixed_prev, kwin, kwin_next, vwin, vwin_next,
             seq_start):
        def out_chunk(c):
            cols = slice(c * PROJ_CHUNK, min((c + 1) * PROJ_CHUNK, D_MODEL))
            y = jnp.dot(mixed_prev[...], wout_ref[:, cols], preferred_element_type=F32)
            gate = ada_ref[ada_r, 2 * D_MODEL + cols.start:2 * D_MODEL + cols.stop]
            o_ref[half, :, cols] = xres_ref[half, :, cols] + gate * y
            xres_ref[half, :, cols] = xn_ref[half, :, cols]

        def norm_next():
            x = xn_ref[half]
            ms = jnp.mean(x * x, axis=-1, keepdims=True)
            h = x * lax.rsqrt(ms + EPS) * g1_ref[layer:layer + 1, :]
            shift, scale = ada_ref[ada_n, 0:D_MODEL], ada_ref[ada_n, D_MODEL:2 * D_MODEL]
            hn_ref[...] = (h * (1.0 + scale) + shift).astype(BF16)

        def in_chunk(c):
            cols = slice(c * PROJ_CHUNK, min((c + 1) * PROJ_CHUNK, IN_WIDTH))
            proj_next[:, cols] = jnp.dot(hn_ref[...], win_ref[:, cols], preferred_element_type=F32)
            if cols.start <= OFF_AK and OFF_AV + 2 * HEAD_DIM <= cols.stop:
                _build_windows(proj_next, _pair_gain(gk_ref, layer), kwin_next, kwin, vwin_next, vwin)

        def first_chunk():
            out_chunk(0)
            norm_next()
            out_chunk(1)

        chunks = [first_chunk]
        chunks += [functools.partial(out_chunk, c) for c in range(2, pl.cdiv(D_MODEL, PROJ_CHUNK))]
        chunks += [functools.partial(in_chunk, c) for c in range(pl.cdiv(IN_WIDTH, PROJ_CHUNK))]
        _mix_slot(chunks, proj_cur, mixed_cur, kwin, vwin, seq_start, *consts)

    slot(0, proj_b, proj_a, mixed_b, mixed_a, kwin_b, kwin_a, vwin_b, vwin_a, False)
    slot(1, proj_a, proj_b, mixed_a, mixed_b, kwin_a, kwin_b, vwin_a, vwin_b,
         (2 * i) % tiles_per_seq == 0)


def _mix_call(layer, x, ada, g1, w_in, gq, gk, sinks, rg, w_out, tables, w_mlp1, w_mlp2):
    b, s, d = x.shape
    tiles_per_seq = s // MIX_ROWS
    pairs_per_seq = tiles_per_seq // 2
    n_pairs = b * pairs_per_seq
    bias_tab, ret_tab, cdec = tables
    x_pairs = x.reshape(n_pairs, 2, MIX_ROWS, d)

    nxt = lambda i: jnp.minimum(i, n_pairs - 1)
    res = lambda i: jnp.maximum(i - 1, 0)
    const2 = lambda i: (0, 0)
    const4 = lambda i: (0, 0, 0, 0)
    single = pl.Buffered(1)
    smem = pl.BlockSpec(memory_space=pltpu.SMEM)
    slab1, slab2 = w_mlp1.shape[1] // n_pairs, w_mlp2.shape[1] // n_pairs
    lay3 = lambda i: (layer, 0, 0)
    out, w1_b, w2_b = pl.pallas_call(
        functools.partial(_mix_kernel, layer, tiles_per_seq, n_pairs),
        grid=(n_pairs + 1,),
        in_specs=[
            pl.BlockSpec((None, 2, MIX_ROWS, d), lambda i: (nxt(i), 0, 0, 0)),
            pl.BlockSpec((None,) + ada.shape[1:], lambda i: (layer, 0, 0)),
            pl.BlockSpec(g1.shape, const2),
            pl.BlockSpec((None, d, IN_WIDTH), lay3, pipeline_mode=single),
            pl.BlockSpec(gq.shape, const2),
            pl.BlockSpec(gk.shape, const2),
            smem,
            pl.BlockSpec(rg.shape, const2),
            pl.BlockSpec((None, d, d), lay3, pipeline_mode=single),
            pl.BlockSpec(bias_tab.shape, const4, pipeline_mode=single),
            pl.BlockSpec(ret_tab.shape, const4, pipeline_mode=single),
            smem,
            pl.BlockSpec((None, slab1, D_FF), lambda i: (layer, nxt(i), 0)),
            pl.BlockSpec((None, slab2, d), lambda i: (layer, nxt(i), 0)),
        ],
        out_specs=[
            pl.BlockSpec((None, 2, MIX_ROWS, d), lambda i: (res(i), 0, 0, 0)),
            pl.BlockSpec((slab1, D_FF), lambda i: (nxt(i), 0)),
            pl.BlockSpec((slab2, d), lambda i: (nxt(i), 0)),
        ],
        out_shape=[
            jax.ShapeDtypeStruct((n_pairs, 2, MIX_ROWS, d), F32),
            jax.ShapeDtypeStruct(w_mlp1.shape[1:], BF16),
            jax.ShapeDtypeStruct(w_mlp2.shape[1:], BF16),
        ],
        scratch_shapes=[
            pltpu.VMEM((MIX_ROWS, IN_WIDTH), F32),
            pltpu.VMEM((MIX_ROWS, IN_WIDTH), F32),
            pltpu.VMEM((MIX_ROWS, d), BF16),
            pltpu.VMEM((MIX_ROWS, d), BF16),
            pltpu.VMEM((MIX_ROWS, d), BF16),
            pltpu.VMEM((2, MIX_ROWS, d), F32),
            pltpu.VMEM((N_KV_HEADS, BLOCK + MIX_ROWS, 2 * HEAD_DIM), BF16),
            pltpu.VMEM((N_KV_HEADS, BLOCK + MIX_ROWS, 2 * HEAD_DIM), BF16),
            pltpu.VMEM((2 * HEAD_DIM, BLOCK + MIX_ROWS), BF16),
            pltpu.VMEM((2 * HEAD_DIM, BLOCK + MIX_ROWS), BF16),
            pltpu.VMEM((RET_HEADS, 128, 128), F32),
            pltpu.VMEM((d, IN_WIDTH), BF16),
            pltpu.VMEM((d, d), BF16),
        ],
        compiler_params=pltpu.CompilerParams(
            dimension_semantics=("arbitrary",), vmem_limit_bytes=VMEM_LIMIT),
        name="mix",
    )(x_pairs, ada, g1, w_in, gq, gk, sinks, rg, w_out, bias_tab, ret_tab, cdec,
      w_mlp1, w_mlp2)
    return out.reshape(b, s, d), w1_b, w2_b


MLP_ROWS = 1024


def _mlp_kernel(layer, x_ref, ada_ref, g2_ref, w1_ref, w2_ref, o_ref):
    x = x_ref[...]
    ms = jnp.mean(x * x, axis=-1, keepdims=True)
    h = x * lax.rsqrt(ms + EPS) * g2_ref[layer:layer + 1, :]
    row = pl.ds(pl.program_id(0), 1)
    shift, scale, gate = (ada_ref[row, k * D_MODEL:(k + 1) * D_MODEL] for k in (3, 4, 5))
    h = (h * (1.0 + scale) + shift).astype(BF16)
    acc = None
    for c in range(D_FF // FF_CHUNK):
        cols = slice(c * FF_CHUNK, (c + 1) * FF_CHUNK)
        a = jnp.dot(h, w1_ref[:, cols], preferred_element_type=F32)
        a = jnp.maximum(a, 0.0)
        part = jnp.dot((a * a).astype(BF16), w2_ref[cols, :], preferred_element_type=F32)
        acc = part if acc is None else acc + part
    o_ref[...] = x + gate * acc


def _mlp_call(layer, x, ada, g2, w1, w2):
    b, s, d = x.shape
    const2 = lambda i, j: (0, 0)
    single = pl.Buffered(1)
    return pl.pallas_call(
        functools.partial(_mlp_kernel, layer),
        grid=(b, s // MLP_ROWS),
        in_specs=[
            pl.BlockSpec((None, MLP_ROWS, d), lambda i, j: (i, j, 0)),
            pl.BlockSpec((None,) + ada.shape[1:], lambda i, j: (layer, 0, 0)),
            pl.BlockSpec(g2.shape, const2),
            pl.BlockSpec((d, D_FF), const2, pipeline_mode=single),
            pl.BlockSpec((D_FF, d), const2, pipeline_mode=single),
        ],
        out_specs=pl.BlockSpec((None, MLP_ROWS, d), lambda i, j: (i, j, 0)),
        out_shape=jax.ShapeDtypeStruct((b, s, d), F32),
        input_output_aliases={0: 0},
        compiler_params=pltpu.CompilerParams(
            dimension_semantics=("arbitrary", "arbitrary"), vmem_limit_bytes=VMEM_LIMIT),
        name="mlp",
    )(x, ada, g2, w1, w2)


def kernel(x, c, norm1_g, norm2_g, w_ada, b_ada, w_in, q_norm_g, k_norm_g, sinks, ret_norm_g,
           w_out, w_mlp1, w_mlp2):
    tables = tuple(jnp.asarray(t) for t in _constant_tables())
    ada = _ada_call(c, w_ada, b_ada)
    for l in range(DEPTH):
        x, w1_b, w2_b = _mix_call(l, x, ada, norm1_g, w_in, q_norm_g, k_norm_g, sinks, ret_norm_g,
                                  w_out, tables, w_mlp1, w_mlp2)
        x = _mlp_call(l, x, ada, norm2_g, w1_b, w2_b)
    return x
```

```python
import functools

import numpy as np
import jax
import jax.numpy as jnp
from jax import lax
from jax.experimental import pallas as pl
from jax.experimental.pallas import tpu as pltpu

D_MODEL = 1024
DEPTH = 2
ATTN_WIDTH = 512
RET_WIDTH = 512
HEAD_DIM = 64
N_Q_HEADS = 8
N_KV_HEADS = 2
N_PAIRS = N_Q_HEADS // 2
BLOCK = 128
RET_HEADS = 4
RET_DK = 128
D_FF = 4 * D_MODEL
EPS = 1e-6
NEG_INF = -1e30
LOG2E = 1.4426950408889634
IN_WIDTH = 2816
OFF_AQ, OFF_AK, OFF_AV, OFF_RQ, OFF_RK, OFF_RV, OFF_RG = 0, 512, 640, 768, 1280, 1792, 2304

MIX_ROWS = 256
PROJ_CHUNK = 256
FF_CHUNK = 1024
ADA_COLS = 1536
VMEM_LIMIT = 60 * 1024 * 1024

F32 = jnp.float32
BF16 = jnp.bfloat16


@functools.lru_cache(maxsize=None)
def _constant_tables():
    q_pos = np.arange(BLOCK)[:, None]
    k_pos = np.arange(2 * BLOCK)[None, :]
    dist = q_pos + BLOCK - k_pos
    valid = (dist >= 0) & (dist < BLOCK)
    valid_first = valid & (k_pos >= BLOCK)
    slopes = np.exp2(-8.0 * np.arange(1, N_Q_HEADS + 1, dtype=np.float64) / N_Q_HEADS)
    bias = -slopes[:, None, None] * dist[None].astype(np.float64)
    bias_tab = np.stack([np.where(v[None], bias * LOG2E, NEG_INF) for v in (valid_first, valid)])
    bias_tab = np.ascontiguousarray(bias_tab.transpose(0, 1, 3, 2), np.float32)

    log_gamma = np.log1p(-np.exp2(-5.0 - np.arange(RET_HEADS, dtype=np.float64)))
    idx = np.arange(BLOCK, dtype=np.float64)
    rel = idx[:, None] - idx[None, :]
    k_scale = RET_DK ** -0.5
    decay_in = np.where(rel >= 0, np.exp(log_gamma[:, None, None] * np.maximum(rel, 0.0)), 0.0) * k_scale
    q_decay = np.exp(log_gamma[:, None] * (idx[None, :] + 1.0))
    k_decay = np.exp(log_gamma[:, None] * (BLOCK - 1.0 - idx[None, :])) * k_scale
    q_decay = np.broadcast_to(q_decay[:, :, None], (RET_HEADS, BLOCK, BLOCK))
    k_decay = np.broadcast_to(k_decay[:, :, None], (RET_HEADS, BLOCK, BLOCK))
    chunk_decay = np.exp(log_gamma * BLOCK)
    ret_tab = np.stack([decay_in, q_decay, k_decay]).astype(np.float32)
    return bias_tab, ret_tab, chunk_decay.astype(np.float32)


def _ada_kernel(c_ref, w_ref, b_ref, o_ref):
    c = c_ref[...]
    c_act = c * (1.0 / (1.0 + jnp.exp(-c)))
    o_ref[...] = jnp.dot(c_act.astype(BF16), w_ref[...].astype(BF16),
                         preferred_element_type=F32) + b_ref[pl.ds(pl.program_id(0), 1), :]


def _ada_call(c, w_ada, b_ada):
    rows = c.shape[0]
    n = w_ada.shape[-1]
    return pl.pallas_call(
        _ada_kernel,
        grid=(DEPTH, n // ADA_COLS),
        in_specs=[
            pl.BlockSpec((rows, D_MODEL), lambda l, j: (0, 0)),
            pl.BlockSpec((None, D_MODEL, ADA_COLS), lambda l, j: (l, 0, j)),
            pl.BlockSpec((DEPTH, ADA_COLS), lambda l, j: (0, j)),
        ],
        out_specs=pl.BlockSpec((None, rows, ADA_COLS), lambda l, j: (l, 0, j)),
        out_shape=jax.ShapeDtypeStruct((DEPTH, rows, n), F32),
        compiler_params=pltpu.CompilerParams(
            dimension_semantics=("arbitrary", "arbitrary"), vmem_limit_bytes=VMEM_LIMIT),
        name="ada",
    )(c, w_ada, b_ada)


def _head_norm(a, gain_row):
    lo = lax.broadcasted_iota(jnp.int32, a.shape, 1) < HEAD_DIM
    a2 = a * a
    s_lo = jnp.sum(jnp.where(lo, a2, 0.0), axis=-1, keepdims=True)
    s_hi = jnp.sum(jnp.where(lo, 0.0, a2), axis=-1, keepdims=True)
    mean_sq = jnp.where(lo, s_lo, s_hi) * (1.0 / HEAD_DIM)
    return a * lax.rsqrt(mean_sq + EPS) * gain_row


def _pair_gain(g_ref, layer):
    row = g_ref[layer:layer + 1, :]
    return jnp.concatenate([row, row], axis=1)


def _build_windows(proj_ref, gk_row, kwin_ref, kwin_prev, vwin_ref, vwin_prev):
    kn = _head_norm(proj_ref[:, OFF_AK:OFF_AK + 128], gk_row)
    kn_rot = pltpu.roll(kn, HEAD_DIM, axis=1)
    lo = lax.broadcasted_iota(jnp.int32, kn.shape, 1) < HEAD_DIM
    k_dup = (jnp.where(lo, kn, kn_rot), jnp.where(lo, kn_rot, kn))
    for g in range(N_KV_HEADS):
        kwin_ref[g, 0:BLOCK, :] = kwin_prev[g, MIX_ROWS:MIX_ROWS + BLOCK, :]
        kwin_ref[g, BLOCK:BLOCK + MIX_ROWS, :] = k_dup[g].astype(BF16)
    vwin_ref[:, 0:BLOCK] = vwin_prev[:, MIX_ROWS:MIX_ROWS + BLOCK]
    vwin_ref[:, BLOCK:BLOCK + MIX_ROWS] = proj_ref[:, OFF_AV:OFF_AV + 128].T.astype(BF16)


def _mix_slot(chunks, proj_ref, mixed_ref, kwin_ref, vwin_ref,
              seq_start, layer, gq_ref, sinks_ref, rg_ref,
              bias_ref, ret_ref, cdec_ref, s_ref):
    first_tab = 1 if seq_start is False else jnp.where(seq_start, 0, 1)
    chunks = list(chunks)
    n_chunks = len(chunks)
    n_points = 5 + 5 * (MIX_ROWS // BLOCK)
    point = [0]

    def issue_chunk():
        point[0] += 1
        while chunks and (n_chunks - len(chunks)) * n_points < point[0] * n_chunks:
            chunks.pop(0)()

    issue_chunk()

    lo = lax.broadcasted_iota(jnp.int32, (BLOCK, 2 * HEAD_DIM), 1) < HEAD_DIM
    gq = _pair_gain(gq_ref, layer) * (HEAD_DIM ** -0.5 * LOG2E)
    n_blocks = MIX_ROWS // BLOCK
    rows = [pl.ds(j * BLOCK, BLOCK) for j in range(n_blocks)]
    wins = [slice(j * BLOCK, (j + 2) * BLOCK) for j in range(n_blocks)]
    tabs = [first_tab if j == 0 else 1 for j in range(n_blocks)]

    s_t = {}
    for j in range(n_blocks):
        for g in range(N_KV_HEADS):
            qs = []
            for pp in range(2):
                p = 2 * g + pp
                qn = _head_norm(proj_ref[rows[j], OFF_AQ + 128 * p:OFF_AQ + 128 * p + 128], gq)
                qs.append(jnp.where(lo, qn, 0.0).astype(BF16))
                qs.append(jnp.where(lo, 0.0, qn).astype(BF16))
            s_t[j, g] = lax.dot_general(kwin_ref[g, wins[j], :], jnp.concatenate(qs, axis=0),
                                        (((1,), (1,)), ((), ())),
                                        preferred_element_type=F32)
    issue_chunk()
    sinks = [sinks_ref[layer, head] * LOG2E for head in range(N_Q_HEADS)]
    es, ms = {}, {}
    for j in range(n_blocks):
        for head in range(N_Q_HEADS):
            g, hh = divmod(head, 4)
            sh = s_t[j, g][:, 128 * hh:128 * hh + 128] + bias_ref[tabs[j], head]
            ms[j, head] = jnp.maximum(jnp.max(sh, axis=0, keepdims=True), sinks[head])
            es[j, head] = jnp.exp2(sh - ms[j, head]).astype(BF16)
            if head % 4 == 3:
                issue_chunk()
    ones = jnp.ones((HEAD_DIM, 2 * BLOCK), BF16)
    o_t = {}
    for j in range(n_blocks):
        for g in range(N_KV_HEADS):
            lhs = jnp.concatenate([vwin_ref[64 * g:64 * g + 64, wins[j]], ones], axis=0)
            p_t = jnp.concatenate([es[j, 4 * g + hh] for hh in range(4)], axis=1)
            o_t[j, g] = jnp.dot(lhs, p_t, preferred_element_type=F32)
    issue_chunk()
    for j in range(n_blocks):
        for p in range(N_PAIRS):
            g, pp = divmod(p, 2)
            halves = []
            for head in (2 * p, 2 * p + 1):
                cols = slice(128 * (head % 4), 128 * (head % 4) + 128)
                denom = o_t[j, g][HEAD_DIM:, cols] + jnp.exp2(sinks[head] - ms[j, head])
                halves.append(o_t[j, g][:HEAD_DIM, cols] / denom)
            o = jnp.concatenate(halves, axis=0).T
            mixed_ref[rows[j], 128 * p:128 * p + 128] = o.astype(BF16)
        issue_chunk()

    heads = {}
    for j in range(n_blocks):
        for hd in range(RET_HEADS):
            c0 = 128 * hd
            q_f = proj_ref[rows[j], OFF_RQ + c0:OFF_RQ + c0 + 128]
            k = proj_ref[rows[j], OFF_RK + c0:OFF_RK + c0 + 128]
            v = proj_ref[rows[j], OFF_RV + c0:OFF_RV + c0 + 128].astype(BF16)
            inner = lax.dot_general(q_f.astype(BF16), k.astype(BF16), (((1,), (1,)), ((), ())),
                                    preferred_element_type=F32)
            kd = (k * ret_ref[2, hd]).astype(BF16)
            kv = lax.dot_general(kd, v, (((0,), (0,)), ((), ())), preferred_element_type=F32)
            heads[j, hd] = (q_f, v, inner, kv)
    issue_chunk()
    outs = {}
    for j in range(n_blocks):
        for hd in range(RET_HEADS):
            q_f, v, inner, kv = heads[j, hd]
            state = s_ref[hd]
            if j == 0 and seq_start is not False:
                state = jnp.where(seq_start, 0.0, state)
            lhs = jnp.concatenate([(inner * ret_ref[0, hd]).astype(BF16), (q_f * ret_ref[1, hd]).astype(BF16)],
                                  axis=1)
            outs[j, hd] = jnp.dot(lhs, jnp.concatenate([v, state.astype(BF16)], axis=0),
                                  preferred_element_type=F32)
            s_ref[hd] = cdec_ref[hd] * state + kv
    issue_chunk()
    for j in range(n_blocks):
        for hd in range(RET_HEADS):
            c0 = 128 * hd
            o = outs[j, hd]
            gate = proj_ref[rows[j], OFF_RG + c0:OFF_RG + c0 + 128]
            mu = jnp.mean(o, axis=-1, keepdims=True)
            d = o - mu
            var = jnp.mean(d * d, axis=-1, keepdims=True)
            on = d * lax.rsqrt(var + EPS) * rg_ref[layer:layer + 1, c0:c0 + 128]
            y = gate * (1.0 / (1.0 + jnp.exp(-gate))) * on
            mixed_ref[rows[j], ATTN_WIDTH + c0:ATTN_WIDTH + c0 + 128] = y.astype(BF16)
            if hd % 2 == 1:
                issue_chunk()

    assert not chunks


def _mix_kernel(layer, tiles_per_seq, n_pairs,
                xn_ref, ada_ref, g1_ref, win_f32_ref, gq_ref, gk_ref, sinks_ref,
                rg_ref, wout_f32_ref, bias_ref, ret_ref, cdec_ref, w1_f32_ref, w2_f32_ref,
                o_ref, w1_bf16_ref, w2_bf16_ref,
                proj_a, proj_b, mixed_a, mixed_b, hn_ref, xres_ref, kwin_a, kwin_b, vwin_a, vwin_b, s_ref,
                win_ref, wout_ref):
    i = pl.program_id(0)
    pairs_per_seq = tiles_per_seq // 2
    ada_n = pl.ds(jnp.minimum(i, n_pairs - 1) // pairs_per_seq, 1)
    ada_r = pl.ds(jnp.maximum(i - 1, 0) // pairs_per_seq, 1)

    @pl.when(i == 0)
    def _():
        win_ref[...] = win_f32_ref[...].astype(BF16)
        wout_ref[...] = wout_f32_ref[...].astype(BF16)

    @pl.when(i == 0)
    def _():
        proj_b[...] = jnp.zeros_like(proj_b)
        mixed_a[...] = jnp.zeros_like(mixed_a)
        xres_ref[...] = jnp.zeros_like(xres_ref)
        kwin_a[...] = jnp.zeros_like(kwin_a)
        kwin_b[...] = jnp.zeros_like(kwin_b)
        vwin_a[...] = jnp.zeros_like(vwin_a)
        vwin_b[...] = jnp.zeros_like(vwin_b)
        s_ref[...] = jnp.zeros_like(s_ref)

    consts = (layer, gq_ref, sinks_ref, rg_ref, bias_ref, ret_ref, cdec_ref, s_ref)

    def slot(half, proj_cur, proj_next, mixed_cur, mixed_prev, kwin, kwin_next, vwin, vwin_next,
             seq_start):
        def out_chunk(c):
            cols = slice(c * PROJ_CHUNK, min((c + 1) * PROJ_CHUNK, D_MODEL))
            y = jnp.dot(mixed_prev[...], wout_ref[:, cols], preferred_element_type=F32)
            gate = ada_ref[ada_r, 2 * D_MODEL + cols.start:2 * D_MODEL + cols.stop]
            o_ref[half, :, cols] = xres_ref[half, :, cols] + gate * y
            xres_ref[half, :, cols] = xn_ref[half, :, cols]

        def norm_next():
            x = xn_ref[half]
            ms = jnp.mean(x * x, axis=-1, keepdims=True)
            h = x * lax.rsqrt(ms + EPS) * g1_ref[layer:layer + 1, :]
            shift, scale = ada_ref[ada_n, 0:D_MODEL], ada_ref[ada_n, D_MODEL:2 * D_MODEL]
            hn_ref[...] = (h * (1.0 + scale) + shift).astype(BF16)

        def in_chunk(c):
            cols = slice(c * PROJ_CHUNK, min((c + 1) * PROJ_CHUNK, IN_WIDTH))
            proj_next[:, cols] = jnp.dot(hn_ref[...], win_ref[:, cols], preferred_element_type=F32)
            if cols.start <= OFF_AK and OFF_AV + 2 * HEAD_DIM <= cols.stop:
                _build_windows(proj_next, _pair_gain(gk_ref, layer), kwin_next, kwin, vwin_next, vwin)

        def first_chunk():
            out_chunk(0)
            norm_next()
            out_chunk(1)

        chunks = [first_chunk]
        chunks += [functools.partial(out_chunk, c) for c in range(2, pl.cdiv(D_MODEL, PROJ_CHUNK))]
        chunks += [functools.partial(in_chunk, c) for c in range(pl.cdiv(IN_WIDTH, PROJ_CHUNK))]
        _mix_slot(chunks, proj_cur, mixed_cur, kwin, vwin, seq_start, *consts)

    slot(0, proj_b, proj_a, mixed_b, mixed_a, kwin_b, kwin_a, vwin_b, vwin_a, False)
    slot(1, proj_a, proj_b, mixed_a, mixed_b, kwin_a, kwin_b, vwin_a, vwin_b,
         (2 * i) % tiles_per_seq == 0)

    w1_bf16_ref[...] = w1_f32_ref[...].astype(BF16)
    w2_bf16_ref[...] = w2_f32_ref[...].astype(BF16)


def _mix_call(layer, x, ada, g1, w_in, gq, gk, sinks, rg, w_out, tables, w_mlp1, w_mlp2):
    b, s, d = x.shape
    tiles_per_seq = s // MIX_ROWS
    pairs_per_seq = tiles_per_seq // 2
    n_pairs = b * pairs_per_seq
    bias_tab, ret_tab, cdec = tables
    x_pairs = x.reshape(n_pairs, 2, MIX_ROWS, d)

    nxt = lambda i: jnp.minimum(i, n_pairs - 1)
    res = lambda i: jnp.maximum(i - 1, 0)
    const2 = lambda i: (0, 0)
    const4 = lambda i: (0, 0, 0, 0)
    single = pl.Buffered(1)
    smem = pl.BlockSpec(memory_space=pltpu.SMEM)
    slab1, slab2 = w_mlp1.shape[1] // n_pairs, w_mlp2.shape[1] // n_pairs
    lay3 = lambda i: (layer, 0, 0)
    out, w1_b, w2_b = pl.pallas_call(
        functools.partial(_mix_kernel, layer, tiles_per_seq, n_pairs),
        grid=(n_pairs + 1,),
        in_specs=[
            pl.BlockSpec((None, 2, MIX_ROWS, d), lambda i: (nxt(i), 0, 0, 0)),
            pl.BlockSpec((None,) + ada.shape[1:], lambda i: (layer, 0, 0)),
            pl.BlockSpec(g1.shape, const2),
            pl.BlockSpec((None, d, IN_WIDTH), lay3, pipeline_mode=single),
            pl.BlockSpec(gq.shape, const2),
            pl.BlockSpec(gk.shape, const2),
            smem,
            pl.BlockSpec(rg.shape, const2),
            pl.BlockSpec((None, d, d), lay3, pipeline_mode=single),
            pl.BlockSpec(bias_tab.shape, const4, pipeline_mode=single),
            pl.BlockSpec(ret_tab.shape, const4, pipeline_mode=single),
            smem,
            pl.BlockSpec((None, slab1, D_FF), lambda i: (layer, nxt(i), 0)),
            pl.BlockSpec((None, slab2, d), lambda i: (layer, nxt(i), 0)),
        ],
        out_specs=[
            pl.BlockSpec((None, 2, MIX_ROWS, d), lambda i: (res(i), 0, 0, 0)),
            pl.BlockSpec((slab1, D_FF), lambda i: (nxt(i), 0)),
            pl.BlockSpec((slab2, d), lambda i: (nxt(i), 0)),
        ],
        out_shape=[
            jax.ShapeDtypeStruct((n_pairs, 2, MIX_ROWS, d), F32),
            jax.ShapeDtypeStruct(w_mlp1.shape[1:], BF16),
            jax.ShapeDtypeStruct(w_mlp2.shape[1:], BF16),
        ],
        scratch_shapes=[
            pltpu.VMEM((MIX_ROWS, IN_WIDTH), F32),
            pltpu.VMEM((MIX_ROWS, IN_WIDTH), F32),
            pltpu.VMEM((MIX_ROWS, d), BF16),
            pltpu.VMEM((MIX_ROWS, d), BF16),
            pltpu.VMEM((MIX_ROWS, d), BF16),
            pltpu.VMEM((2, MIX_ROWS, d), F32),
            pltpu.VMEM((N_KV_HEADS, BLOCK + MIX_ROWS, 2 * HEAD_DIM), BF16),
            pltpu.VMEM((N_KV_HEADS, BLOCK + MIX_ROWS, 2 * HEAD_DIM), BF16),
            pltpu.VMEM((2 * HEAD_DIM, BLOCK + MIX_ROWS), BF16),
            pltpu.VMEM((2 * HEAD_DIM, BLOCK + MIX_ROWS), BF16),
            pltpu.VMEM((RET_HEADS, 128, 128), F32),
            pltpu.VMEM((d, IN_WIDTH), BF16),
            pltpu.VMEM((d, d), BF16),
        ],
        compiler_params=pltpu.CompilerParams(
            dimension_semantics=("arbitrary",), vmem_limit_bytes=VMEM_LIMIT),
        name="mix",
    )(x_pairs, ada, g1, w_in, gq, gk, sinks, rg, w_out, bias_tab, ret_tab, cdec,
      w_mlp1, w_mlp2)
    return out.reshape(b, s, d), w1_b, w2_b


MLP_ROWS = 1024


def _mlp_kernel(layer, x_ref, ada_ref, g2_ref, w1_ref, w2_ref, o_ref):
    x = x_ref[...]
    ms = jnp.mean(x * x, axis=-1, keepdims=True)
    h = x * lax.rsqrt(ms + EPS) * g2_ref[layer:layer + 1, :]
    row = pl.ds(pl.program_id(0), 1)
    shift, scale, gate = (ada_ref[row, k * D_MODEL:(k + 1) * D_MODEL] for k in (3, 4, 5))
    h = (h * (1.0 + scale) + shift).astype(BF16)
    acc = None
    for c in range(D_FF // FF_CHUNK):
        cols = slice(c * FF_CHUNK, (c + 1) * FF_CHUNK)
        a = jnp.dot(h, w1_ref[:, cols], preferred_element_type=F32)
        a = jnp.maximum(a, 0.0)
        part = jnp.dot((a * a).astype(BF16), w2_ref[cols, :], preferred_element_type=F32)
        acc = part if acc is None else acc + part
    o_ref[...] = x + gate * acc


def _mlp_call(layer, x, ada, g2, w1, w2):
    b, s, d = x.shape
    const2 = lambda i, j: (0, 0)
    single = pl.Buffered(1)
    return pl.pallas_call(
        functools.partial(_mlp_kernel, layer),
        grid=(b, s // MLP_ROWS),
        in_specs=[
            pl.BlockSpec((None, MLP_ROWS, d), lambda i, j: (i, j, 0)),
            pl.BlockSpec((None,) + ada.shape[1:], lambda i, j: (layer, 0, 0)),
            pl.BlockSpec(g2.shape, const2),
            pl.BlockSpec((d, D_FF), const2, pipeline_mode=single),
            pl.BlockSpec((D_FF, d), const2, pipeline_mode=single),
        ],
        out_specs=pl.BlockSpec((None, MLP_ROWS, d), lambda i, j: (i, j, 0)),
        out_shape=jax.ShapeDtypeStruct((b, s, d), F32),
        compiler_params=pltpu.CompilerParams(
            dimension_semantics=("arbitrary", "arbitrary"), vmem_limit_bytes=VMEM_LIMIT),
        name="mlp",
    )(x, ada, g2, w1, w2)


def kernel(x, c, norm1_g, norm2_g, w_ada, b_ada, w_in, q_norm_g, k_norm_g, sinks, ret_norm_g,
           w_out, w_mlp1, w_mlp2):
    tables = tuple(jnp.asarray(t) for t in _constant_tables())
    ada = _ada_call(c, w_ada, b_ada)
    for l in range(DEPTH):
        x, w1_b, w2_b = _mix_call(l, x, ada, norm1_g, w_in, q_norm_g, k_norm_g, sinks, ret_norm_g,
                                  w_out, tables, w_mlp1, w_mlp2)
        x = _mlp_call(l, x, ada, norm2_g, w1_b, w2_b)
    return x
```

```python
import functools

import numpy as np
import jax
import jax.numpy as jnp
from jax import lax
from jax.experimental import pallas as pl
from jax.experimental.pallas import tpu as pltpu

D_MODEL = 1024
DEPTH = 2
ATTN_WIDTH = 512
RET_WIDTH = 512
HEAD_DIM = 64
N_Q_HEADS = 8
N_KV_HEADS = 2
N_PAIRS = N_Q_HEADS // 2
BLOCK = 128
RET_HEADS = 4
RET_DK = 128
D_FF = 4 * D_MODEL
EPS = 1e-6
NEG_INF = -1e30
LOG2E = 1.4426950408889634
IN_WIDTH = 2816
OFF_AQ, OFF_AK, OFF_AV, OFF_RQ, OFF_RK, OFF_RV, OFF_RG = 0, 512, 640, 768, 1280, 1792, 2304

MIX_ROWS = 256
PROJ_CHUNK = 256
FF_CHUNK = 1024
ADA_COLS = 768
VMEM_LIMIT = 60 * 1024 * 1024

F32 = jnp.float32
BF16 = jnp.bfloat16


@functools.lru_cache(maxsize=None)
def _constant_tables():
    q_pos = np.arange(BLOCK)[:, None]
    k_pos = np.arange(2 * BLOCK)[None, :]
    dist = q_pos + BLOCK - k_pos
    valid = (dist >= 0) & (dist < BLOCK)
    valid_first = valid & (k_pos >= BLOCK)
    slopes = np.exp2(-8.0 * np.arange(1, N_Q_HEADS + 1, dtype=np.float64) / N_Q_HEADS)
    bias = -slopes[:, None, None] * dist[None].astype(np.float64)
    bias_tab = np.stack([np.where(v[None], bias * LOG2E, NEG_INF) for v in (valid_first, valid)])
    bias_tab = np.ascontiguousarray(bias_tab.transpose(0, 1, 3, 2), np.float32)

    log_gamma = np.log1p(-np.exp2(-5.0 - np.arange(RET_HEADS, dtype=np.float64)))
    idx = np.arange(BLOCK, dtype=np.float64)
    rel = idx[:, None] - idx[None, :]
    k_scale = RET_DK ** -0.5
    decay_in = np.where(rel >= 0, np.exp(log_gamma[:, None, None] * np.maximum(rel, 0.0)), 0.0) * k_scale
    q_decay = np.exp(log_gamma[:, None] * (idx[None, :] + 1.0))
    k_decay = np.exp(log_gamma[:, None] * (BLOCK - 1.0 - idx[None, :])) * k_scale
    q_decay = np.broadcast_to(q_decay[:, :, None], (RET_HEADS, BLOCK, BLOCK))
    k_decay = np.broadcast_to(k_decay[:, :, None], (RET_HEADS, BLOCK, BLOCK))
    chunk_decay = np.exp(log_gamma * BLOCK)
    ret_tab = np.stack([decay_in, q_decay, k_decay]).astype(np.float32)
    return bias_tab, ret_tab, chunk_decay.astype(np.float32)


def _ada_kernel(c_ref, w_ref, b_ref, o_ref):
    c = c_ref[...]
    c_act = c * (1.0 / (1.0 + jnp.exp(-c)))
    o_ref[...] = jnp.dot(c_act.astype(BF16), w_ref[...].astype(BF16),
                         preferred_element_type=F32) + b_ref[pl.ds(pl.program_id(0), 1), :]


def _ada_call(c, w_ada, b_ada):
    rows = c.shape[0]
    n = w_ada.shape[-1]
    return pl.pallas_call(
        _ada_kernel,
        grid=(DEPTH, n // ADA_COLS),
        in_specs=[
            pl.BlockSpec((rows, D_MODEL), lambda l, j: (0, 0)),
            pl.BlockSpec((None, D_MODEL, ADA_COLS), lambda l, j: (l, 0, j)),
            pl.BlockSpec((DEPTH, ADA_COLS), lambda l, j: (0, j)),
        ],
        out_specs=pl.BlockSpec((None, rows, ADA_COLS), lambda l, j: (l, 0, j)),
        out_shape=jax.ShapeDtypeStruct((DEPTH, rows, n), F32),
        compiler_params=pltpu.CompilerParams(
            dimension_semantics=("arbitrary", "arbitrary"), vmem_limit_bytes=VMEM_LIMIT),
        name="ada",
    )(c, w_ada, b_ada)


def _head_norm(a, gain_row):
    lo = lax.broadcasted_iota(jnp.int32, a.shape, 1) < HEAD_DIM
    a2 = a * a
    s_lo = jnp.sum(jnp.where(lo, a2, 0.0), axis=-1, keepdims=True)
    s_hi = jnp.sum(jnp.where(lo, 0.0, a2), axis=-1, keepdims=True)
    mean_sq = jnp.where(lo, s_lo, s_hi) * (1.0 / HEAD_DIM)
    return a * lax.rsqrt(mean_sq + EPS) * gain_row


def _pair_gain(g_ref, layer):
    row = g_ref[layer:layer + 1, :]
    return jnp.concatenate([row, row], axis=1)


def _build_windows(proj_ref, gk_row, kwin_ref, kwin_prev, vwin_ref, vwin_prev):
    kn = _head_norm(proj_ref[:, OFF_AK:OFF_AK + 128], gk_row)
    kn_rot = pltpu.roll(kn, HEAD_DIM, axis=1)
    lo = lax.broadcasted_iota(jnp.int32, kn.shape, 1) < HEAD_DIM
    k_dup = (jnp.where(lo, kn, kn_rot), jnp.where(lo, kn_rot, kn))
    for g in range(N_KV_HEADS):
        kwin_ref[g, 0:BLOCK, :] = kwin_prev[g, MIX_ROWS:MIX_ROWS + BLOCK, :]
        kwin_ref[g, BLOCK:BLOCK + MIX_ROWS, :] = k_dup[g].astype(BF16)
    vwin_ref[:, 0:BLOCK] = vwin_prev[:, MIX_ROWS:MIX_ROWS + BLOCK]
    vwin_ref[:, BLOCK:BLOCK + MIX_ROWS] = proj_ref[:, OFF_AV:OFF_AV + 128].T.astype(BF16)


def _mix_slot(chunks, proj_ref, mixed_ref, kwin_ref, vwin_ref,
              seq_start, layer, gq_ref, sinks_ref, rg_ref,
              bias_ref, ret_ref, cdec_ref, s_ref):
    first_tab = 1 if seq_start is False else jnp.where(seq_start, 0, 1)
    chunks = list(chunks)
    n_chunks = len(chunks)
    n_points = 5 + 5 * (MIX_ROWS // BLOCK)
    point = [0]

    def issue_chunk():
        point[0] += 1
        while chunks and (n_chunks - len(chunks)) * n_points < point[0] * n_chunks:
            chunks.pop(0)()

    issue_chunk()

    lo = lax.broadcasted_iota(jnp.int32, (BLOCK, 2 * HEAD_DIM), 1) < HEAD_DIM
    gq = _pair_gain(gq_ref, layer) * (HEAD_DIM ** -0.5 * LOG2E)
    n_blocks = MIX_ROWS // BLOCK
    rows = [pl.ds(j * BLOCK, BLOCK) for j in range(n_blocks)]
    wins = [slice(j * BLOCK, (j + 2) * BLOCK) for j in range(n_blocks)]
    tabs = [first_tab if j == 0 else 1 for j in range(n_blocks)]

    s_t = {}
    for j in range(n_blocks):
        for g in range(N_KV_HEADS):
            qs = []
            for pp in range(2):
                p = 2 * g + pp
                qn = _head_norm(proj_ref[rows[j], OFF_AQ + 128 * p:OFF_AQ + 128 * p + 128], gq)
                qs.append(jnp.where(lo, qn, 0.0).astype(BF16))
                qs.append(jnp.where(lo, 0.0, qn).astype(BF16))
            s_t[j, g] = lax.dot_general(kwin_ref[g, wins[j], :], jnp.concatenate(qs, axis=0),
                                        (((1,), (1,)), ((), ())),
                                        preferred_element_type=F32)
    issue_chunk()
    sinks = [sinks_ref[layer, head] * LOG2E for head in range(N_Q_HEADS)]
    es, ms = {}, {}
    for j in range(n_blocks):
        for head in range(N_Q_HEADS):
            g, hh = divmod(head, 4)
            sh = s_t[j, g][:, 128 * hh:128 * hh + 128] + bias_ref[tabs[j], head]
            ms[j, head] = jnp.maximum(jnp.max(sh, axis=0, keepdims=True), sinks[head])
            es[j, head] = jnp.exp2(sh - ms[j, head]).astype(BF16)
            if head % 4 == 3:
                issue_chunk()
    ones = jnp.ones((HEAD_DIM, 2 * BLOCK), BF16)
    o_t = {}
    for j in range(n_blocks):
        for g in range(N_KV_HEADS):
            lhs = jnp.concatenate([vwin_ref[64 * g:64 * g + 64, wins[j]], ones], axis=0)
            p_t = jnp.concatenate([es[j, 4 * g + hh] for hh in range(4)], axis=1)
            o_t[j, g] = jnp.dot(lhs, p_t, preferred_element_type=F32)
    issue_chunk()
    for j in range(n_blocks):
        for p in range(N_PAIRS):
            g, pp = divmod(p, 2)
            halves = []
            for head in (2 * p, 2 * p + 1):
                cols = slice(128 * (head % 4), 128 * (head % 4) + 128)
                denom = o_t[j, g][HEAD_DIM:, cols] + jnp.exp2(sinks[head] - ms[j, head])
                halves.append(o_t[j, g][:HEAD_DIM, cols] / denom)
            o = jnp.concatenate(halves, axis=0).T
            mixed_ref[rows[j], 128 * p:128 * p + 128] = o.astype(BF16)
        issue_chunk()

    heads = {}
    for j in range(n_blocks):
        for hd in range(RET_HEADS):
            c0 = 128 * hd
            q_f = proj_ref[rows[j], OFF_RQ + c0:OFF_RQ + c0 + 128]
            k = proj_ref[rows[j], OFF_RK + c0:OFF_RK + c0 + 128]
            v = proj_ref[rows[j], OFF_RV + c0:OFF_RV + c0 + 128].astype(BF16)
            inner = lax.dot_general(q_f.astype(BF16), k.astype(BF16), (((1,), (1,)), ((), ())),
                                    preferred_element_type=F32)
            kd = (k * ret_ref[2, hd]).astype(BF16)
            kv = lax.dot_general(kd, v, (((0,), (0,)), ((), ())), preferred_element_type=F32)
            heads[j, hd] = (q_f, v, inner, kv)
    issue_chunk()
    outs = {}
    for j in range(n_blocks):
        for hd in range(RET_HEADS):
            q_f, v, inner, kv = heads[j, hd]
            state = s_ref[hd]
            if j == 0 and seq_start is not False:
                state = jnp.where(seq_start, 0.0, state)
            lhs = jnp.concatenate([(inner * ret_ref[0, hd]).astype(BF16), (q_f * ret_ref[1, hd]).astype(BF16)],
                                  axis=1)
            outs[j, hd] = jnp.dot(lhs, jnp.concatenate([v, state.astype(BF16)], axis=0),
                                  preferred_element_type=F32)
            s_ref[hd] = cdec_ref[hd] * state + kv
    issue_chunk()
    for j in range(n_blocks):
        for hd in range(RET_HEADS):
            c0 = 128 * hd
            o = outs[j, hd]
            gate = proj_ref[rows[j], OFF_RG + c0:OFF_RG + c0 + 128]
            mu = jnp.mean(o, axis=-1, keepdims=True)
            d = o - mu
            var = jnp.mean(d * d, axis=-1, keepdims=True)
            on = d * lax.rsqrt(var + EPS) * rg_ref[layer:layer + 1, c0:c0 + 128]
            y = gate * (1.0 / (1.0 + jnp.exp(-gate))) * on
            mixed_ref[rows[j], ATTN_WIDTH + c0:ATTN_WIDTH + c0 + 128] = y.astype(BF16)
            if hd % 2 == 1:
                issue_chunk()

    assert not chunks


def _mix_kernel(layer, tiles_per_seq, n_pairs,
                xn_ref, ada_ref, g1_ref, win_f32_ref, gq_ref, gk_ref, sinks_ref,
                rg_ref, wout_f32_ref, bias_ref, ret_ref, cdec_ref, w1_f32_ref, w2_f32_ref,
                o_ref, w1_bf16_ref, w2_bf16_ref,
                proj_a, proj_b, mixed_a, mixed_b, hn_ref, xres_ref, kwin_a, kwin_b, vwin_a, vwin_b, s_ref,
                win_ref, wout_ref):
    i = pl.program_id(0)
    pairs_per_seq = tiles_per_seq // 2
    ada_n = pl.ds(jnp.minimum(i, n_pairs - 1) // pairs_per_seq, 1)
    ada_r = pl.ds(jnp.maximum(i - 1, 0) // pairs_per_seq, 1)

    @pl.when(i == 0)
    def _():
        win_ref[...] = win_f32_ref[...].astype(BF16)
        wout_ref[...] = wout_f32_ref[...].astype(BF16)

    w1_bf16_ref[...] = w1_f32_ref[...].astype(BF16)
    w2_bf16_ref[...] = w2_f32_ref[...].astype(BF16)

    @pl.when(i == 0)
    def _():
        proj_b[...] = jnp.zeros_like(proj_b)
        mixed_a[...] = jnp.zeros_like(mixed_a)
        xres_ref[...] = jnp.zeros_like(xres_ref)
        kwin_a[...] = jnp.zeros_like(kwin_a)
        kwin_b[...] = jnp.zeros_like(kwin_b)
        vwin_a[...] = jnp.zeros_like(vwin_a)
        vwin_b[...] = jnp.zeros_like(vwin_b)
        s_ref[...] = jnp.zeros_like(s_ref)

    consts = (layer, gq_ref, sinks_ref, rg_ref, bias_ref, ret_ref, cdec_ref, s_ref)

    def slot(half, proj_cur, proj_next, mixed_cur, mixed_prev, kwin, kwin_next, vwin, vwin_next,
             seq_start):
        def out_chunk(c):
            cols = slice(c * PROJ_CHUNK, min((c + 1) * PROJ_CHUNK, D_MODEL))
            y = jnp.dot(mixed_prev[...], wout_ref[:, cols], preferred_element_type=F32)
            gate = ada_ref[ada_r, 2 * D_MODEL + cols.start:2 * D_MODEL + cols.stop]
            o_ref[half, :, cols] = xres_ref[half, :, cols] + gate * y
            xres_ref[half, :, cols] = xn_ref[half, :, cols]

        def norm_next():
            x = xn_ref[half]
            ms = jnp.mean(x * x, axis=-1, keepdims=True)
            h = x * lax.rsqrt(ms + EPS) * g1_ref[layer:layer + 1, :]
            shift, scale = ada_ref[ada_n, 0:D_MODEL], ada_ref[ada_n, D_MODEL:2 * D_MODEL]
            hn_ref[...] = (h * (1.0 + scale) + shift).astype(BF16)

        def in_chunk(c):
            cols = slice(c * PROJ_CHUNK, min((c + 1) * PROJ_CHUNK, IN_WIDTH))
            proj_next[:, cols] = jnp.dot(hn_ref[...], win_ref[:, cols], preferred_element_type=F32)
            if cols.start <= OFF_AK and OFF_AV + 2 * HEAD_DIM <= cols.stop:
                _build_windows(proj_next, _pair_gain(gk_ref, layer), kwin_next, kwin, vwin_next, vwin)

        def first_chunk():
            out_chunk(0)
            norm_next()
            out_chunk(1)

        chunks = [first_chunk]
        chunks += [functools.partial(out_chunk, c) for c in range(2, pl.cdiv(D_MODEL, PROJ_CHUNK))]
        chunks += [functools.partial(in_chunk, c) for c in range(pl.cdiv(IN_WIDTH, PROJ_CHUNK))]
        _mix_slot(chunks, proj_cur, mixed_cur, kwin, vwin, seq_start, *consts)

    slot(0, proj_b, proj_a, mixed_b, mixed_a, kwin_b, kwin_a, vwin_b, vwin_a, False)
    slot(1, proj_a, proj_b, mixed_a, mixed_b, kwin_a, kwin_b, vwin_a, vwin_b,
         (2 * i) % tiles_per_seq == 0)


def _mix_call(layer, x, ada, g1, w_in, gq, gk, sinks, rg, w_out, tables, w_mlp1, w_mlp2):
    b, s, d = x.shape
    tiles_per_seq = s // MIX_ROWS
    pairs_per_seq = tiles_per_seq // 2
    n_pairs = b * pairs_per_seq
    bias_tab, ret_tab, cdec = tables
    x_pairs = x.reshape(n_pairs, 2, MIX_ROWS, d)

    nxt = lambda i: jnp.minimum(i, n_pairs - 1)
    res = lambda i: jnp.maximum(i - 1, 0)
    const2 = lambda i: (0, 0)
    const4 = lambda i: (0, 0, 0, 0)
    single = pl.Buffered(1)
    smem = pl.BlockSpec(memory_space=pltpu.SMEM)
    slab1, slab2 = w_mlp1.shape[1] // n_pairs, w_mlp2.shape[1] // n_pairs
    lay3 = lambda i: (layer, 0, 0)
    out, w1_b, w2_b = pl.pallas_call(
        functools.partial(_mix_kernel, layer, tiles_per_seq, n_pairs),
        grid=(n_pairs + 1,),
        in_specs=[
            pl.BlockSpec((None, 2, MIX_ROWS, d), lambda i: (nxt(i), 0, 0, 0)),
            pl.BlockSpec((None,) + ada.shape[1:], lambda i: (layer, 0, 0)),
            pl.BlockSpec(g1.shape, const2),
            pl.BlockSpec((None, d, IN_WIDTH), lay3, pipeline_mode=single),
            pl.BlockSpec(gq.shape, const2),
            pl.BlockSpec(gk.shape, const2),
            smem,
            pl.BlockSpec(rg.shape, const2),
            pl.BlockSpec((None, d, d), lay3, pipeline_mode=single),
            pl.BlockSpec(bias_tab.shape, const4, pipeline_mode=single),
            pl.BlockSpec(ret_tab.shape, const4, pipeline_mode=single),
            smem,
            pl.BlockSpec((None, slab1, D_FF), lambda i: (layer, nxt(i), 0)),
            pl.BlockSpec((None, slab2, d), lambda i: (layer, nxt(i), 0)),
        ],
        out_specs=[
            pl.BlockSpec((None, 2, MIX_ROWS, d), lambda i: (res(i), 0, 0, 0)),
            pl.BlockSpec((slab1, D_FF), lambda i: (nxt(i), 0)),
            pl.BlockSpec((slab2, d), lambda i: (nxt(i), 0)),
        ],
        out_shape=[
            jax.ShapeDtypeStruct((n_pairs, 2, MIX_ROWS, d), F32),
            jax.ShapeDtypeStruct(w_mlp1.shape[1:], BF16),
            jax.ShapeDtypeStruct(w_mlp2.shape[1:], BF16),
        ],
        scratch_shapes=[
            pltpu.VMEM((MIX_ROWS, IN_WIDTH), F32),
            pltpu.VMEM((MIX_ROWS, IN_WIDTH), F32),
            pltpu.VMEM((MIX_ROWS, d), BF16),
            pltpu.VMEM((MIX_ROWS, d), BF16),
            pltpu.VMEM((MIX_ROWS, d), BF16),
            pltpu.VMEM((2, MIX_ROWS, d), F32),
            pltpu.VMEM((N_KV_HEADS, BLOCK + MIX_ROWS, 2 * HEAD_DIM), BF16),
            pltpu.VMEM((N_KV_HEADS, BLOCK + MIX_ROWS, 2 * HEAD_DIM), BF16),
            pltpu.VMEM((2 * HEAD_DIM, BLOCK + MIX_ROWS), BF16),
            pltpu.VMEM((2 * HEAD_DIM, BLOCK + MIX_ROWS), BF16),
            pltpu.VMEM((RET_HEADS, 128, 128), F32),
            pltpu.VMEM((d, IN_WIDTH), BF16),
            pltpu.VMEM((d, d), BF16),
        ],
        compiler_params=pltpu.CompilerParams(
            dimension_semantics=("arbitrary",), vmem_limit_bytes=VMEM_LIMIT),
        name="mix",
    )(x_pairs, ada, g1, w_in, gq, gk, sinks, rg, w_out, bias_tab, ret_tab, cdec,
      w_mlp1, w_mlp2)
    return out.reshape(b, s, d), w1_b, w2_b


MLP_ROWS = 1024


def _mlp_kernel(layer, x_ref, ada_ref, g2_ref, w1_ref, w2_ref, o_ref):
    x = x_ref[...]
    ms = jnp.mean(x * x, axis=-1, keepdims=True)
    h = x * lax.rsqrt(ms + EPS) * g2_ref[layer:layer + 1, :]
    row = pl.ds(pl.program_id(0), 1)
    shift, scale, gate = (ada_ref[row, k * D_MODEL:(k + 1) * D_MODEL] for k in (3, 4, 5))
    h = (h * (1.0 + scale) + shift).astype(BF16)
    acc = None
    for c in range(D_FF // FF_CHUNK):
        cols = slice(c * FF_CHUNK, (c + 1) * FF_CHUNK)
        a = jnp.dot(h, w1_ref[:, cols], preferred_element_type=F32)
        a = jnp.maximum(a, 0.0)
        part = jnp.dot((a * a).astype(BF16), w2_ref[cols, :], preferred_element_type=F32)
        acc = part if acc is None else acc + part
    o_ref[...] = x + gate * acc


def _mlp_call(layer, x, ada, g2, w1, w2):
    b, s, d = x.shape
    const2 = lambda i, j: (0, 0)
    single = pl.Buffered(1)
    return pl.pallas_call(
        functools.partial(_mlp_kernel, layer),
        grid=(b, s // MLP_ROWS),
        in_specs=[
            pl.BlockSpec((None, MLP_ROWS, d), lambda i, j: (i, j, 0)),
            pl.BlockSpec((None,) + ada.shape[1:], lambda i, j: (layer, 0, 0)),
            pl.BlockSpec(g2.shape, const2),
            pl.BlockSpec((d, D_FF), const2, pipeline_mode=single),
            pl.BlockSpec((D_FF, d), const2, pipeline_mode=single),
        ],
        out_specs=pl.BlockSpec((None, MLP_ROWS, d), lambda i, j: (i, j, 0)),
        out_shape=jax.ShapeDtypeStruct((b, s, d), F32),
        compiler_params=pltpu.CompilerParams(
            dimension_semantics=("arbitrary", "arbitrary"), vmem_limit_bytes=VMEM_LIMIT),
        name="mlp",
    )(x, ada, g2, w1, w2)


def kernel(x, c, norm1_g, norm2_g, w_ada, b_ada, w_in, q_norm_g, k_norm_g, sinks, ret_norm_g,
           w_out, w_mlp1, w_mlp2):
    tables = tuple(jnp.asarray(t) for t in _constant_tables())
    ada = _ada_call(c, w_ada, b_ada)
    for l in range(DEPTH):
        x, w1_b, w2_b = _mix_call(l, x, ada, norm1_g, w_in, q_norm_g, k_norm_g, sinks, ret_norm_g,
                                  w_out, tables, w_mlp1, w_mlp2)
        x = _mlp_call(l, x, ada, norm2_g, w1_b, w2_b)
    return x
```

```python
import functools

import numpy as np
import jax
import jax.numpy as jnp
from jax import lax
from jax.experimental import pallas as pl
from jax.experimental.pallas import tpu as pltpu

D_MODEL = 1024
DEPTH = 2
ATTN_WIDTH = 512
RET_WIDTH = 512
HEAD_DIM = 64
N_Q_HEADS = 8
N_KV_HEADS = 2
N_PAIRS = N_Q_HEADS // 2
BLOCK = 128
RET_HEADS = 4
RET_DK = 128
D_FF = 4 * D_MODEL
EPS = 1e-6
NEG_INF = -1e30
LOG2E = 1.4426950408889634
IN_WIDTH = 2816
OFF_AQ, OFF_AK, OFF_AV, OFF_RQ, OFF_RK, OFF_RV, OFF_RG = 0, 512, 640, 768, 1280, 1792, 2304

MIX_ROWS = 256
PROJ_CHUNK = 256
FF_CHUNK = 1024
ADA_COLS = 3072
VMEM_LIMIT = 60 * 1024 * 1024

F32 = jnp.float32
BF16 = jnp.bfloat16


@functools.lru_cache(maxsize=None)
def _constant_tables():
    q_pos = np.arange(BLOCK)[:, None]
    k_pos = np.arange(2 * BLOCK)[None, :]
    dist = q_pos + BLOCK - k_pos
    valid = (dist >= 0) & (dist < BLOCK)
    valid_first = valid & (k_pos >= BLOCK)
    slopes = np.exp2(-8.0 * np.arange(1, N_Q_HEADS + 1, dtype=np.float64) / N_Q_HEADS)
    bias = -slopes[:, None, None] * dist[None].astype(np.float64)
    bias_tab = np.stack([np.where(v[None], bias * LOG2E, NEG_INF) for v in (valid_first, valid)])
    bias_tab = np.ascontiguousarray(bias_tab.transpose(0, 1, 3, 2), np.float32)

    log_gamma = np.log1p(-np.exp2(-5.0 - np.arange(RET_HEADS, dtype=np.float64)))
    idx = np.arange(BLOCK, dtype=np.float64)
    rel = idx[:, None] - idx[None, :]
    k_scale = RET_DK ** -0.5
    decay_in = np.where(rel >= 0, np.exp(log_gamma[:, None, None] * np.maximum(rel, 0.0)), 0.0) * k_scale
    q_decay = np.exp(log_gamma[:, None] * (idx[None, :] + 1.0))
    k_decay = np.exp(log_gamma[:, None] * (BLOCK - 1.0 - idx[None, :])) * k_scale
    q_decay = np.broadcast_to(q_decay[:, :, None], (RET_HEADS, BLOCK, BLOCK))
    k_decay = np.broadcast_to(k_decay[:, :, None], (RET_HEADS, BLOCK, BLOCK))
    chunk_decay = np.exp(log_gamma * BLOCK)
    ret_tab = np.stack([decay_in, q_decay, k_decay]).astype(np.float32)
    return bias_tab, ret_tab, chunk_decay.astype(np.float32)


def _ada_kernel(c_ref, w_ref, b_ref, o_ref):
    c = c_ref[...]
    c_act = c * (1.0 / (1.0 + jnp.exp(-c)))
    o_ref[...] = jnp.dot(c_act.astype(BF16), w_ref[...].astype(BF16),
                         preferred_element_type=F32) + b_ref[pl.ds(pl.program_id(0), 1), :]


def _ada_call(c, w_ada, b_ada):
    rows = c.shape[0]
    n = w_ada.shape[-1]
    return pl.pallas_call(
        _ada_kernel,
        grid=(DEPTH, n // ADA_COLS),
        in_specs=[
            pl.BlockSpec((rows, D_MODEL), lambda l, j: (0, 0)),
            pl.BlockSpec((None, D_MODEL, ADA_COLS), lambda l, j: (l, 0, j)),
            pl.BlockSpec((DEPTH, ADA_COLS), lambda l, j: (0, j)),
        ],
        out_specs=pl.BlockSpec((None, rows, ADA_COLS), lambda l, j: (l, 0, j)),
        out_shape=jax.ShapeDtypeStruct((DEPTH, rows, n), F32),
        compiler_params=pltpu.CompilerParams(
            dimension_semantics=("arbitrary", "arbitrary"), vmem_limit_bytes=VMEM_LIMIT),
        name="ada",
    )(c, w_ada, b_ada)


def _head_norm(a, gain_row):
    lo = lax.broadcasted_iota(jnp.int32, a.shape, 1) < HEAD_DIM
    a2 = a * a
    s_lo = jnp.sum(jnp.where(lo, a2, 0.0), axis=-1, keepdims=True)
    s_hi = jnp.sum(jnp.where(lo, 0.0, a2), axis=-1, keepdims=True)
    mean_sq = jnp.where(lo, s_lo, s_hi) * (1.0 / HEAD_DIM)
    return a * lax.rsqrt(mean_sq + EPS) * gain_row


def _pair_gain(g_ref, layer):
    row = g_ref[layer:layer + 1, :]
    return jnp.concatenate([row, row], axis=1)


def _build_windows(proj_ref, gk_row, kwin_ref, kwin_prev, vwin_ref, vwin_prev):
    kn = _head_norm(proj_ref[:, OFF_AK:OFF_AK + 128], gk_row)
    kn_rot = pltpu.roll(kn, HEAD_DIM, axis=1)
    lo = lax.broadcasted_iota(jnp.int32, kn.shape, 1) < HEAD_DIM
    k_dup = (jnp.where(lo, kn, kn_rot), jnp.where(lo, kn_rot, kn))
    for g in range(N_KV_HEADS):
        kwin_ref[g, 0:BLOCK, :] = kwin_prev[g, MIX_ROWS:MIX_ROWS + BLOCK, :]
        kwin_ref[g, BLOCK:BLOCK + MIX_ROWS, :] = k_dup[g].astype(BF16)
    vwin_ref[:, 0:BLOCK] = vwin_prev[:, MIX_ROWS:MIX_ROWS + BLOCK]
    vwin_ref[:, BLOCK:BLOCK + MIX_ROWS] = proj_ref[:, OFF_AV:OFF_AV + 128].T.astype(BF16)


def _mix_slot(chunks, proj_ref, mixed_ref, kwin_ref, vwin_ref,
              seq_start, layer, gq_ref, sinks_ref, rg_ref,
              bias_ref, ret_ref, cdec_ref, s_ref):
    first_tab = 1 if seq_start is False else jnp.where(seq_start, 0, 1)
    chunks = list(chunks)
    n_chunks = len(chunks)
    n_points = 5 + 5 * (MIX_ROWS // BLOCK)
    point = [0]

    def issue_chunk():
        point[0] += 1
        while chunks and (n_chunks - len(chunks)) * n_points < point[0] * n_chunks:
            chunks.pop(0)()

    issue_chunk()

    lo = lax.broadcasted_iota(jnp.int32, (BLOCK, 2 * HEAD_DIM), 1) < HEAD_DIM
    gq = _pair_gain(gq_ref, layer) * (HEAD_DIM ** -0.5 * LOG2E)
    n_blocks = MIX_ROWS // BLOCK
    rows = [pl.ds(j * BLOCK, BLOCK) for j in range(n_blocks)]
    wins = [slice(j * BLOCK, (j + 2) * BLOCK) for j in range(n_blocks)]
    tabs = [first_tab if j == 0 else 1 for j in range(n_blocks)]

    s_t = {}
    for j in range(n_blocks):
        for g in range(N_KV_HEADS):
            qs = []
            for pp in range(2):
                p = 2 * g + pp
                qn = _head_norm(proj_ref[rows[j], OFF_AQ + 128 * p:OFF_AQ + 128 * p + 128], gq)
                qs.append(jnp.where(lo, qn, 0.0).astype(BF16))
                qs.append(jnp.where(lo, 0.0, qn).astype(BF16))
            s_t[j, g] = lax.dot_general(kwin_ref[g, wins[j], :], jnp.concatenate(qs, axis=0),
                                        (((1,), (1,)), ((), ())),
                                        preferred_element_type=F32)
    issue_chunk()
    sinks = [sinks_ref[layer, head] * LOG2E for head in range(N_Q_HEADS)]
    es, ms = {}, {}
    for j in range(n_blocks):
        for head in range(N_Q_HEADS):
            g, hh = divmod(head, 4)
            sh = s_t[j, g][:, 128 * hh:128 * hh + 128] + bias_ref[tabs[j], head]
            ms[j, head] = jnp.maximum(jnp.max(sh, axis=0, keepdims=True), sinks[head])
            es[j, head] = jnp.exp2(sh - ms[j, head]).astype(BF16)
            if head % 4 == 3:
                issue_chunk()
    ones = jnp.ones((HEAD_DIM, 2 * BLOCK), BF16)
    o_t = {}
    for j in range(n_blocks):
        for g in range(N_KV_HEADS):
            lhs = jnp.concatenate([vwin_ref[64 * g:64 * g + 64, wins[j]], ones], axis=0)
            p_t = jnp.concatenate([es[j, 4 * g + hh] for hh in range(4)], axis=1)
            o_t[j, g] = jnp.dot(lhs, p_t, preferred_element_type=F32)
    issue_chunk()
    for j in range(n_blocks):
        for p in range(N_PAIRS):
            g, pp = divmod(p, 2)
            halves = []
            for head in (2 * p, 2 * p + 1):
                cols = slice(128 * (head % 4), 128 * (head % 4) + 128)
                denom = o_t[j, g][HEAD_DIM:, cols] + jnp.exp2(sinks[head] - ms[j, head])
                halves.append(o_t[j, g][:HEAD_DIM, cols] / denom)
            o = jnp.concatenate(halves, axis=0).T
            mixed_ref[rows[j], 128 * p:128 * p + 128] = o.astype(BF16)
        issue_chunk()

    heads = {}
    for j in range(n_blocks):
        for hd in range(RET_HEADS):
            c0 = 128 * hd
            q_f = proj_ref[rows[j], OFF_RQ + c0:OFF_RQ + c0 + 128]
            k = proj_ref[rows[j], OFF_RK + c0:OFF_RK + c0 + 128]
            v = proj_ref[rows[j], OFF_RV + c0:OFF_RV + c0 + 128].astype(BF16)
            inner = lax.dot_general(q_f.astype(BF16), k.astype(BF16), (((1,), (1,)), ((), ())),
                                    preferred_element_type=F32)
            kd = (k * ret_ref[2, hd]).astype(BF16)
            kv = lax.dot_general(kd, v, (((0,), (0,)), ((), ())), preferred_element_type=F32)
            heads[j, hd] = (q_f, v, inner, kv)
    issue_chunk()
    outs = {}
    for j in range(n_blocks):
        for hd in range(RET_HEADS):
            q_f, v, inner, kv = heads[j, hd]
            state = s_ref[hd]
            if j == 0 and seq_start is not False:
                state = jnp.where(seq_start, 0.0, state)
            lhs = jnp.concatenate([(inner * ret_ref[0, hd]).astype(BF16), (q_f * ret_ref[1, hd]).astype(BF16)],
                                  axis=1)
            outs[j, hd] = jnp.dot(lhs, jnp.concatenate([v, state.astype(BF16)], axis=0),
                                  preferred_element_type=F32)
            s_ref[hd] = cdec_ref[hd] * state + kv
    issue_chunk()
    for j in range(n_blocks):
        for hd in range(RET_HEADS):
            c0 = 128 * hd
            o = outs[j, hd]
            gate = proj_ref[rows[j], OFF_RG + c0:OFF_RG + c0 + 128]
            mu = jnp.mean(o, axis=-1, keepdims=True)
            d = o - mu
            var = jnp.mean(d * d, axis=-1, keepdims=True)
            on = d * lax.rsqrt(var + EPS) * rg_ref[layer:layer + 1, c0:c0 + 128]
            y = gate * (1.0 / (1.0 + jnp.exp(-gate))) * on
            mixed_ref[rows[j], ATTN_WIDTH + c0:ATTN_WIDTH + c0 + 128] = y.astype(BF16)
            if hd % 2 == 1:
                issue_chunk()

    assert not chunks


def _mix_kernel(layer, tiles_per_seq, n_pairs,
                xn_ref, ada_ref, g1_ref, win_f32_ref, gq_ref, gk_ref, sinks_ref,
                rg_ref, wout_f32_ref, bias_ref, ret_ref, cdec_ref, w1_f32_ref, w2_f32_ref,
                o_ref, w1_bf16_ref, w2_bf16_ref,
                proj_a, proj_b, mixed_a, mixed_b, hn_ref, xres_ref, kwin_a, kwin_b, vwin_a, vwin_b, s_ref,
                win_ref, wout_ref):
    i = pl.program_id(0)
    pairs_per_seq = tiles_per_seq // 2
    ada_n = pl.ds(jnp.minimum(i, n_pairs - 1) // pairs_per_seq, 1)
    ada_r = pl.ds(jnp.maximum(i - 1, 0) // pairs_per_seq, 1)

    @pl.when(i == 0)
    def _():
        win_ref[...] = win_f32_ref[...].astype(BF16)
        wout_ref[...] = wout_f32_ref[...].astype(BF16)

    w1_bf16_ref[...] = w1_f32_ref[...].astype(BF16)
    w2_bf16_ref[...] = w2_f32_ref[...].astype(BF16)

    @pl.when(i == 0)
    def _():
        proj_b[...] = jnp.zeros_like(proj_b)
        mixed_a[...] = jnp.zeros_like(mixed_a)
        xres_ref[...] = jnp.zeros_like(xres_ref)
        kwin_a[...] = jnp.zeros_like(kwin_a)
        kwin_b[...] = jnp.zeros_like(kwin_b)
        vwin_a[...] = jnp.zeros_like(vwin_a)
        vwin_b[...] = jnp.zeros_like(vwin_b)
        s_ref[...] = jnp.zeros_like(s_ref)

    consts = (layer, gq_ref, sinks_ref, rg_ref, bias_ref, ret_ref, cdec_ref, s_ref)

    def slot(half, proj_cur, proj_next, mixed_cur, mixed_prev, kwin, kwin_next, vwin, vwin_next,
             seq_start):
        def out_chunk(c):
            cols = slice(c * PROJ_CHUNK, min((c + 1) * PROJ_CHUNK, D_MODEL))
            y = jnp.dot(mixed_prev[...], wout_ref[:, cols], preferred_element_type=F32)
            gate = ada_ref[ada_r, 2 * D_MODEL + cols.start:2 * D_MODEL + cols.stop]
            o_ref[half, :, cols] = xres_ref[half, :, cols] + gate * y
            xres_ref[half, :, cols] = xn_ref[half, :, cols]

        def norm_next():
            x = xn_ref[half]
            ms = jnp.mean(x * x, axis=-1, keepdims=True)
            h = x * lax.rsqrt(ms + EPS) * g1_ref[layer:layer + 1, :]
            shift, scale = ada_ref[ada_n, 0:D_MODEL], ada_ref[ada_n, D_MODEL:2 * D_MODEL]
            hn_ref[...] = (h * (1.0 + scale) + shift).astype(BF16)

        def in_chunk(c):
            cols = slice(c * PROJ_CHUNK, min((c + 1) * PROJ_CHUNK, IN_WIDTH))
            proj_next[:, cols] = jnp.dot(hn_ref[...], win_ref[:, cols], preferred_element_type=F32)
            if cols.start <= OFF_AK and OFF_AV + 2 * HEAD_DIM <= cols.stop:
                _build_windows(proj_next, _pair_gain(gk_ref, layer), kwin_next, kwin, vwin_next, vwin)

        def first_chunk():
            out_chunk(0)
            norm_next()
            out_chunk(1)

        chunks = [first_chunk]
        chunks += [functools.partial(out_chunk, c) for c in range(2, pl.cdiv(D_MODEL, PROJ_CHUNK))]
        chunks += [functools.partial(in_chunk, c) for c in range(pl.cdiv(IN_WIDTH, PROJ_CHUNK))]
        _mix_slot(chunks, proj_cur, mixed_cur, kwin, vwin, seq_start, *consts)

    slot(0, proj_b, proj_a, mixed_b, mixed_a, kwin_b, kwin_a, vwin_b, vwin_a, False)
    slot(1, proj_a, proj_b, mixed_a, mixed_b, kwin_a, kwin_b, vwin_a, vwin_b,
         (2 * i) % tiles_per_seq == 0)


def _mix_call(layer, x, ada, g1, w_in, gq, gk, sinks, rg, w_out, tables, w_mlp1, w_mlp2):
    b, s, d = x.shape
    tiles_per_seq = s // MIX_ROWS
    pairs_per_seq = tiles_per_seq // 2
    n_pairs = b * pairs_per_seq
    bias_tab, ret_tab, cdec = tables
    x_pairs = x.reshape(n_pairs, 2, MIX_ROWS, d)

    nxt = lambda i: jnp.minimum(i, n_pairs - 1)
    res = lambda i: jnp.maximum(i - 1, 0)
    const2 = lambda i: (0, 0)
    const4 = lambda i: (0, 0, 0, 0)
    single = pl.Buffered(1)
    smem = pl.BlockSpec(memory_space=pltpu.SMEM)
    slab1, slab2 = w_mlp1.shape[1] // n_pairs, w_mlp2.shape[1] // n_pairs
    lay3 = lambda i: (layer, 0, 0)
    out, w1_b, w2_b = pl.pallas_call(
        functools.partial(_mix_kernel, layer, tiles_per_seq, n_pairs),
        grid=(n_pairs + 1,),
        in_specs=[
            pl.BlockSpec((None, 2, MIX_ROWS, d), lambda i: (nxt(i), 0, 0, 0)),
            pl.BlockSpec((None,) + ada.shape[1:], lambda i: (layer, 0, 0)),
            pl.BlockSpec(g1.shape, const2),
            pl.BlockSpec((None, d, IN_WIDTH), lay3, pipeline_mode=single),
            pl.BlockSpec(gq.shape, const2),
            pl.BlockSpec(gk.shape, const2),
            smem,
            pl.BlockSpec(rg.shape, const2),
            pl.BlockSpec((None, d, d), lay3, pipeline_mode=single),
            pl.BlockSpec(bias_tab.shape, const4, pipeline_mode=single),
            pl.BlockSpec(ret_tab.shape, const4, pipeline_mode=single),
            smem,
            pl.BlockSpec((None, slab1, D_FF), lambda i: (layer, nxt(i), 0)),
            pl.BlockSpec((None, slab2, d), lambda i: (layer, nxt(i), 0)),
        ],
        out_specs=[
            pl.BlockSpec((None, 2, MIX_ROWS, d), lambda i: (res(i), 0, 0, 0)),
            pl.BlockSpec((slab1, D_FF), lambda i: (nxt(i), 0)),
            pl.BlockSpec((slab2, d), lambda i: (nxt(i), 0)),
        ],
        out_shape=[
            jax.ShapeDtypeStruct((n_pairs, 2, MIX_ROWS, d), F32),
            jax.ShapeDtypeStruct(w_mlp1.shape[1:], BF16),
            jax.ShapeDtypeStruct(w_mlp2.shape[1:], BF16),
        ],
        scratch_shapes=[
            pltpu.VMEM((MIX_ROWS, IN_WIDTH), F32),
            pltpu.VMEM((MIX_ROWS, IN_WIDTH), F32),
            pltpu.VMEM((MIX_ROWS, d), BF16),
            pltpu.VMEM((MIX_ROWS, d), BF16),
            pltpu.VMEM((MIX_ROWS, d), BF16),
            pltpu.VMEM((2, MIX_ROWS, d), F32),
            pltpu.VMEM((N_KV_HEADS, BLOCK + MIX_ROWS, 2 * HEAD_DIM), BF16),
            pltpu.VMEM((N_KV_HEADS, BLOCK + MIX_ROWS, 2 * HEAD_DIM), BF16),
            pltpu.VMEM((2 * HEAD_DIM, BLOCK + MIX_ROWS), BF16),
            pltpu.VMEM((2 * HEAD_DIM, BLOCK + MIX_ROWS), BF16),
            pltpu.VMEM((RET_HEADS, 128, 128), F32),
            pltpu.VMEM((d, IN_WIDTH), BF16),
            pltpu.VMEM((d, d), BF16),
        ],
        compiler_params=pltpu.CompilerParams(
            dimension_semantics=("arbitrary",), vmem_limit_bytes=VMEM_LIMIT),
        name="mix",
    )(x_pairs, ada, g1, w_in, gq, gk, sinks, rg, w_out, bias_tab, ret_tab, cdec,
      w_mlp1, w_mlp2)
    return out.reshape(b, s, d), w1_b, w2_b


MLP_ROWS = 1024


def _mlp_kernel(layer, x_ref, ada_ref, g2_ref, w1_ref, w2_ref, o_ref):
    x = x_ref[...]
    ms = jnp.mean(x * x, axis=-1, keepdims=True)
    h = x * lax.rsqrt(ms + EPS) * g2_ref[layer:layer + 1, :]
    row = pl.ds(pl.program_id(0), 1)
    shift, scale, gate = (ada_ref[row, k * D_MODEL:(k + 1) * D_MODEL] for k in (3, 4, 5))
    h = (h * (1.0 + scale) + shift).astype(BF16)
    acc = None
    for c in range(D_FF // FF_CHUNK):
        cols = slice(c * FF_CHUNK, (c + 1) * FF_CHUNK)
        a = jnp.dot(h, w1_ref[:, cols], preferred_element_type=F32)
        a = jnp.maximum(a, 0.0)
        part = jnp.dot((a * a).astype(BF16), w2_ref[cols, :], preferred_element_type=F32)
        acc = part if acc is None else acc + part
    o_ref[...] = x + gate * acc


def _mlp_call(layer, x, ada, g2, w1, w2):
    b, s, d = x.shape
    const2 = lambda i, j: (0, 0)
    single = pl.Buffered(1)
    return pl.pallas_call(
        functools.partial(_mlp_kernel, layer),
        grid=(b, s // MLP_ROWS),
        in_specs=[
            pl.BlockSpec((None, MLP_ROWS, d), lambda i, j: (i, j, 0)),
            pl.BlockSpec((None,) + ada.shape[1:], lambda i, j: (layer, 0, 0)),
            pl.BlockSpec(g2.shape, const2),
            pl.BlockSpec((d, D_FF), const2, pipeline_mode=single),
            pl.BlockSpec((D_FF, d), const2, pipeline_mode=single),
        ],
        out_specs=pl.BlockSpec((None, MLP_ROWS, d), lambda i, j: (i, j, 0)),
        out_shape=jax.ShapeDtypeStruct((b, s, d), F32),
        compiler_params=pltpu.CompilerParams(
            dimension_semantics=("arbitrary", "arbitrary"), vmem_limit_bytes=VMEM_LIMIT),
        name="mlp",
    )(x, ada, g2, w1, w2)


def kernel(x, c, norm1_g, norm2_g, w_ada, b_ada, w_in, q_norm_g, k_norm_g, sinks, ret_norm_g,
           w_out, w_mlp1, w_mlp2):
    tables = tuple(jnp.asarray(t) for t in _constant_tables())
    ada = _ada_call(c, w_ada, b_ada)
    for l in range(DEPTH):
        x, w1_b, w2_b = _mix_call(l, x, ada, norm1_g, w_in, q_norm_g, k_norm_g, sinks, ret_norm_g,
                                  w_out, tables, w_mlp1, w_mlp2)
        x = _mlp_call(l, x, ada, norm2_g, w1_b, w2_b)
    return x
```
